```python
import math
import jax, jax.numpy as jnp
from jax import lax
import numpy as np

D_MODEL = 1024
BATCH = 4
SEQ = 8192
DEPTH = 1

MLA_HEADS = 8
MLA_Q_LORA = 256
MLA_KV_LORA = 128
MLA_NOPE = 64
MLA_ROPE = 32
MLA_V = 64
SWA_HEADS = 8
SWA_KV_HEADS = 2
SWA_HEAD_DIM = 64
WINDOW = 128
BLOCK = 128
REL_BUCKETS = 32
REL_MAX_DIST = 128
ROPE_THETA = 10000.0
EPS = 1e-6

MLA_WIDTH = MLA_HEADS * MLA_V
SWA_WIDTH = SWA_HEADS * SWA_HEAD_DIM
MIX_WIDTH = MLA_WIDTH + SWA_WIDTH
IN_SPLITS = (MLA_Q_LORA, MLA_KV_LORA, MLA_ROPE,
             SWA_HEADS * SWA_HEAD_DIM, SWA_KV_HEADS * SWA_HEAD_DIM, SWA_KV_HEADS * SWA_HEAD_DIM,
             MIX_WIDTH)
IN_WIDTH = sum(IN_SPLITS)

kernel_name = "hymba_mla_swa_sink_gated"


def _offsets(sizes):
    out, acc = [], 0
    for s in sizes[:-1]:
        acc += s
        out.append(acc)
    return tuple(out)


def rmsnorm(x, g):
    xf = x.astype(jnp.float32)
    y = xf * lax.rsqrt(jnp.mean(xf * xf, axis=-1, keepdims=True) + EPS)
    return (y * g.astype(jnp.float32)).astype(x.dtype)


def rope(x, pos):
    half = x.shape[-1] // 2
    inv = ROPE_THETA ** (-jnp.arange(half, dtype=jnp.float32) / half)
    ang = pos.astype(jnp.float32)[..., None] * inv
    ang = ang.reshape(ang.shape[:2] + (1,) * (x.ndim - 3) + (half,))
    cos, sin = jnp.cos(ang), jnp.sin(ang)
    xf = x.astype(jnp.float32)
    x1, x2 = xf[..., :half], xf[..., half:]
    return jnp.concatenate([x1 * cos - x2 * sin, x2 * cos + x1 * sin], axis=-1).astype(x.dtype)


def t5_bucket(dist):
    n = jnp.maximum(dist, 0)
    max_exact = REL_BUCKETS // 2
    nf = jnp.maximum(n, 1).astype(jnp.float32)
    large = max_exact + (jnp.log(nf / max_exact) / math.log(REL_MAX_DIST / max_exact)
                         * (REL_BUCKETS - max_exact)).astype(jnp.int32)
    large = jnp.minimum(large, REL_BUCKETS - 1)
    return jnp.where(n < max_exact, n, large)


def mla_group(q_lat, kv_lat, k_rope, positions, q_a_norm, w_q_b, kv_a_norm, w_kv_b):
    B, S, _ = q_lat.shape
    q = (rmsnorm(q_lat, q_a_norm) @ w_q_b).reshape(B, S, MLA_HEADS, MLA_NOPE + MLA_ROPE)
    q_nope, q_pe = q[..., :MLA_NOPE], rope(q[..., MLA_NOPE:], positions)
    kv = (rmsnorm(kv_lat, kv_a_norm) @ w_kv_b).reshape(B, S, MLA_HEADS, MLA_NOPE + MLA_V)
    k_nope, v = kv[..., :MLA_NOPE], kv[..., MLA_NOPE:]
    k_pe = rope(k_rope, positions)
    scale = (MLA_NOPE + MLA_ROPE) ** -0.5
    nb = S // BLOCK
    qn_blocks = jnp.moveaxis(q_nope.reshape(B, nb, BLOCK, MLA_HEADS, MLA_NOPE), 1, 0)
    qp_blocks = jnp.moveaxis(q_pe.reshape(B, nb, BLOCK, MLA_HEADS, MLA_ROPE), 1, 0)
    key_idx = jnp.arange(S)

    def block_attn(args):
        i, qn, qp = args
        s = (jnp.einsum('bqhd,bkhd->bhqk', qn, k_nope)
             + jnp.einsum('bqhr,bkr->bhqk', qp, k_pe)).astype(jnp.float32) * scale
        q_idx = i * BLOCK + jnp.arange(BLOCK)
        mask = key_idx[None, :] <= q_idx[:, None]
        s = jnp.where(mask[None, None], s, -jnp.inf)
        p = jax.nn.softmax(s, axis=-1).astype(v.dtype)
        return jnp.einsum('bhqk,bkhd->bqhd', p, v)

    o = lax.map(block_attn, (jnp.arange(nb), qn_blocks, qp_blocks))
    return jnp.moveaxis(o, 0, 1).reshape(B, S, MLA_WIDTH)


def _band(t, nb):
    pad = [(0, 0), (BLOCK, 0)] + [(0, 0)] * (t.ndim - 2)
    tp = jnp.pad(t, pad).reshape((t.shape[0], nb + 1, BLOCK) + t.shape[2:])
    return jnp.concatenate([tp[:, :-1], tp[:, 1:]], axis=2)


def swa_group(q, k, v, positions, rel_bias, sinks):
    B, S, _ = q.shape
    G = SWA_HEADS // SWA_KV_HEADS
    nb = S // BLOCK
    qb = q.reshape(B, nb, BLOCK, SWA_KV_HEADS, G, SWA_HEAD_DIM)
    kb = _band(k.reshape(B, S, SWA_KV_HEADS, SWA_HEAD_DIM), nb)
    vb = _band(v.reshape(B, S, SWA_KV_HEADS, SWA_HEAD_DIM), nb)
    kpos = _band(positions, nb)
    qpos = positions.reshape(B, nb, BLOCK)
    qi = jnp.arange(BLOCK)[:, None] + BLOCK
    ki = jnp.arange(2 * BLOCK)[None, :]
    delta = qi - ki
    blk = jnp.arange(nb)[:, None, None]
    valid = (delta >= 0) & (delta < WINDOW) & (blk * BLOCK + ki[None] - BLOCK >= 0)
    bucket = t5_bucket(qpos[..., :, None] - kpos[..., None, :])
    bias = jnp.take(rel_bias, bucket, axis=0)
    bias = jnp.moveaxis(bias, -1, 2).reshape(B, nb, SWA_KV_HEADS, G, BLOCK, 2 * BLOCK)
    s = jnp.einsum('bnqhgd,bnkhd->bnhgqk', qb, kb).astype(jnp.float32) * (SWA_HEAD_DIM ** -0.5)
    s = s + bias.astype(jnp.float32)
    s = jnp.where(valid[None, :, None, None], s, -jnp.inf)
    sink = sinks.astype(jnp.float32).reshape(1, 1, SWA_KV_HEADS, G, 1, 1)
    m = jnp.maximum(jnp.max(s, axis=-1, keepdims=True), sink)
    e = jnp.exp(s - m)
    p = e / (jnp.sum(e, axis=-1, keepdims=True) + jnp.exp(sink - m))
    o = jnp.einsum('bnhgqk,bnkhd->bnqhgd', p.astype(vb.dtype), vb)
    return o.reshape(B, S, SWA_WIDTH)


def setup_inputs(seed: int = 0) -> dict:
    key = jax.random.key(seed)
    ks = jax.random.split(key, 16)
    f32 = jnp.float32
    x = jax.random.normal(ks[0], (BATCH, SEQ, D_MODEL), f32)
    offset = jax.random.randint(ks[1], (BATCH, 1), 0, 1024, dtype=jnp.int32)
    positions = (offset + jnp.arange(SEQ, dtype=jnp.int32)[None, :]).astype(jnp.int32)
    norm_gain = 1.0 + 0.05 * jax.random.normal(ks[2], (DEPTH, D_MODEL), f32)
    w_in = jax.random.normal(ks[3], (DEPTH, D_MODEL, IN_WIDTH), f32) * D_MODEL ** -0.5
    q_a_norm = 1.0 + 0.05 * jax.random.normal(ks[4], (DEPTH, MLA_Q_LORA), f32)
    w_q_b = jax.random.normal(ks[5], (DEPTH, MLA_Q_LORA, MLA_HEADS * (MLA_NOPE + MLA_ROPE)), f32) * MLA_Q_LORA ** -0.5
    kv_a_norm = 1.0 + 0.05 * jax.random.normal(ks[6], (DEPTH, MLA_KV_LORA), f32)
    w_kv_b = jax.random.normal(ks[7], (DEPTH, MLA_KV_LORA, MLA_HEADS * (MLA_NOPE + MLA_V)), f32) * MLA_KV_LORA ** -0.5
    sinks = jax.random.normal(ks[8], (DEPTH, SWA_HEADS), f32)
    rel_bias = 0.5 * jax.random.normal(ks[9], (REL_BUCKETS, SWA_HEADS), f32)
    w_out = jax.random.normal(ks[10], (DEPTH, MIX_WIDTH, D_MODEL), f32) * MIX_WIDTH ** -0.5
    final_norm = 1.0 + 0.05 * jax.random.normal(ks[11], (D_MODEL,), f32)
    return {"x": x, "positions": positions, "norm_gain": norm_gain, "w_in": w_in,
            "q_a_norm": q_a_norm, "w_q_b": w_q_b, "kv_a_norm": kv_a_norm, "w_kv_b": w_kv_b,
            "sinks": sinks, "rel_bias": rel_bias, "w_out": w_out, "final_norm": final_norm}


def reference(x, positions, norm_gain, w_in, q_a_norm, w_q_b, kv_a_norm, w_kv_b,
              sinks, rel_bias, w_out, final_norm):
    offs = _offsets(IN_SPLITS)
    for layer in range(DEPTH):
        h = rmsnorm(x, norm_gain[layer])
        proj = h @ w_in[layer]
        q_lat, kv_lat, k_rope, q_s, k_s, v_s, gate = jnp.split(proj, offs, axis=-1)
        o_a = mla_group(q_lat, kv_lat, k_rope, positions, q_a_norm[layer], w_q_b[layer],
                        kv_a_norm[layer], w_kv_b[layer])
        o_b = swa_group(q_s, k_s, v_s, positions, rel_bias, sinks[layer])
        y = jnp.concatenate([o_a, o_b], axis=-1) * jax.nn.silu(gate)
        x = x + y @ w_out[layer]
    return rmsnorm(x, final_norm)
```

```python
import functools
import math

import jax
import jax.numpy as jnp
from jax import lax
from jax.experimental import pallas as pl
from jax.experimental.pallas import tpu as pltpu

D_MODEL = 1024
MLA_HEADS = 8
MLA_Q_LORA = 256
MLA_KV_LORA = 128
MLA_NOPE = 64
MLA_ROPE = 32
MLA_V = 64
SWA_HEADS = 8
SWA_KV_HEADS = 2
SWA_HEAD_DIM = 64
WINDOW = 128
BLOCK = 128
REL_BUCKETS = 32
REL_MAX_DIST = 128
ROPE_THETA = 10000.0
EPS = 1e-6

MLA_QK = MLA_NOPE + MLA_ROPE
MLA_WIDTH = MLA_HEADS * MLA_V
SWA_WIDTH = SWA_HEADS * SWA_HEAD_DIM
SWA_KV_WIDTH = SWA_KV_HEADS * SWA_HEAD_DIM
MIX_WIDTH = MLA_WIDTH + SWA_WIDTH
ROPE_HALF = MLA_ROPE // 2

LANES = 128
HEAD_PAD = LANES
SEQ_TILE = 512
NEG_BIG = -1e30
VMEM_LIMIT = 56 * 1024 * 1024

_T_QLAT = 0
_T_KVLAT = _T_QLAT + MLA_Q_LORA
_T_QS = _T_KVLAT + MLA_KV_LORA
_T_VS = _T_QS + SWA_WIDTH
_T_GATE = _T_VS + SWA_KV_WIDTH
_T_ROWS = _T_GATE + MIX_WIDTH
_S_KVLAT = 0
_S_KS = LANES
_S_ROPE = 2 * LANES
_S_ROPE_ROT = 3 * LANES
_S_COLS = 4 * LANES


def _rms_rows(v, gain_col):
    ms = jnp.mean(v * v, axis=0, keepdims=True)
    return v * lax.rsqrt(ms + EPS) * gain_col


def _rms_lanes(v, gain_row):
    ms = jnp.mean(v * v, axis=-1, keepdims=True)
    return v * lax.rsqrt(ms + EPS) * gain_row


def _dot(a, b):
    return jnp.dot(a, b, preferred_element_type=jnp.float32)


def _dot_nt(a, b):
    return lax.dot_general(a, b, (((1,), (1,)), ((), ())), preferred_element_type=jnp.float32)


def _dot_tn(a, b):
    return lax.dot_general(a, b, (((0,), (0,)), ((), ())), preferred_element_type=jnp.float32)


def _proj_kernel(x_ref, prow_ref, pcol_ref, g1_ref, wt_ref, ws_ref, gq_ref, gkvc_ref, gkvr_ref,
                 wqt_ref, wvt_ref, wk_ref, invc_ref, invr_ref,
                 qt_ref, k_ref, vt_ref, qst_ref, ks_ref, vst_ref, gt_ref):
    bf = jnp.bfloat16
    x = x_ref[0]
    h = _rms_lanes(x, g1_ref[...]).astype(bf)
    pt = _dot_nt(wt_ref[...], h)
    ps = _dot(h, ws_ref[...])

    qn = _rms_rows(pt[_T_QLAT:_T_KVLAT], gq_ref[...]).astype(bf)
    qt = _dot(wqt_ref[...], qn) * (MLA_QK ** -0.5)
    ang_t = invc_ref[...] * prow_ref[0].astype(jnp.float32)
    cos_t, sin_t = jnp.cos(ang_t), jnp.sin(ang_t)
    for hd in range(MLA_HEADS):
        r0 = hd * HEAD_PAD
        x1 = qt[r0 + MLA_NOPE:r0 + MLA_NOPE + ROPE_HALF]
        x2 = qt[r0 + MLA_NOPE + ROPE_HALF:r0 + MLA_QK]
        qt_ref[0, 0, r0:r0 + MLA_NOPE, :] = qt[r0:r0 + MLA_NOPE].astype(bf)
        qt_ref[0, 0, r0 + MLA_NOPE:r0 + MLA_NOPE + ROPE_HALF, :] = (x1 * cos_t - x2 * sin_t).astype(bf)
        qt_ref[0, 0, r0 + MLA_NOPE + ROPE_HALF:r0 + MLA_QK, :] = (x2 * cos_t + x1 * sin_t).astype(bf)
        qt_ref[0, 0, r0 + MLA_QK:r0 + HEAD_PAD, :] = qt[r0 + MLA_QK:r0 + HEAD_PAD].astype(bf)

    kvn_t = _rms_rows(pt[_T_KVLAT:_T_QS], gkvc_ref[...]).astype(bf)
    vt_ref[0, 0] = _dot(wvt_ref[...], kvn_t).astype(bf)
    kvn_s = _rms_lanes(ps[:, _S_KVLAT:_S_KS], gkvr_ref[...]).astype(bf)
    kn = _dot(kvn_s, wk_ref[...])
    ang_s = pcol_ref[0].astype(jnp.float32) * invr_ref[...]
    kpe = ps[:, _S_ROPE:_S_ROPE_ROT] * jnp.cos(ang_s) + ps[:, _S_ROPE_ROT:_S_COLS] * jnp.sin(ang_s)
    for hd in range(MLA_HEADS):
        k_ref[0, hd] = (kn[:, hd * HEAD_PAD:(hd + 1) * HEAD_PAD] + kpe).astype(bf)

    qst_ref[0, 0] = (pt[_T_QS:_T_VS] * (SWA_HEAD_DIM ** -0.5)).astype(bf)
    ks_ref[0] = ps[:, _S_KS:_S_ROPE].astype(bf)
    vst_ref[0, 0] = pt[_T_VS:_T_GATE].astype(bf)
    gt_ref[0, 0] = pt[_T_GATE:_T_ROWS].astype(bf)


def _mla_kernel(qt_ref, k_ref, vt_ref, ot_ref):
    bf = jnp.bfloat16
    qi = pl.program_id(2)
    qt = qt_ref[0, 0]
    tq = qt.shape[1]

    def tile(j, masked):
        kj = k_ref[0, 0, pl.ds(pl.multiple_of(j * SEQ_TILE, SEQ_TILE), SEQ_TILE), :]
        st = _dot(kj, qt)
        if masked:
            key = lax.broadcasted_iota(jnp.int32, st.shape, 0)
            qry = lax.broadcasted_iota(jnp.int32, st.shape, 1)
            st = jnp.where(key <= qry, st, NEG_BIG)
        return st, vt_ref[0, j]

    st, vj = tile(qi, True)
    m0 = jnp.max(st, axis=0, keepdims=True)
    p = jnp.exp(st - m0)
    l0 = jnp.sum(p, axis=0, keepdims=True)
    acc0 = _dot(vj, p.astype(bf))

    def body(j, carry):
        m, l, acc = carry
        st, vj = tile(j, False)
        m_new = jnp.maximum(m, jnp.max(st, axis=0, keepdims=True))
        alpha = jnp.exp(m - m_new)
        p = jnp.exp(st - m_new)
        l = alpha * l + jnp.sum(p, axis=0, keepdims=True)
        acc = alpha * acc + _dot(vj, p.astype(bf))
        return m_new, l, acc

    _, l, acc = lax.fori_loop(0, qi, body, (m0, l0, acc0))
    ot_ref[0, 0] = (acc / l).astype(bf)


def _swa_kernel(bias_ref, sink_ref, qst_ref, kp_ref, kc_ref, vp_ref, vc_ref,
                qpos_ref, kpp_ref, kpc_ref, ot_ref):
    bf = jnp.bfloat16
    blk = pl.program_id(1)
    groups = SWA_HEADS // SWA_KV_HEADS
    kband = jnp.concatenate([kp_ref[0], kc_ref[0]], axis=0)
    vband = jnp.concatenate([vp_ref[0, 0], vc_ref[0, 0]], axis=1)
    kpos = jnp.concatenate([kpp_ref[0], kpc_ref[0]], axis=0)
    qpos = qpos_ref[0]

    shape = (2 * BLOCK, BLOCK)
    n = jnp.maximum(qpos - kpos, 0)
    max_exact = REL_BUCKETS // 2
    nf = jnp.maximum(n, 1).astype(jnp.float32)
    large = max_exact + (jnp.log(nf / max_exact) / math.log(REL_MAX_DIST / max_exact)
                         * (REL_BUCKETS - max_exact)).astype(jnp.int32)
    bucket = jnp.where(n < max_exact, n, jnp.minimum(large, REL_BUCKETS - 1))

    ki = lax.broadcasted_iota(jnp.int32, shape, 0)
    qi = lax.broadcasted_iota(jnp.int32, shape, 1)
    delta = qi + BLOCK - ki
    valid = (delta >= 0) & (delta < WINDOW) & (blk * BLOCK + ki - BLOCK >= 0)

    for kvh in range(SWA_KV_HEADS):
        d0 = kvh * SWA_HEAD_DIM
        k_h = kband[:, d0:d0 + SWA_HEAD_DIM]
        v_h = vband[d0:d0 + SWA_HEAD_DIM, :]
        qcat = jnp.concatenate(
            [qst_ref[0, 0, (kvh * groups + g) * SWA_HEAD_DIM:(kvh * groups + g + 1) * SWA_HEAD_DIM, :]
             for g in range(groups)], axis=1)
        st = _dot(k_h, qcat)
        for g in range(groups):
            hd = kvh * groups + g
            table = jnp.broadcast_to(bias_ref[hd:hd + 1, :], shape)
            bias = jnp.take_along_axis(table, bucket, axis=1, mode="promise_in_bounds")
            s = jnp.where(valid, st[:, g * BLOCK:(g + 1) * BLOCK] + bias, NEG_BIG)
            sink = sink_ref[hd]
            m = jnp.maximum(jnp.max(s, axis=0, keepdims=True), sink)
            e = jnp.exp(s - m)
            denom = jnp.sum(e, axis=0, keepdims=True) + jnp.exp(sink - m)
            p = (e / denom).astype(bf)
            ot_ref[0, 0, hd * SWA_HEAD_DIM:(hd + 1) * SWA_HEAD_DIM, :] = _dot(v_h, p).astype(bf)


def _out_kernel(oat_ref, obt_ref, gt_ref, x_ref, wo_ref, fn_ref, o_ref):
    g = gt_ref[0, 0].astype(jnp.float32)
    mix = jnp.concatenate([oat_ref[0, 0], obt_ref[0, 0]], axis=0).astype(jnp.float32)
    yt = (mix * (g * jax.nn.sigmoid(g))).astype(jnp.bfloat16)
    z = x_ref[0] + _dot_tn(yt, wo_ref[...])
    o_ref[0] = _rms_lanes(z, fn_ref[...])


def _const_spec(shape):
    return pl.BlockSpec(shape, lambda *_: (0,) * len(shape))


def _params(n_axes):
    return pltpu.CompilerParams(dimension_semantics=("arbitrary",) * n_axes,
                                vmem_limit_bytes=VMEM_LIMIT)


def kernel(x, positions, norm_gain, w_in, q_a_norm, w_q_b, kv_a_norm, w_kv_b, sinks, rel_bias, w_out,
           final_norm):
    bf, f32 = jnp.bfloat16, jnp.float32
    batch, seq, _ = x.shape
    assert norm_gain.shape[0] == 1, "single-layer kernel"
    tm = SEQ_TILE
    nt = seq // tm
    nb = seq // BLOCK
    per = tm // BLOCK

    w = w_in[0]
    c = 0
    w_qlat, c = w[:, c:c + MLA_Q_LORA], c + MLA_Q_LORA
    w_kvlat, c = w[:, c:c + MLA_KV_LORA], c + MLA_KV_LORA
    w_krope, c = w[:, c:c + MLA_ROPE], c + MLA_ROPE
    w_qs, c = w[:, c:c + SWA_WIDTH], c + SWA_WIDTH
    w_ks, c = w[:, c:c + SWA_KV_WIDTH], c + SWA_KV_WIDTH
    w_vs, c = w[:, c:c + SWA_KV_WIDTH], c + SWA_KV_WIDTH
    w_gate = w[:, c:c + MIX_WIDTH]
    wt = jnp.concatenate([w_qlat, w_kvlat, w_qs, w_vs, w_gate], axis=1).T.astype(bf)
    r1, r2 = w_krope[:, :ROPE_HALF], w_krope[:, ROPE_HALF:]
    zl = jnp.zeros((D_MODEL, MLA_NOPE), f32)
    zr = jnp.zeros((D_MODEL, HEAD_PAD - MLA_QK), f32)
    ws = jnp.concatenate([w_kvlat, w_ks, zl, r1, r2, zr, zl, -r2, r1, zr], axis=1).astype(bf)

    wq = w_q_b[0].T.reshape(MLA_HEADS, MLA_QK, MLA_Q_LORA)
    wqt = jnp.pad(wq, ((0, 0), (0, HEAD_PAD - MLA_QK), (0, 0))).reshape(MLA_HEADS * HEAD_PAD, MLA_Q_LORA)
    wkv = w_kv_b[0].reshape(MLA_KV_LORA, MLA_HEADS, MLA_NOPE + MLA_V)
    wk = jnp.pad(wkv[:, :, :MLA_NOPE], ((0, 0), (0, 0), (0, HEAD_PAD - MLA_NOPE)))
    wk = wk.reshape(MLA_KV_LORA, MLA_HEADS * HEAD_PAD).astype(bf)
    wvt = wkv[:, :, MLA_NOPE:].reshape(MLA_KV_LORA, MLA_WIDTH).T.astype(bf)
    wqt = wqt.astype(bf)

    inv = ROPE_THETA ** (-jnp.arange(ROPE_HALF, dtype=f32) / ROPE_HALF)
    inv_col = inv.reshape(ROPE_HALF, 1)
    inv_row = jnp.tile(inv, LANES // ROPE_HALF).reshape(1, LANES)
    pos_row = positions.reshape(batch, 1, seq)
    pos_col = positions.reshape(batch, seq, 1)

    tile4 = lambda rows: pl.BlockSpec((1, 1, rows, tm), lambda b, i: (b, i, 0, 0))
    qt, k, vt, qst, ks, vst, gt = pl.pallas_call(
        _proj_kernel,
        grid=(batch, nt),
        in_specs=[
            pl.BlockSpec((1, tm, D_MODEL), lambda b, i: (b, i, 0)),
            pl.BlockSpec((1, 1, tm), lambda b, i: (b, 0, i)),
            pl.BlockSpec((1, tm, 1), lambda b, i: (b, i, 0)),
            _const_spec((1, D_MODEL)),
            _const_spec((_T_ROWS, D_MODEL)),
            _const_spec((D_MODEL, _S_COLS)),
            _const_spec((MLA_Q_LORA, 1)),
            _const_spec((MLA_KV_LORA, 1)),
            _const_spec((1, MLA_KV_LORA)),
            _const_spec((MLA_HEADS * HEAD_PAD, MLA_Q_LORA)),
            _const_spec((MLA_WIDTH, MLA_KV_LORA)),
            _const_spec((MLA_KV_LORA, MLA_HEADS * HEAD_PAD)),
            _const_spec((ROPE_HALF, 1)),
            _const_spec((1, LANES)),
        ],
        out_specs=[
            tile4(MLA_HEADS * HEAD_PAD),
            pl.BlockSpec((1, MLA_HEADS, tm, HEAD_PAD), lambda b, i: (b, 0, i, 0)),
            tile4(MLA_WIDTH),
            tile4(SWA_WIDTH),
            pl.BlockSpec((1, tm, SWA_KV_WIDTH), lambda b, i: (b, i, 0)),
            tile4(SWA_KV_WIDTH),
            tile4(MIX_WIDTH),
        ],
        out_shape=[
            jax.ShapeDtypeStruct((batch, nt, MLA_HEADS * HEAD_PAD, tm), bf),
            jax.ShapeDtypeStruct((batch, MLA_HEADS, seq, HEAD_PAD), bf),
            jax.ShapeDtypeStruct((batch, nt, MLA_WIDTH, tm), bf),
            jax.ShapeDtypeStruct((batch, nt, SWA_WIDTH, tm), bf),
            jax.ShapeDtypeStruct((batch, seq, SWA_KV_WIDTH), bf),
            jax.ShapeDtypeStruct((batch, nt, SWA_KV_WIDTH, tm), bf),
            jax.ShapeDtypeStruct((batch, nt, MIX_WIDTH, tm), bf),
        ],
        compiler_params=_params(2),
        name="proj",
    )(x, pos_row, pos_col, norm_gain[0].reshape(1, D_MODEL), wt, ws,
      q_a_norm[0].reshape(MLA_Q_LORA, 1), kv_a_norm[0].reshape(MLA_KV_LORA, 1),
      kv_a_norm[0].reshape(1, MLA_KV_LORA), wqt, wvt, wk, inv_col, inv_row)

    oat = pl.pallas_call(
        _mla_kernel,
        grid=(batch, MLA_HEADS, nt),
        in_specs=[
            pl.BlockSpec((1, 1, HEAD_PAD, tm), lambda b, h, i: (b, i, h, 0)),
            pl.BlockSpec((1, 1, seq, HEAD_PAD), lambda b, h, i: (b, h, 0, 0)),
            pl.BlockSpec((1, nt, MLA_V, tm), lambda b, h, i: (b, 0, h, 0)),
        ],
        out_specs=pl.BlockSpec((1, 1, MLA_V, tm), lambda b, h, i: (b, i, h, 0)),
        out_shape=jax.ShapeDtypeStruct((batch, nt, MLA_WIDTH, tm), bf),
        compiler_params=_params(3),
        name="mla",
    )(qt, k, vt)

    prev = lambda i: jnp.maximum(i - 1, 0)
    bias_t = jnp.pad(rel_bias.T.astype(f32), ((0, 0), (0, LANES - REL_BUCKETS)))
    obt = pl.pallas_call(
        _swa_kernel,
        grid=(batch, nb),
        in_specs=[
            _const_spec((SWA_HEADS, LANES)),
            pl.BlockSpec(memory_space=pltpu.SMEM),
            pl.BlockSpec((1, 1, SWA_WIDTH, BLOCK), lambda b, i: (b, i // per, 0, i % per)),
            pl.BlockSpec((1, BLOCK, SWA_KV_WIDTH), lambda b, i: (b, prev(i), 0)),
            pl.BlockSpec((1, BLOCK, SWA_KV_WIDTH), lambda b, i: (b, i, 0)),
            pl.BlockSpec((1, 1, SWA_KV_WIDTH, BLOCK), lambda b, i: (b, prev(i) // per, 0, prev(i) % per)),
            pl.BlockSpec((1, 1, SWA_KV_WIDTH, BLOCK), lambda b, i: (b, i // per, 0, i % per)),
            pl.BlockSpec((1, 1, BLOCK), lambda b, i: (b, 0, i)),
            pl.BlockSpec((1, BLOCK, 1), lambda b, i: (b, prev(i), 0)),
            pl.BlockSpec((1, BLOCK, 1), lambda b, i: (b, i, 0)),
        ],
        out_specs=pl.BlockSpec((1, 1, SWA_WIDTH, BLOCK), lambda b, i: (b, i // per, 0, i % per)),
        out_shape=jax.ShapeDtypeStruct((batch, nt, SWA_WIDTH, tm), bf),
        compiler_params=_params(2),
        name="swa",
    )(bias_t, sinks[0].astype(f32), qst, ks, ks, vst, vst, pos_row, pos_col, pos_col)

    return pl.pallas_call(
        _out_kernel,
        grid=(batch, nt),
        in_specs=[
            tile4(MLA_WIDTH),
            tile4(SWA_WIDTH),
            tile4(MIX_WIDTH),
            pl.BlockSpec((1, tm, D_MODEL), lambda b, i: (b, i, 0)),
            _const_spec((MIX_WIDTH, D_MODEL)),
            _const_spec((1, D_MODEL)),
        ],
        out_specs=pl.BlockSpec((1, tm, D_MODEL), lambda b, i: (b, i, 0)),
        out_shape=jax.ShapeDtypeStruct((batch, seq, D_MODEL), x.dtype),
        compiler_params=_params(2),
        name="out",
    )(oat, obt, gt, x, w_out[0].astype(bf), final_norm.reshape(1, D_MODEL))
```

```python
import functools
import math

import jax
import jax.numpy as jnp
from jax import lax
from jax.experimental import pallas as pl
from jax.experimental.pallas import tpu as pltpu

D_MODEL = 1024
MLA_HEADS = 8
MLA_Q_LORA = 256
MLA_KV_LORA = 128
MLA_NOPE = 64
MLA_ROPE = 32
MLA_V = 64
SWA_HEADS = 8
SWA_KV_HEADS = 2
SWA_HEAD_DIM = 64
WINDOW = 128
BLOCK = 128
REL_BUCKETS = 32
REL_MAX_DIST = 128
ROPE_THETA = 10000.0
EPS = 1e-6

MLA_QK = MLA_NOPE + MLA_ROPE
MLA_WIDTH = MLA_HEADS * MLA_V
SWA_WIDTH = SWA_HEADS * SWA_HEAD_DIM
SWA_KV_WIDTH = SWA_KV_HEADS * SWA_HEAD_DIM
MIX_WIDTH = MLA_WIDTH + SWA_WIDTH
ROPE_HALF = MLA_ROPE // 2

LANES = 128
HEAD_PAD = LANES
BF16_ROWS = 16
V_EXT = MLA_V + BF16_ROWS
LOG2E = math.log2(math.e)
SEQ_TILE = 512
NEG_BIG = -1e30
VMEM_LIMIT = 56 * 1024 * 1024

_T_QLAT = 0
_T_KVLAT = _T_QLAT + MLA_Q_LORA
_T_QS = _T_KVLAT + MLA_KV_LORA
_T_VS = _T_QS + SWA_WIDTH
_T_GATE = _T_VS + SWA_KV_WIDTH
_T_ROWS = _T_GATE + MIX_WIDTH
_S_KVLAT = 0
_S_KS = LANES
_S_ROPE = 2 * LANES
_S_ROPE_ROT = 3 * LANES
_S_COLS = 4 * LANES


def _rms_rows(v, gain_col):
    ms = jnp.mean(v * v, axis=0, keepdims=True)
    return v * lax.rsqrt(ms + EPS) * gain_col


def _rms_lanes(v, gain_row):
    ms = jnp.mean(v * v, axis=-1, keepdims=True)
    return v * lax.rsqrt(ms + EPS) * gain_row


def _dot(a, b):
    return jnp.dot(a, b, preferred_element_type=jnp.float32)


def _dot_nt(a, b):
    return lax.dot_general(a, b, (((1,), (1,)), ((), ())), preferred_element_type=jnp.float32)


def _dot_tn(a, b):
    return lax.dot_general(a, b, (((0,), (0,)), ((), ())), preferred_element_type=jnp.float32)


def _proj_kernel(x_ref, prow_ref, pcol_ref, g1_ref, wt_ref, ws_ref, gq_ref, gkvc_ref, gkvr_ref,
                 wqt_ref, wvt_ref, wk_ref, invc_ref, invr_ref,
                 qt_ref, k_ref, vt_ref, qst_ref, ks_ref, vst_ref, gt_ref):
    bf = jnp.bfloat16
    x = x_ref[0]
    h = _rms_lanes(x, g1_ref[...]).astype(bf)
    pt = _dot_nt(wt_ref[...], h)
    ps = _dot(h, ws_ref[...])

    qn = _rms_rows(pt[_T_QLAT:_T_KVLAT], gq_ref[...]).astype(bf)
    qt = _dot(wqt_ref[...], qn) * (MLA_QK ** -0.5 * LOG2E)
    ang_t = invc_ref[...] * prow_ref[0].astype(jnp.float32)
    cos_t, sin_t = jnp.cos(ang_t), jnp.sin(ang_t)
    for hd in range(MLA_HEADS):
        r0 = hd * HEAD_PAD
        x1 = qt[r0 + MLA_NOPE:r0 + MLA_NOPE + ROPE_HALF]
        x2 = qt[r0 + MLA_NOPE + ROPE_HALF:r0 + MLA_QK]
        qt_ref[0, 0, r0:r0 + MLA_NOPE, :] = qt[r0:r0 + MLA_NOPE].astype(bf)
        qt_ref[0, 0, r0 + MLA_NOPE:r0 + MLA_NOPE + ROPE_HALF, :] = (x1 * cos_t - x2 * sin_t).astype(bf)
        qt_ref[0, 0, r0 + MLA_NOPE + ROPE_HALF:r0 + MLA_QK, :] = (x2 * cos_t + x1 * sin_t).astype(bf)
        qt_ref[0, 0, r0 + MLA_QK:r0 + HEAD_PAD, :] = qt[r0 + MLA_QK:r0 + HEAD_PAD].astype(bf)

    kvn_t = _rms_rows(pt[_T_KVLAT:_T_QS], gkvc_ref[...]).astype(bf)
    vt = _dot(wvt_ref[...], kvn_t).astype(bf)
    ones_row = (lax.broadcasted_iota(jnp.int32, (BF16_ROWS, vt.shape[1]), 0) == 0).astype(bf)
    for hd in range(MLA_HEADS):
        vt_ref[0, 0, hd * V_EXT:hd * V_EXT + MLA_V, :] = vt[hd * MLA_V:(hd + 1) * MLA_V]
        vt_ref[0, 0, hd * V_EXT + MLA_V:(hd + 1) * V_EXT, :] = ones_row
    kvn_s = _rms_lanes(ps[:, _S_KVLAT:_S_KS], gkvr_ref[...]).astype(bf)
    kn = _dot(kvn_s, wk_ref[...])
    ang_s = pcol_ref[0].astype(jnp.float32) * invr_ref[...]
    kpe = ps[:, _S_ROPE:_S_ROPE_ROT] * jnp.cos(ang_s) + ps[:, _S_ROPE_ROT:_S_COLS] * jnp.sin(ang_s)
    for hd in range(MLA_HEADS):
        k_ref[0, hd] = (kn[:, hd * HEAD_PAD:(hd + 1) * HEAD_PAD] + kpe).astype(bf)

    qst_ref[0, 0] = (pt[_T_QS:_T_VS] * (SWA_HEAD_DIM ** -0.5)).astype(bf)
    ks_ref[0] = ps[:, _S_KS:_S_ROPE].astype(bf)
    vst_ref[0, 0] = pt[_T_VS:_T_GATE].astype(bf)
    gt_ref[0, 0] = pt[_T_GATE:_T_ROWS].astype(bf)


def _mla_kernel(qt_ref, k_ref, vt_ref, ot_ref, s_scr, p_scr):
    bf = jnp.bfloat16
    qi = pl.program_id(2)
    qt = qt_ref[0, 0]
    tq = qt.shape[1]

    def scores(kidx):
        kj = k_ref[0, 0, pl.ds(pl.multiple_of(kidx * SEQ_TILE, SEQ_TILE), SEQ_TILE), :]
        return _dot(kj, qt)

    def probs(m, m_tile):
        m_new = jnp.maximum(m, m_tile)
        return m_new, jnp.exp2(m - m_new), jnp.exp2(s_scr[...] - m_new).astype(bf)

    def values(vidx, p, alpha, acc):
        return alpha * acc + _dot(vt_ref[0, vidx], p)

    st = scores(qi)
    key = lax.broadcasted_iota(jnp.int32, st.shape, 0)
    qry = lax.broadcasted_iota(jnp.int32, st.shape, 1)
    st = jnp.where(key <= qry, st, NEG_BIG)
    s_scr[...] = st
    p_scr[...] = jnp.zeros(p_scr.shape, bf)
    carry = (jnp.full((1, tq), NEG_BIG, jnp.float32),
             jnp.max(st, axis=0, keepdims=True),
             jnp.ones((1, tq), jnp.float32),
             jnp.zeros((V_EXT, tq), jnp.float32))

    def body(i, carry):
        m, m_tile, alpha_prev, acc = carry
        acc = values(jnp.where(i == 2, qi, jnp.maximum(i - 3, 0)), p_scr[...], alpha_prev, acc)
        m, alpha, p = probs(m, m_tile)
        p_scr[...] = p
        st = scores(i - 1)
        s_scr[...] = st
        return m, jnp.max(st, axis=0, keepdims=True), alpha, acc

    m, m_tile, alpha_prev, acc = lax.fori_loop(1, qi + 1, body, carry)
    acc = values(jnp.where(qi == 1, qi, jnp.maximum(qi - 2, 0)), p_scr[...], alpha_prev, acc)
    m, alpha, p = probs(m, m_tile)
    acc = values(jnp.maximum(qi - 1, 0), p, alpha, acc)
    ot_ref[0, 0] = (acc[:MLA_V] / acc[MLA_V:MLA_V + 1]).astype(bf)


def _swa_kernel(bias_ref, sink_ref, qst_ref, kp_ref, kc_ref, vp_ref, vc_ref,
                qpos_ref, kpp_ref, kpc_ref, ot_ref):
    bf = jnp.bfloat16
    blk = pl.program_id(1)
    groups = SWA_HEADS // SWA_KV_HEADS
    kband = jnp.concatenate([kp_ref[0], kc_ref[0]], axis=0)
    vband = jnp.concatenate([vp_ref[0, 0], vc_ref[0, 0]], axis=1)
    kpos = jnp.concatenate([kpp_ref[0], kpc_ref[0]], axis=0)
    qpos = qpos_ref[0]

    shape = (2 * BLOCK, BLOCK)
    n = jnp.maximum(qpos - kpos, 0)
    max_exact = REL_BUCKETS // 2
    nf = jnp.maximum(n, 1).astype(jnp.float32)
    large = max_exact + (jnp.log(nf / max_exact) / math.log(REL_MAX_DIST / max_exact)
                         * (REL_BUCKETS - max_exact)).astype(jnp.int32)
    bucket = jnp.where(n < max_exact, n, jnp.minimum(large, REL_BUCKETS - 1))

    ki = lax.broadcasted_iota(jnp.int32, shape, 0)
    qi = lax.broadcasted_iota(jnp.int32, shape, 1)
    delta = qi + BLOCK - ki
    valid = (delta >= 0) & (delta < WINDOW) & (blk * BLOCK + ki - BLOCK >= 0)

    for kvh in range(SWA_KV_HEADS):
        d0 = kvh * SWA_HEAD_DIM
        k_h = kband[:, d0:d0 + SWA_HEAD_DIM]
        v_h = vband[d0:d0 + SWA_HEAD_DIM, :]
        qcat = jnp.concatenate(
            [qst_ref[0, 0, (kvh * groups + g) * SWA_HEAD_DIM:(kvh * groups + g + 1) * SWA_HEAD_DIM, :]
             for g in range(groups)], axis=1)
        st = _dot(k_h, qcat)
        for g in range(groups):
            hd = kvh * groups + g
            table = jnp.broadcast_to(bias_ref[hd:hd + 1, :], shape)
            bias = jnp.take_along_axis(table, bucket, axis=1, mode="promise_in_bounds")
            s = jnp.where(valid, st[:, g * BLOCK:(g + 1) * BLOCK] + bias, NEG_BIG)
            sink = sink_ref[hd]
            m = jnp.maximum(jnp.max(s, axis=0, keepdims=True), sink)
            e = jnp.exp(s - m)
            denom = jnp.sum(e, axis=0, keepdims=True) + jnp.exp(sink - m)
            p = (e / denom).astype(bf)
            ot_ref[0, 0, hd * SWA_HEAD_DIM:(hd + 1) * SWA_HEAD_DIM, :] = _dot(v_h, p).astype(bf)


def _out_kernel(oat_ref, obt_ref, gt_ref, x_ref, wo_ref, fn_ref, o_ref):
    g = gt_ref[0, 0].astype(jnp.float32)
    mix = jnp.concatenate([oat_ref[0, 0], obt_ref[0, 0]], axis=0).astype(jnp.float32)
    yt = (mix * (g * jax.nn.sigmoid(g))).astype(jnp.bfloat16)
    z = x_ref[0] + _dot_tn(yt, wo_ref[...])
    o_ref[0] = _rms_lanes(z, fn_ref[...])


def _const_spec(shape):
    return pl.BlockSpec(shape, lambda *_: (0,) * len(shape))


def _params(n_axes):
    return pltpu.CompilerParams(dimension_semantics=("arbitrary",) * n_axes,
                                vmem_limit_bytes=VMEM_LIMIT)


def kernel(x, positions, norm_gain, w_in, q_a_norm, w_q_b, kv_a_norm, w_kv_b, sinks, rel_bias, w_out,
           final_norm):
    bf, f32 = jnp.bfloat16, jnp.float32
    batch, seq, _ = x.shape
    assert norm_gain.shape[0] == 1, "single-layer kernel"
    tm = SEQ_TILE
    nt = seq // tm
    nb = seq // BLOCK
    per = tm // BLOCK

    w = w_in[0]
    c = 0
    w_qlat, c = w[:, c:c + MLA_Q_LORA], c + MLA_Q_LORA
    w_kvlat, c = w[:, c:c + MLA_KV_LORA], c + MLA_KV_LORA
    w_krope, c = w[:, c:c + MLA_ROPE], c + MLA_ROPE
    w_qs, c = w[:, c:c + SWA_WIDTH], c + SWA_WIDTH
    w_ks, c = w[:, c:c + SWA_KV_WIDTH], c + SWA_KV_WIDTH
    w_vs, c = w[:, c:c + SWA_KV_WIDTH], c + SWA_KV_WIDTH
    w_gate = w[:, c:c + MIX_WIDTH]
    wt = jnp.concatenate([w_qlat, w_kvlat, w_qs, w_vs, w_gate], axis=1).T.astype(bf)
    r1, r2 = w_krope[:, :ROPE_HALF], w_krope[:, ROPE_HALF:]
    zl = jnp.zeros((D_MODEL, MLA_NOPE), f32)
    zr = jnp.zeros((D_MODEL, HEAD_PAD - MLA_QK), f32)
    ws = jnp.concatenate([w_kvlat, w_ks, zl, r1, r2, zr, zl, -r2, r1, zr], axis=1).astype(bf)

    wq = w_q_b[0].T.reshape(MLA_HEADS, MLA_QK, MLA_Q_LORA)
    wqt = jnp.pad(wq, ((0, 0), (0, HEAD_PAD - MLA_QK), (0, 0))).reshape(MLA_HEADS * HEAD_PAD, MLA_Q_LORA)
    wkv = w_kv_b[0].reshape(MLA_KV_LORA, MLA_HEADS, MLA_NOPE + MLA_V)
    wk = jnp.pad(wkv[:, :, :MLA_NOPE], ((0, 0), (0, 0), (0, HEAD_PAD - MLA_NOPE)))
    wk = wk.reshape(MLA_KV_LORA, MLA_HEADS * HEAD_PAD).astype(bf)
    wvt = wkv[:, :, MLA_NOPE:].reshape(MLA_KV_LORA, MLA_WIDTH).T.astype(bf)
    wqt = wqt.astype(bf)

    inv = ROPE_THETA ** (-jnp.arange(ROPE_HALF, dtype=f32) / ROPE_HALF)
    inv_col = inv.reshape(ROPE_HALF, 1)
    inv_row = jnp.tile(inv, LANES // ROPE_HALF).reshape(1, LANES)
    pos_row = positions.reshape(batch, 1, seq)
    pos_col = positions.reshape(batch, seq, 1)

    tile4 = lambda rows: pl.BlockSpec((1, 1, rows, tm), lambda b, i: (b, i, 0, 0))
    qt, k, vt, qst, ks, vst, gt = pl.pallas_call(
        _proj_kernel,
        grid=(batch, nt),
        in_specs=[
            pl.BlockSpec((1, tm, D_MODEL), lambda b, i: (b, i, 0)),
            pl.BlockSpec((1, 1, tm), lambda b, i: (b, 0, i)),
            pl.BlockSpec((1, tm, 1), lambda b, i: (b, i, 0)),
            _const_spec((1, D_MODEL)),
            _const_spec((_T_ROWS, D_MODEL)),
            _const_spec((D_MODEL, _S_COLS)),
            _const_spec((MLA_Q_LORA, 1)),
            _const_spec((MLA_KV_LORA, 1)),
            _const_spec((1, MLA_KV_LORA)),
            _const_spec((MLA_HEADS * HEAD_PAD, MLA_Q_LORA)),
            _const_spec((MLA_WIDTH, MLA_KV_LORA)),
            _const_spec((MLA_KV_LORA, MLA_HEADS * HEAD_PAD)),
            _const_spec((ROPE_HALF, 1)),
            _const_spec((1, LANES)),
        ],
        out_specs=[
            tile4(MLA_HEADS * HEAD_PAD),
            pl.BlockSpec((1, MLA_HEADS, tm, HEAD_PAD), lambda b, i: (b, 0, i, 0)),
            tile4(MLA_HEADS * V_EXT),
            tile4(SWA_WIDTH),
            pl.BlockSpec((1, tm, SWA_KV_WIDTH), lambda b, i: (b, i, 0)),
            tile4(SWA_KV_WIDTH),
            tile4(MIX_WIDTH),
        ],
        out_shape=[
            jax.ShapeDtypeStruct((batch, nt, MLA_HEADS * HEAD_PAD, tm), bf),
            jax.ShapeDtypeStruct((batch, MLA_HEADS, seq, HEAD_PAD), bf),
            jax.ShapeDtypeStruct((batch, nt, MLA_HEADS * V_EXT, tm), bf),
            jax.ShapeDtypeStruct((batch, nt, SWA_WIDTH, tm), bf),
            jax.ShapeDtypeStruct((batch, seq, SWA_KV_WIDTH), bf),
            jax.ShapeDtypeStruct((batch, nt, SWA_KV_WIDTH, tm), bf),
            jax.ShapeDtypeStruct((batch, nt, MIX_WIDTH, tm), bf),
        ],
        compiler_params=_params(2),
        name="proj",
    )(x, pos_row, pos_col, norm_gain[0].reshape(1, D_MODEL), wt, ws,
      q_a_norm[0].reshape(MLA_Q_LORA, 1), kv_a_norm[0].reshape(MLA_KV_LORA, 1),
      kv_a_norm[0].reshape(1, MLA_KV_LORA), wqt, wvt, wk, inv_col, inv_row)

    oat = pl.pallas_call(
        _mla_kernel,
        grid=(batch, MLA_HEADS, nt),
        in_specs=[
            pl.BlockSpec((1, 1, HEAD_PAD, tm), lambda b, h, i: (b, i, h, 0)),
            pl.BlockSpec((1, 1, seq, HEAD_PAD), lambda b, h, i: (b, h, 0, 0)),
            pl.BlockSpec((1, nt, V_EXT, tm), lambda b, h, i: (b, 0, h, 0)),
        ],
        out_specs=pl.BlockSpec((1, 1, MLA_V, tm), lambda b, h, i: (b, i, h, 0)),
        out_shape=jax.ShapeDtypeStruct((batch, nt, MLA_WIDTH, tm), bf),
        scratch_shapes=[pltpu.VMEM((tm, tm), f32), pltpu.VMEM((tm, tm), bf)],
        compiler_params=_params(3),
        name="mla",
    )(qt, k, vt)

    prev = lambda i: jnp.maximum(i - 1, 0)
    bias_t = jnp.pad(rel_bias.T.astype(f32), ((0, 0), (0, LANES - REL_BUCKETS)))
    obt = pl.pallas_call(
        _swa_kernel,
        grid=(batch, nb),
        in_specs=[
            _const_spec((SWA_HEADS, LANES)),
            pl.BlockSpec(memory_space=pltpu.SMEM),
            pl.BlockSpec((1, 1, SWA_WIDTH, BLOCK), lambda b, i: (b, i // per, 0, i % per)),
            pl.BlockSpec((1, BLOCK, SWA_KV_WIDTH), lambda b, i: (b, prev(i), 0)),
            pl.BlockSpec((1, BLOCK, SWA_KV_WIDTH), lambda b, i: (b, i, 0)),
            pl.BlockSpec((1, 1, SWA_KV_WIDTH, BLOCK), lambda b, i: (b, prev(i) // per, 0, prev(i) % per)),
            pl.BlockSpec((1, 1, SWA_KV_WIDTH, BLOCK), lambda b, i: (b, i // per, 0, i % per)),
            pl.BlockSpec((1, 1, BLOCK), lambda b, i: (b, 0, i)),
            pl.BlockSpec((1, BLOCK, 1), lambda b, i: (b, prev(i), 0)),
            pl.BlockSpec((1, BLOCK, 1), lambda b, i: (b, i, 0)),
        ],
        out_specs=pl.BlockSpec((1, 1, SWA_WIDTH, BLOCK), lambda b, i: (b, i // per, 0, i % per)),
        out_shape=jax.ShapeDtypeStruct((batch, nt, SWA_WIDTH, tm), bf),
        compiler_params=_params(2),
        name="swa",
    )(bias_t, sinks[0].astype(f32), qst, ks, ks, vst, vst, pos_row, pos_col, pos_col)

    return pl.pallas_call(
        _out_kernel,
        grid=(batch, nt),
        in_specs=[
            tile4(MLA_WIDTH),
            tile4(SWA_WIDTH),
            tile4(MIX_WIDTH),
            pl.BlockSpec((1, tm, D_MODEL), lambda b, i: (b, i, 0)),
            _const_spec((MIX_WIDTH, D_MODEL)),
            _const_spec((1, D_MODEL)),
        ],
        out_specs=pl.BlockSpec((1, tm, D_MODEL), lambda b, i: (b, i, 0)),
        out_shape=jax.ShapeDtypeStruct((batch, seq, D_MODEL), x.dtype),
        compiler_params=_params(2),
        name="out",
    )(oat, obt, gt, x, w_out[0].astype(bf), final_norm.reshape(1, D_MODEL))
```

```python
import functools
import math

import jax
import numpy as np
import jax.numpy as jnp
from jax import lax
from jax.experimental import pallas as pl
from jax.experimental.pallas import tpu as pltpu

D_MODEL = 1024
MLA_HEADS = 8
MLA_Q_LORA = 256
MLA_KV_LORA = 128
MLA_NOPE = 64
MLA_ROPE = 32
MLA_V = 64
SWA_HEADS = 8
SWA_KV_HEADS = 2
SWA_HEAD_DIM = 64
WINDOW = 128
BLOCK = 128
REL_BUCKETS = 32
REL_MAX_DIST = 128
ROPE_THETA = 10000.0
EPS = 1e-6

MLA_QK = MLA_NOPE + MLA_ROPE
MLA_WIDTH = MLA_HEADS * MLA_V
SWA_WIDTH = SWA_HEADS * SWA_HEAD_DIM
SWA_KV_WIDTH = SWA_KV_HEADS * SWA_HEAD_DIM
MIX_WIDTH = MLA_WIDTH + SWA_WIDTH
ROPE_HALF = MLA_ROPE // 2

LANES = 128
HEAD_PAD = LANES
BF16_ROWS = 16
V_EXT = MLA_V + BF16_ROWS
LOG2E = math.log2(math.e)
SEQ_TILE = 512
MLA_UNROLL = 4
MLA_STAGES = 3
NEG_BIG = -1e30
VMEM_LIMIT = 56 * 1024 * 1024

_T_QLAT = 0
_T_KVLAT = _T_QLAT + MLA_Q_LORA
_T_QS = _T_KVLAT + MLA_KV_LORA
_T_VS = _T_QS + SWA_WIDTH
_T_GATE = _T_VS + SWA_KV_WIDTH
_T_ROWS = _T_GATE + MIX_WIDTH
_S_KVLAT = 0
_S_KS = LANES
_S_ROPE = 2 * LANES
_S_ROPE_ROT = 3 * LANES
_S_COLS = 4 * LANES


def _rms_rows(v, gain_col):
    ms = jnp.mean(v * v, axis=0, keepdims=True)
    return v * lax.rsqrt(ms + EPS) * gain_col


def _rms_lanes(v, gain_row):
    ms = jnp.mean(v * v, axis=-1, keepdims=True)
    return v * lax.rsqrt(ms + EPS) * gain_row


def _dot(a, b):
    return jnp.dot(a, b, preferred_element_type=jnp.float32)


def _dot_nt(a, b):
    return lax.dot_general(a, b, (((1,), (1,)), ((), ())), preferred_element_type=jnp.float32)


def _dot_tn(a, b):
    return lax.dot_general(a, b, (((0,), (0,)), ((), ())), preferred_element_type=jnp.float32)


def _proj_kernel(x_ref, prow_ref, pcol_ref, g1_ref, wt_ref, ws_ref, gq_ref, gkvc_ref, gkvr_ref,
                 wqt_ref, wvt_ref, wk_ref, invc_ref, invr_ref,
                 qt_ref, k_ref, vt_ref, qst_ref, ks_ref, vst_ref, gt_ref):
    bf = jnp.bfloat16
    x = x_ref[0]
    h = _rms_lanes(x, g1_ref[...]).astype(bf)
    pt = _dot_nt(wt_ref[...], h)
    ps = _dot(h, ws_ref[...])

    qn = _rms_rows(pt[_T_QLAT:_T_KVLAT], gq_ref[...]).astype(bf)
    qt = _dot(wqt_ref[...], qn) * (MLA_QK ** -0.5 * LOG2E)
    ang_t = invc_ref[...] * prow_ref[0].astype(jnp.float32)
    cos_t, sin_t = jnp.cos(ang_t), jnp.sin(ang_t)
    for hd in range(MLA_HEADS):
        r0 = hd * HEAD_PAD
        x1 = qt[r0 + MLA_NOPE:r0 + MLA_NOPE + ROPE_HALF]
        x2 = qt[r0 + MLA_NOPE + ROPE_HALF:r0 + MLA_QK]
        qt_ref[0, 0, r0:r0 + MLA_NOPE, :] = qt[r0:r0 + MLA_NOPE].astype(bf)
        qt_ref[0, 0, r0 + MLA_NOPE:r0 + MLA_NOPE + ROPE_HALF, :] = (x1 * cos_t - x2 * sin_t).astype(bf)
        qt_ref[0, 0, r0 + MLA_NOPE + ROPE_HALF:r0 + MLA_QK, :] = (x2 * cos_t + x1 * sin_t).astype(bf)
        qt_ref[0, 0, r0 + MLA_QK:r0 + HEAD_PAD, :] = qt[r0 + MLA_QK:r0 + HEAD_PAD].astype(bf)

    kvn_t = _rms_rows(pt[_T_KVLAT:_T_QS], gkvc_ref[...]).astype(bf)
    vt = _dot(wvt_ref[...], kvn_t).astype(bf)
    ones_row = (lax.broadcasted_iota(jnp.int32, (BF16_ROWS, vt.shape[1]), 0) == 0).astype(bf)
    for hd in range(MLA_HEADS):
        vt_ref[0, 0, hd * V_EXT:hd * V_EXT + MLA_V, :] = vt[hd * MLA_V:(hd + 1) * MLA_V]
        vt_ref[0, 0, hd * V_EXT + MLA_V:(hd + 1) * V_EXT, :] = ones_row
    kvn_s = _rms_lanes(ps[:, _S_KVLAT:_S_KS], gkvr_ref[...]).astype(bf)
    kn = _dot(kvn_s, wk_ref[...])
    ang_s = pcol_ref[0].astype(jnp.float32) * invr_ref[...]
    kpe = ps[:, _S_ROPE:_S_ROPE_ROT] * jnp.cos(ang_s) + ps[:, _S_ROPE_ROT:_S_COLS] * jnp.sin(ang_s)
    for hd in range(MLA_HEADS):
        k_ref[0, hd] = (kn[:, hd * HEAD_PAD:(hd + 1) * HEAD_PAD] + kpe).astype(bf)

    qst_ref[0, 0] = (pt[_T_QS:_T_VS] * (SWA_HEAD_DIM ** -0.5)).astype(bf)
    ks_ref[0] = ps[:, _S_KS:_S_ROPE].astype(bf)
    vst_ref[0, 0] = pt[_T_VS:_T_GATE].astype(bf)
    gt_ref[0, 0] = pt[_T_GATE:_T_ROWS].astype(bf)


def _mla_schedule(n_tiles):
    rows = [(q, q if t == 0 else t - 1, int(t == 0), int(t == 0)) for q in range(n_tiles) for t in range(q + 1)]
    assert len(rows) % MLA_UNROLL == 0
    pad = (MLA_STAGES - 1) * MLA_UNROLL
    rows = [rows[0]] * pad + rows + [rows[-1]] * pad
    return np.asarray(rows, np.int32).T.copy(), len(rows) // MLA_UNROLL - (MLA_STAGES - 1)


def _mla_kernel(tab_ref, qt_ref, k_ref, vt_ref, ot_ref, s_scr, p_scr, mask_scr, *, n_iters):
    bf = jnp.bfloat16
    u_n = MLA_UNROLL
    tq = SEQ_TILE

    @pl.when((pl.program_id(0) == 0) & (pl.program_id(1) == 0))
    def _():
        s_scr[...] = jnp.zeros(s_scr.shape, jnp.float32)
        p_scr[...] = jnp.zeros(p_scr.shape, bf)
        key = lax.broadcasted_iota(jnp.int32, (tq, tq), 0)
        qry = lax.broadcasted_iota(jnp.int32, (tq, tq), 1)
        mask_scr[0] = jnp.zeros((tq, tq), jnp.float32)
        mask_scr[1] = jnp.where(key <= qry, 0.0, NEG_BIG)

    def body(i, carry):
        m, acc, m_tiles, alphas = carry
        new_tiles, new_alphas = [], []
        for u in range(u_n):
            f = i * u_n + u
            pv = _dot(vt_ref[0, tab_ref[1, f]], p_scr[u])
            acc = jnp.where(tab_ref[3, f] == 1, pv, alphas[u] * acc + pv)
            ot_ref[0, tab_ref[0, f]] = (acc[:MLA_V] * (1.0 / acc[MLA_V:MLA_V + 1])).astype(bf)
        for u in range(u_n):
            f = (i + 1) * u_n + u
            m_in = jnp.where(tab_ref[3, f] == 1, NEG_BIG, m)
            m = jnp.maximum(m_in, m_tiles[u])
            new_alphas.append(jnp.exp2(m_in - m))
            p_scr[u] = jnp.exp2(s_scr[u] - m).astype(bf)
        for u in range(u_n):
            f = (i + 2) * u_n + u
            kj = k_ref[0, 0, pl.ds(pl.multiple_of(tab_ref[1, f] * tq, tq), tq), :]
            st = _dot(kj, qt_ref[0, tab_ref[0, f]]) + mask_scr[tab_ref[2, f]]
            s_scr[u] = st
            new_tiles.append(jnp.max(st, axis=0, keepdims=True))
        return m, acc, tuple(new_tiles), tuple(new_alphas)

    row = mask_scr[0, 0:1, :]
    lax.fori_loop(0, n_iters, body,
                  (row, jnp.zeros((V_EXT, tq), jnp.float32), (row,) * u_n, (row,) * u_n))


def _swa_kernel(bias_ref, sink_ref, qst_ref, kp_ref, kc_ref, vp_ref, vc_ref,
                qpos_ref, kpp_ref, kpc_ref, ot_ref):
    bf = jnp.bfloat16
    blk = pl.program_id(1)
    groups = SWA_HEADS // SWA_KV_HEADS
    kband = jnp.concatenate([kp_ref[0], kc_ref[0]], axis=0)
    vband = jnp.concatenate([vp_ref[0, 0], vc_ref[0, 0]], axis=1)
    kpos = jnp.concatenate([kpp_ref[0], kpc_ref[0]], axis=0)
    qpos = qpos_ref[0]

    shape = (2 * BLOCK, BLOCK)
    n = jnp.maximum(qpos - kpos, 0)
    max_exact = REL_BUCKETS // 2
    nf = jnp.maximum(n, 1).astype(jnp.float32)
    large = max_exact + (jnp.log(nf / max_exact) / math.log(REL_MAX_DIST / max_exact)
                         * (REL_BUCKETS - max_exact)).astype(jnp.int32)
    bucket = jnp.where(n < max_exact, n, jnp.minimum(large, REL_BUCKETS - 1))

    ki = lax.broadcasted_iota(jnp.int32, shape, 0)
    qi = lax.broadcasted_iota(jnp.int32, shape, 1)
    delta = qi + BLOCK - ki
    valid = (delta >= 0) & (delta < WINDOW) & (blk * BLOCK + ki - BLOCK >= 0)

    for kvh in range(SWA_KV_HEADS):
        d0 = kvh * SWA_HEAD_DIM
        k_h = kband[:, d0:d0 + SWA_HEAD_DIM]
        v_h = vband[d0:d0 + SWA_HEAD_DIM, :]
        qcat = jnp.concatenate(
            [qst_ref[0, 0, (kvh * groups + g) * SWA_HEAD_DIM:(kvh * groups + g + 1) * SWA_HEAD_DIM, :]
             for g in range(groups)], axis=1)
        st = _dot(k_h, qcat)
        for g in range(groups):
            hd = kvh * groups + g
            table = jnp.broadcast_to(bias_ref[hd:hd + 1, :], shape)
            bias = jnp.take_along_axis(table, bucket, axis=1, mode="promise_in_bounds")
            s = jnp.where(valid, st[:, g * BLOCK:(g + 1) * BLOCK] + bias, NEG_BIG)
            sink = sink_ref[hd]
            m = jnp.maximum(jnp.max(s, axis=0, keepdims=True), sink)
            e = jnp.exp(s - m)
            denom = jnp.sum(e, axis=0, keepdims=True) + jnp.exp(sink - m)
            p = (e / denom).astype(bf)
            ot_ref[0, 0, hd * SWA_HEAD_DIM:(hd + 1) * SWA_HEAD_DIM, :] = _dot(v_h, p).astype(bf)


def _out_kernel(oat_ref, obt_ref, gt_ref, x_ref, wo_ref, fn_ref, o_ref):
    g = gt_ref[0, 0].astype(jnp.float32)
    mix = jnp.concatenate([oat_ref[0, 0], obt_ref[0, 0]], axis=0).astype(jnp.float32)
    yt = (mix * (g * jax.nn.sigmoid(g))).astype(jnp.bfloat16)
    z = x_ref[0] + _dot_tn(yt, wo_ref[...])
    o_ref[0] = _rms_lanes(z, fn_ref[...])


def _const_spec(shape):
    return pl.BlockSpec(shape, lambda *_: (0,) * len(shape))


def _params(n_axes):
    return pltpu.CompilerParams(dimension_semantics=("arbitrary",) * n_axes,
                                vmem_limit_bytes=VMEM_LIMIT)


def kernel(x, positions, norm_gain, w_in, q_a_norm, w_q_b, kv_a_norm, w_kv_b, sinks, rel_bias, w_out,
           final_norm):
    bf, f32 = jnp.bfloat16, jnp.float32
    batch, seq, _ = x.shape
    assert norm_gain.shape[0] == 1, "single-layer kernel"
    tm = SEQ_TILE
    nt = seq // tm
    nb = seq // BLOCK
    per = tm // BLOCK

    w = w_in[0]
    c = 0
    w_qlat, c = w[:, c:c + MLA_Q_LORA], c + MLA_Q_LORA
    w_kvlat, c = w[:, c:c + MLA_KV_LORA], c + MLA_KV_LORA
    w_krope, c = w[:, c:c + MLA_ROPE], c + MLA_ROPE
    w_qs, c = w[:, c:c + SWA_WIDTH], c + SWA_WIDTH
    w_ks, c = w[:, c:c + SWA_KV_WIDTH], c + SWA_KV_WIDTH
    w_vs, c = w[:, c:c + SWA_KV_WIDTH], c + SWA_KV_WIDTH
    w_gate = w[:, c:c + MIX_WIDTH]
    wt = jnp.concatenate([w_qlat, w_kvlat, w_qs, w_vs, w_gate], axis=1).T.astype(bf)
    r1, r2 = w_krope[:, :ROPE_HALF], w_krope[:, ROPE_HALF:]
    zl = jnp.zeros((D_MODEL, MLA_NOPE), f32)
    zr = jnp.zeros((D_MODEL, HEAD_PAD - MLA_QK), f32)
    ws = jnp.concatenate([w_kvlat, w_ks, zl, r1, r2, zr, zl, -r2, r1, zr], axis=1).astype(bf)

    wq = w_q_b[0].T.reshape(MLA_HEADS, MLA_QK, MLA_Q_LORA)
    wqt = jnp.pad(wq, ((0, 0), (0, HEAD_PAD - MLA_QK), (0, 0))).reshape(MLA_HEADS * HEAD_PAD, MLA_Q_LORA)
    wkv = w_kv_b[0].reshape(MLA_KV_LORA, MLA_HEADS, MLA_NOPE + MLA_V)
    wk = jnp.pad(wkv[:, :, :MLA_NOPE], ((0, 0), (0, 0), (0, HEAD_PAD - MLA_NOPE)))
    wk = wk.reshape(MLA_KV_LORA, MLA_HEADS * HEAD_PAD).astype(bf)
    wvt = wkv[:, :, MLA_NOPE:].reshape(MLA_KV_LORA, MLA_WIDTH).T.astype(bf)
    wqt = wqt.astype(bf)

    inv = ROPE_THETA ** (-jnp.arange(ROPE_HALF, dtype=f32) / ROPE_HALF)
    inv_col = inv.reshape(ROPE_HALF, 1)
    inv_row = jnp.tile(inv, LANES // ROPE_HALF).reshape(1, LANES)
    pos_row = positions.reshape(batch, 1, seq)
    pos_col = positions.reshape(batch, seq, 1)

    tile4 = lambda rows: pl.BlockSpec((1, 1, rows, tm), lambda b, i: (b, i, 0, 0))
    qt, k, vt, qst, ks, vst, gt = pl.pallas_call(
        _proj_kernel,
        grid=(batch, nt),
        in_specs=[
            pl.BlockSpec((1, tm, D_MODEL), lambda b, i: (b, i, 0)),
            pl.BlockSpec((1, 1, tm), lambda b, i: (b, 0, i)),
            pl.BlockSpec((1, tm, 1), lambda b, i: (b, i, 0)),
            _const_spec((1, D_MODEL)),
            _const_spec((_T_ROWS, D_MODEL)),
            _const_spec((D_MODEL, _S_COLS)),
            _const_spec((MLA_Q_LORA, 1)),
            _const_spec((MLA_KV_LORA, 1)),
            _const_spec((1, MLA_KV_LORA)),
            _const_spec((MLA_HEADS * HEAD_PAD, MLA_Q_LORA)),
            _const_spec((MLA_WIDTH, MLA_KV_LORA)),
            _const_spec((MLA_KV_LORA, MLA_HEADS * HEAD_PAD)),
            _const_spec((ROPE_HALF, 1)),
            _const_spec((1, LANES)),
        ],
        out_specs=[
            tile4(MLA_HEADS * HEAD_PAD),
            pl.BlockSpec((1, MLA_HEADS, tm, HEAD_PAD), lambda b, i: (b, 0, i, 0)),
            tile4(MLA_HEADS * V_EXT),
            tile4(SWA_WIDTH),
            pl.BlockSpec((1, tm, SWA_KV_WIDTH), lambda b, i: (b, i, 0)),
            tile4(SWA_KV_WIDTH),
            tile4(MIX_WIDTH),
        ],
        out_shape=[
            jax.ShapeDtypeStruct((batch, nt, MLA_HEADS * HEAD_PAD, tm), bf),
            jax.ShapeDtypeStruct((batch, MLA_HEADS, seq, HEAD_PAD), bf),
            jax.ShapeDtypeStruct((batch, nt, MLA_HEADS * V_EXT, tm), bf),
            jax.ShapeDtypeStruct((batch, nt, SWA_WIDTH, tm), bf),
            jax.ShapeDtypeStruct((batch, seq, SWA_KV_WIDTH), bf),
            jax.ShapeDtypeStruct((batch, nt, SWA_KV_WIDTH, tm), bf),
            jax.ShapeDtypeStruct((batch, nt, MIX_WIDTH, tm), bf),
        ],
        compiler_params=_params(2),
        name="proj",
    )(x, pos_row, pos_col, norm_gain[0].reshape(1, D_MODEL), wt, ws,
      q_a_norm[0].reshape(MLA_Q_LORA, 1), kv_a_norm[0].reshape(MLA_KV_LORA, 1),
      kv_a_norm[0].reshape(1, MLA_KV_LORA), wqt, wvt, wk, inv_col, inv_row)

    schedule, n_iters = _mla_schedule(nt)
    oat = pl.pallas_call(
        functools.partial(_mla_kernel, n_iters=n_iters),
        grid=(batch, MLA_HEADS),
        in_specs=[
            pl.BlockSpec(memory_space=pltpu.SMEM),
            pl.BlockSpec((1, nt, HEAD_PAD, tm), lambda b, h: (b, 0, h, 0)),
            pl.BlockSpec((1, 1, seq, HEAD_PAD), lambda b, h: (b, h, 0, 0)),
            pl.BlockSpec((1, nt, V_EXT, tm), lambda b, h: (b, 0, h, 0)),
        ],
        out_specs=pl.BlockSpec((1, nt, MLA_V, tm), lambda b, h: (b, 0, h, 0)),
        out_shape=jax.ShapeDtypeStruct((batch, nt, MLA_WIDTH, tm), bf),
        scratch_shapes=[pltpu.VMEM((MLA_UNROLL, tm, tm), f32), pltpu.VMEM((MLA_UNROLL, tm, tm), bf),
                        pltpu.VMEM((2, tm, tm), f32)],
        compiler_params=_params(2),
        name="mla",
    )(jnp.asarray(schedule), qt, k, vt)

    prev = lambda i: jnp.maximum(i - 1, 0)
    bias_t = jnp.pad(rel_bias.T.astype(f32), ((0, 0), (0, LANES - REL_BUCKETS)))
    obt = pl.pallas_call(
        _swa_kernel,
        grid=(batch, nb),
        in_specs=[
            _const_spec((SWA_HEADS, LANES)),
            pl.BlockSpec(memory_space=pltpu.SMEM),
            pl.BlockSpec((1, 1, SWA_WIDTH, BLOCK), lambda b, i: (b, i // per, 0, i % per)),
            pl.BlockSpec((1, BLOCK, SWA_KV_WIDTH), lambda b, i: (b, prev(i), 0)),
            pl.BlockSpec((1, BLOCK, SWA_KV_WIDTH), lambda b, i: (b, i, 0)),
            pl.BlockSpec((1, 1, SWA_KV_WIDTH, BLOCK), lambda b, i: (b, prev(i) // per, 0, prev(i) % per)),
            pl.BlockSpec((1, 1, SWA_KV_WIDTH, BLOCK), lambda b, i: (b, i // per, 0, i % per)),
            pl.BlockSpec((1, 1, BLOCK), lambda b, i: (b, 0, i)),
            pl.BlockSpec((1, BLOCK, 1), lambda b, i: (b, prev(i), 0)),
            pl.BlockSpec((1, BLOCK, 1), lambda b, i: (b, i, 0)),
        ],
        out_specs=pl.BlockSpec((1, 1, SWA_WIDTH, BLOCK), lambda b, i: (b, i // per, 0, i % per)),
        out_shape=jax.ShapeDtypeStruct((batch, nt, SWA_WIDTH, tm), bf),
        compiler_params=_params(2),
        name="swa",
    )(bias_t, sinks[0].astype(f32), qst, ks, ks, vst, vst, pos_row, pos_col, pos_col)

    return pl.pallas_call(
        _out_kernel,
        grid=(batch, nt),
        in_specs=[
            tile4(MLA_WIDTH),
            tile4(SWA_WIDTH),
            tile4(MIX_WIDTH),
            pl.BlockSpec((1, tm, D_MODEL), lambda b, i: (b, i, 0)),
            _const_spec((MIX_WIDTH, D_MODEL)),
            _const_spec((1, D_MODEL)),
        ],
        out_specs=pl.BlockSpec((1, tm, D_MODEL), lambda b, i: (b, i, 0)),
        out_shape=jax.ShapeDtypeStruct((batch, seq, D_MODEL), x.dtype),
        compiler_params=_params(2),
        name="out",
    )(oat, obt, gt, x, w_out[0].astype(bf), final_norm.reshape(1, D_MODEL))
```

```python
import functools
import math

import jax
import numpy as np
import jax.numpy as jnp
from jax import lax
from jax.experimental import pallas as pl
from jax.experimental.pallas import tpu as pltpu

D_MODEL = 1024
MLA_HEADS = 8
MLA_Q_LORA = 256
MLA_KV_LORA = 128
MLA_NOPE = 64
MLA_ROPE = 32
MLA_V = 64
SWA_HEADS = 8
SWA_KV_HEADS = 2
SWA_HEAD_DIM = 64
WINDOW = 128
BLOCK = 128
REL_BUCKETS = 32
REL_MAX_DIST = 128
ROPE_THETA = 10000.0
EPS = 1e-6

MLA_QK = MLA_NOPE + MLA_ROPE
MLA_WIDTH = MLA_HEADS * MLA_V
SWA_WIDTH = SWA_HEADS * SWA_HEAD_DIM
SWA_KV_WIDTH = SWA_KV_HEADS * SWA_HEAD_DIM
MIX_WIDTH = MLA_WIDTH + SWA_WIDTH
ROPE_HALF = MLA_ROPE // 2

LANES = 128
HEAD_PAD = LANES
BF16_ROWS = 16
V_EXT = MLA_V + BF16_ROWS
SWA_V_EXT = SWA_HEAD_DIM + BF16_ROWS
LOG2E = math.log2(math.e)
SEQ_TILE = 512
MLA_UNROLL = 4
MLA_STAGES = 3
NEG_BIG = -1e30
VMEM_LIMIT = 56 * 1024 * 1024

_T_QLAT = 0
_T_KVLAT = _T_QLAT + MLA_Q_LORA
_T_QS = _T_KVLAT + MLA_KV_LORA
_T_VS = _T_QS + SWA_WIDTH
_T_GATE = _T_VS + SWA_KV_WIDTH
_T_ROWS = _T_GATE + MIX_WIDTH
_S_KVLAT = 0
_S_KS = LANES
_S_ROPE = 2 * LANES
_S_ROPE_ROT = 3 * LANES
_S_COLS = 4 * LANES


def _rms_rows(v, gain_col):
    ms = jnp.mean(v * v, axis=0, keepdims=True)
    return v * lax.rsqrt(ms + EPS) * gain_col


def _rms_lanes(v, gain_row):
    ms = jnp.mean(v * v, axis=-1, keepdims=True)
    return v * lax.rsqrt(ms + EPS) * gain_row


def _dot(a, b):
    return jnp.dot(a, b, preferred_element_type=jnp.float32)


def _dot_nt(a, b):
    return lax.dot_general(a, b, (((1,), (1,)), ((), ())), preferred_element_type=jnp.float32)


def _dot_tn(a, b):
    return lax.dot_general(a, b, (((0,), (0,)), ((), ())), preferred_element_type=jnp.float32)


def _proj_kernel(x_ref, prow_ref, pcol_ref, g1_ref, wt_ref, ws_ref, gq_ref, gkvc_ref, gkvr_ref,
                 wqt_ref, wvt_ref, wk_ref, invc_ref, invr_ref,
                 qt_ref, k_ref, vt_ref, qst_ref, ks_ref, vst_ref, gt_ref):
    bf = jnp.bfloat16
    x = x_ref[0]
    h = _rms_lanes(x, g1_ref[...]).astype(bf)
    pt = _dot_nt(wt_ref[...], h)
    ps = _dot(h, ws_ref[...])

    qn = _rms_rows(pt[_T_QLAT:_T_KVLAT], gq_ref[...]).astype(bf)
    qt = _dot(wqt_ref[...], qn) * (MLA_QK ** -0.5 * LOG2E)
    ang_t = invc_ref[...] * prow_ref[0].astype(jnp.float32)
    cos_t, sin_t = jnp.cos(ang_t), jnp.sin(ang_t)
    for hd in range(MLA_HEADS):
        r0 = hd * HEAD_PAD
        x1 = qt[r0 + MLA_NOPE:r0 + MLA_NOPE + ROPE_HALF]
        x2 = qt[r0 + MLA_NOPE + ROPE_HALF:r0 + MLA_QK]
        qt_ref[0, 0, r0:r0 + MLA_NOPE, :] = qt[r0:r0 + MLA_NOPE].astype(bf)
        qt_ref[0, 0, r0 + MLA_NOPE:r0 + MLA_NOPE + ROPE_HALF, :] = (x1 * cos_t - x2 * sin_t).astype(bf)
        qt_ref[0, 0, r0 + MLA_NOPE + ROPE_HALF:r0 + MLA_QK, :] = (x2 * cos_t + x1 * sin_t).astype(bf)
        qt_ref[0, 0, r0 + MLA_QK:r0 + HEAD_PAD, :] = qt[r0 + MLA_QK:r0 + HEAD_PAD].astype(bf)

    kvn_t = _rms_rows(pt[_T_KVLAT:_T_QS], gkvc_ref[...]).astype(bf)
    vt = _dot(wvt_ref[...], kvn_t).astype(bf)
    ones_row = (lax.broadcasted_iota(jnp.int32, (BF16_ROWS, vt.shape[1]), 0) == 0).astype(bf)
    for hd in range(MLA_HEADS):
        vt_ref[0, 0, hd * V_EXT:hd * V_EXT + MLA_V, :] = vt[hd * MLA_V:(hd + 1) * MLA_V]
        vt_ref[0, 0, hd * V_EXT + MLA_V:(hd + 1) * V_EXT, :] = ones_row
    kvn_s = _rms_lanes(ps[:, _S_KVLAT:_S_KS], gkvr_ref[...]).astype(bf)
    kn = _dot(kvn_s, wk_ref[...])
    ang_s = pcol_ref[0].astype(jnp.float32) * invr_ref[...]
    kpe = ps[:, _S_ROPE:_S_ROPE_ROT] * jnp.cos(ang_s) + ps[:, _S_ROPE_ROT:_S_COLS] * jnp.sin(ang_s)
    for hd in range(MLA_HEADS):
        k_ref[0, hd] = (kn[:, hd * HEAD_PAD:(hd + 1) * HEAD_PAD] + kpe).astype(bf)

    qst_ref[0, 0] = (pt[_T_QS:_T_VS] * (SWA_HEAD_DIM ** -0.5 * LOG2E)).astype(bf)
    ks_ref[0] = ps[:, _S_KS:_S_ROPE].astype(bf)
    for hd in range(SWA_KV_HEADS):
        vst_ref[0, 0, hd * SWA_V_EXT:hd * SWA_V_EXT + SWA_HEAD_DIM, :] = (
            pt[_T_VS + hd * SWA_HEAD_DIM:_T_VS + (hd + 1) * SWA_HEAD_DIM].astype(bf))
        vst_ref[0, 0, hd * SWA_V_EXT + SWA_HEAD_DIM:(hd + 1) * SWA_V_EXT, :] = ones_row
    gt_ref[0, 0] = pt[_T_GATE:_T_ROWS].astype(bf)


def _mla_schedule(n_tiles):
    rows, todo = [], list(range(n_tiles))
    while todo:
        gap = -len(rows) % MLA_UNROLL
        q = min(t for t in todo if t + 1 > gap)
        todo.remove(q)
        keys = list(range(q))
        keys.insert(gap, q)
        rows += [(q, kt, int(kt == q), int(j == 0)) for j, kt in enumerate(keys)]
    assert len(rows) % MLA_UNROLL == 0 and all(r[2] == 0 for j, r in enumerate(rows) if j % MLA_UNROLL)
    pad = (MLA_STAGES - 1) * MLA_UNROLL
    rows = [rows[0]] * pad + rows + [rows[-1]] * pad
    return np.asarray(rows, np.int32).T.copy(), len(rows) // MLA_UNROLL - (MLA_STAGES - 1)


def _mla_kernel(tab_ref, qt_ref, k_ref, vt_ref, ot_ref, s_scr, p_scr, mask_scr, *, n_iters):
    bf = jnp.bfloat16
    u_n = MLA_UNROLL
    tq = SEQ_TILE

    @pl.when((pl.program_id(0) == 0) & (pl.program_id(1) == 0))
    def _():
        s_scr[...] = jnp.zeros(s_scr.shape, jnp.float32)
        p_scr[...] = jnp.zeros(p_scr.shape, bf)
        key = lax.broadcasted_iota(jnp.int32, (tq, tq), 0)
        qry = lax.broadcasted_iota(jnp.int32, (tq, tq), 1)
        mask_scr[0] = jnp.zeros((tq, tq), jnp.float32)
        mask_scr[1] = jnp.where(key <= qry, 0.0, NEG_BIG)

    def body(i, carry):
        m, acc, m_tiles, alphas = carry
        new_tiles, new_alphas = [], []
        for u in range(u_n):
            f = i * u_n + u
            pv = _dot(vt_ref[0, tab_ref[1, f]], p_scr[u])
            acc = jnp.where(tab_ref[3, f] == 1, pv, alphas[u] * acc + pv)
            ot_ref[0, tab_ref[0, f]] = (acc[:MLA_V] * (1.0 / acc[MLA_V:MLA_V + 1])).astype(bf)
        for u in range(u_n):
            f = (i + 1) * u_n + u
            m_in = jnp.where(tab_ref[3, f] == 1, NEG_BIG, m)
            m = jnp.maximum(m_in, m_tiles[u])
            new_alphas.append(jnp.exp2(m_in - m))
            p_scr[u] = jnp.exp2(s_scr[u] - m).astype(bf)
        for u in range(u_n):
            f = (i + 2) * u_n + u
            kj = k_ref[0, 0, pl.ds(pl.multiple_of(tab_ref[1, f] * tq, tq), tq), :]
            st = _dot(kj, qt_ref[0, tab_ref[0, f]])
            if u == 0:
                st = st + mask_scr[tab_ref[2, f]]
            s_scr[u] = st
            new_tiles.append(jnp.max(st, axis=0, keepdims=True))
        return m, acc, tuple(new_tiles), tuple(new_alphas)

    row = mask_scr[0, 0:1, :]
    lax.fori_loop(0, n_iters, body,
                  (row, jnp.zeros((V_EXT, tq), jnp.float32), (row,) * u_n, (row,) * u_n))


def _swa_logit_offsets(bias_ref, qpos, kpos, first_key):
    shape = (2 * BLOCK, BLOCK)
    n = jnp.maximum(qpos - kpos, 0)
    max_exact = REL_BUCKETS // 2
    nf = jnp.maximum(n, 1).astype(jnp.float32)
    large = max_exact + (jnp.log(nf / max_exact) / math.log(REL_MAX_DIST / max_exact)
                         * (REL_BUCKETS - max_exact)).astype(jnp.int32)
    bucket = jnp.where(n < max_exact, n, jnp.minimum(large, REL_BUCKETS - 1))
    ki = lax.broadcasted_iota(jnp.int32, shape, 0)
    qi = lax.broadcasted_iota(jnp.int32, shape, 1)
    delta = qi + BLOCK - ki
    valid = (delta >= 0) & (delta < WINDOW) & (ki >= first_key)
    tiles = []
    for hd in range(SWA_HEADS):
        table = jnp.broadcast_to(bias_ref[hd:hd + 1, :], shape)
        bias = jnp.take_along_axis(table, bucket, axis=1, mode="promise_in_bounds")
        tiles.append(jnp.where(valid, bias * LOG2E, NEG_BIG))
    return tiles


_SWA_SLOT_RUN = 0
_SWA_SLOT_RUN_FIRST = 1
_SWA_SLOT_BLOCK = 2


def _swa_kernel(bias_ref, sink_ref, qst_ref, kp_ref, kc_ref, vp_ref, vc_ref,
                prow_ref, ppcol_ref, pccol_ref, ot_ref, off_scr):
    bf = jnp.bfloat16
    step = pl.program_id(1)
    per = SEQ_TILE // BLOCK
    groups = SWA_HEADS // SWA_KV_HEADS

    @pl.when((pl.program_id(0) == 0) & (step == 0))
    def _():
        run_q = lax.broadcasted_iota(jnp.int32, (1, BLOCK), 1) + BLOCK
        run_k = lax.broadcasted_iota(jnp.int32, (2 * BLOCK, 1), 0)
        for slot, first_key in ((_SWA_SLOT_RUN, 0), (_SWA_SLOT_RUN_FIRST, BLOCK)):
            for hd, tile in enumerate(_swa_logit_offsets(bias_ref, run_q, run_k, first_key)):
                off_scr[slot, hd] = tile

    pos = prow_ref[0]
    prev = ppcol_ref[0]
    base = pos[:, 0:1]
    run_pos = pos - base == lax.broadcasted_iota(jnp.int32, pos.shape, 1)
    run_prev = prev - base + BLOCK == lax.broadcasted_iota(jnp.int32, prev.shape, 0)
    consecutive = (jnp.min(jnp.where(run_pos, 1.0, 0.0)) > 0.5) & (
        (jnp.min(jnp.where(run_prev, 1.0, 0.0)) > 0.5) | (step == 0))

    @pl.when(jnp.logical_not(consecutive))
    def _():
        cur = pccol_ref[0]
        for j in range(per):
            kpos = (jnp.concatenate([prev, cur[:BLOCK]], axis=0) if j == 0
                    else cur[(j - 1) * BLOCK:(j + 1) * BLOCK])
            first_key = jnp.where(step == 0, BLOCK, 0) if j == 0 else 0
            tiles = _swa_logit_offsets(bias_ref, pos[:, j * BLOCK:(j + 1) * BLOCK], kpos, first_key)
            for hd, tile in enumerate(tiles):
                off_scr[_SWA_SLOT_BLOCK + j, hd] = tile

    for j in range(per):
        q0 = j * BLOCK
        if j == 0:
            slot = jnp.where(consecutive, jnp.where(step == 0, _SWA_SLOT_RUN_FIRST, _SWA_SLOT_RUN),
                             _SWA_SLOT_BLOCK)
            kband = jnp.concatenate([kp_ref[0], kc_ref[0, :BLOCK]], axis=0)
            vband = jnp.concatenate([vp_ref[0, 0], vc_ref[0, 0, :, :BLOCK]], axis=1)
        else:
            slot = jnp.where(consecutive, _SWA_SLOT_RUN, _SWA_SLOT_BLOCK + j)
            kband = kc_ref[0, q0 - BLOCK:q0 + BLOCK]
            vband = vc_ref[0, 0, :, q0 - BLOCK:q0 + BLOCK]
        for kvh in range(SWA_KV_HEADS):
            k_h = kband[:, kvh * SWA_HEAD_DIM:(kvh + 1) * SWA_HEAD_DIM]
            v_h = vband[kvh * SWA_V_EXT:(kvh + 1) * SWA_V_EXT]
            qcat = jnp.concatenate(
                [qst_ref[0, 0, (kvh * groups + g) * SWA_HEAD_DIM:(kvh * groups + g + 1) * SWA_HEAD_DIM,
                         q0:q0 + BLOCK] for g in range(groups)], axis=1)
            st = _dot(k_h, qcat)
            for g in range(groups):
                hd = kvh * groups + g
                s = st[:, g * BLOCK:(g + 1) * BLOCK] + off_scr[slot, hd]
                sink = sink_ref[hd] * LOG2E
                m = jnp.maximum(jnp.max(s, axis=0, keepdims=True), sink)
                ov = _dot(v_h, jnp.exp2(s - m).astype(bf))
                denom = ov[SWA_HEAD_DIM:SWA_HEAD_DIM + 1] + jnp.exp2(sink - m)
                ot_ref[0, 0, hd * SWA_HEAD_DIM:(hd + 1) * SWA_HEAD_DIM, q0:q0 + BLOCK] = (
                    ov[:SWA_HEAD_DIM] * (1.0 / denom)).astype(bf)


def _out_kernel(oat_ref, obt_ref, gt_ref, x_ref, wo_ref, fn_ref, o_ref):
    g = gt_ref[0, 0].astype(jnp.float32)
    mix = jnp.concatenate([oat_ref[0, 0], obt_ref[0, 0]], axis=0).astype(jnp.float32)
    yt = (mix * (g * jax.nn.sigmoid(g))).astype(jnp.bfloat16)
    z = x_ref[0] + _dot_tn(yt, wo_ref[...])
    o_ref[0] = _rms_lanes(z, fn_ref[...])


def _const_spec(shape):
    return pl.BlockSpec(shape, lambda *_: (0,) * len(shape))


def _params(n_axes):
    return pltpu.CompilerParams(dimension_semantics=("arbitrary",) * n_axes,
                                vmem_limit_bytes=VMEM_LIMIT)


def kernel(x, positions, norm_gain, w_in, q_a_norm, w_q_b, kv_a_norm, w_kv_b, sinks, rel_bias, w_out,
           final_norm):
    bf, f32 = jnp.bfloat16, jnp.float32
    batch, seq, _ = x.shape
    assert norm_gain.shape[0] == 1, "single-layer kernel"
    tm = SEQ_TILE
    nt = seq // tm
    nb = seq // BLOCK
    per = tm // BLOCK

    w = w_in[0]
    c = 0
    w_qlat, c = w[:, c:c + MLA_Q_LORA], c + MLA_Q_LORA
    w_kvlat, c = w[:, c:c + MLA_KV_LORA], c + MLA_KV_LORA
    w_krope, c = w[:, c:c + MLA_ROPE], c + MLA_ROPE
    w_qs, c = w[:, c:c + SWA_WIDTH], c + SWA_WIDTH
    w_ks, c = w[:, c:c + SWA_KV_WIDTH], c + SWA_KV_WIDTH
    w_vs, c = w[:, c:c + SWA_KV_WIDTH], c + SWA_KV_WIDTH
    w_gate = w[:, c:c + MIX_WIDTH]
    wt = jnp.concatenate([w_qlat, w_kvlat, w_qs, w_vs, w_gate], axis=1).T.astype(bf)
    r1, r2 = w_krope[:, :ROPE_HALF], w_krope[:, ROPE_HALF:]
    zl = jnp.zeros((D_MODEL, MLA_NOPE), f32)
    zr = jnp.zeros((D_MODEL, HEAD_PAD - MLA_QK), f32)
    ws = jnp.concatenate([w_kvlat, w_ks, zl, r1, r2, zr, zl, -r2, r1, zr], axis=1).astype(bf)

    wq = w_q_b[0].T.reshape(MLA_HEADS, MLA_QK, MLA_Q_LORA)
    wqt = jnp.pad(wq, ((0, 0), (0, HEAD_PAD - MLA_QK), (0, 0))).reshape(MLA_HEADS * HEAD_PAD, MLA_Q_LORA)
    wkv = w_kv_b[0].reshape(MLA_KV_LORA, MLA_HEADS, MLA_NOPE + MLA_V)
    wk = jnp.pad(wkv[:, :, :MLA_NOPE], ((0, 0), (0, 0), (0, HEAD_PAD - MLA_NOPE)))
    wk = wk.reshape(MLA_KV_LORA, MLA_HEADS * HEAD_PAD).astype(bf)
    wvt = wkv[:, :, MLA_NOPE:].reshape(MLA_KV_LORA, MLA_WIDTH).T.astype(bf)
    wqt = wqt.astype(bf)

    inv = ROPE_THETA ** (-jnp.arange(ROPE_HALF, dtype=f32) / ROPE_HALF)
    inv_col = inv.reshape(ROPE_HALF, 1)
    inv_row = jnp.tile(inv, LANES // ROPE_HALF).reshape(1, LANES)
    pos_row = positions.reshape(batch, 1, seq)
    pos_col = positions.reshape(batch, seq, 1)

    tile4 = lambda rows: pl.BlockSpec((1, 1, rows, tm), lambda b, i: (b, i, 0, 0))
    qt, k, vt, qst, ks, vst, gt = pl.pallas_call(
        _proj_kernel,
        grid=(batch, nt),
        in_specs=[
            pl.BlockSpec((1, tm, D_MODEL), lambda b, i: (b, i, 0)),
            pl.BlockSpec((1, 1, tm), lambda b, i: (b, 0, i)),
            pl.BlockSpec((1, tm, 1), lambda b, i: (b, i, 0)),
            _const_spec((1, D_MODEL)),
            _const_spec((_T_ROWS, D_MODEL)),
            _const_spec((D_MODEL, _S_COLS)),
            _const_spec((MLA_Q_LORA, 1)),
            _const_spec((MLA_KV_LORA, 1)),
            _const_spec((1, MLA_KV_LORA)),
            _const_spec((MLA_HEADS * HEAD_PAD, MLA_Q_LORA)),
            _const_spec((MLA_WIDTH, MLA_KV_LORA)),
            _const_spec((MLA_KV_LORA, MLA_HEADS * HEAD_PAD)),
            _const_spec((ROPE_HALF, 1)),
            _const_spec((1, LANES)),
        ],
        out_specs=[
            tile4(MLA_HEADS * HEAD_PAD),
            pl.BlockSpec((1, MLA_HEADS, tm, HEAD_PAD), lambda b, i: (b, 0, i, 0)),
            tile4(MLA_HEADS * V_EXT),
            tile4(SWA_WIDTH),
            pl.BlockSpec((1, tm, SWA_KV_WIDTH), lambda b, i: (b, i, 0)),
            tile4(SWA_KV_HEADS * SWA_V_EXT),
            tile4(MIX_WIDTH),
        ],
        out_shape=[
            jax.ShapeDtypeStruct((batch, nt, MLA_HEADS * HEAD_PAD, tm), bf),
            jax.ShapeDtypeStruct((batch, MLA_HEADS, seq, HEAD_PAD), bf),
            jax.ShapeDtypeStruct((batch, nt, MLA_HEADS * V_EXT, tm), bf),
            jax.ShapeDtypeStruct((batch, nt, SWA_WIDTH, tm), bf),
            jax.ShapeDtypeStruct((batch, seq, SWA_KV_WIDTH), bf),
            jax.ShapeDtypeStruct((batch, nt, SWA_KV_HEADS * SWA_V_EXT, tm), bf),
            jax.ShapeDtypeStruct((batch, nt, MIX_WIDTH, tm), bf),
        ],
        compiler_params=_params(2),
        name="proj",
    )(x, pos_row, pos_col, norm_gain[0].reshape(1, D_MODEL), wt, ws,
      q_a_norm[0].reshape(MLA_Q_LORA, 1), kv_a_norm[0].reshape(MLA_KV_LORA, 1),
      kv_a_norm[0].reshape(1, MLA_KV_LORA), wqt, wvt, wk, inv_col, inv_row)

    schedule, n_iters = _mla_schedule(nt)
    oat = pl.pallas_call(
        functools.partial(_mla_kernel, n_iters=n_iters),
        grid=(batch, MLA_HEADS),
        in_specs=[
            pl.BlockSpec(memory_space=pltpu.SMEM),
            pl.BlockSpec((1, nt, HEAD_PAD, tm), lambda b, h: (b, 0, h, 0)),
            pl.BlockSpec((1, 1, seq, HEAD_PAD), lambda b, h: (b, h, 0, 0)),
            pl.BlockSpec((1, nt, V_EXT, tm), lambda b, h: (b, 0, h, 0)),
        ],
        out_specs=pl.BlockSpec((1, nt, MLA_V, tm), lambda b, h: (b, 0, h, 0)),
        out_shape=jax.ShapeDtypeStruct((batch, nt, MLA_WIDTH, tm), bf),
        scratch_shapes=[pltpu.VMEM((MLA_UNROLL, tm, tm), f32), pltpu.VMEM((MLA_UNROLL, tm, tm), bf),
                        pltpu.VMEM((2, tm, tm), f32)],
        compiler_params=_params(2),
        name="mla",
    )(jnp.asarray(schedule), qt, k, vt)

    prev = lambda i: jnp.maximum(i * per - 1, 0)
    bias_t = jnp.pad(rel_bias.T.astype(f32), ((0, 0), (0, LANES - REL_BUCKETS)))
    obt = pl.pallas_call(
        _swa_kernel,
        grid=(batch, nt),
        in_specs=[
            _const_spec((SWA_HEADS, LANES)),
            pl.BlockSpec(memory_space=pltpu.SMEM),
            tile4(SWA_WIDTH),
            pl.BlockSpec((1, BLOCK, SWA_KV_WIDTH), lambda b, i: (b, prev(i), 0)),
            pl.BlockSpec((1, tm, SWA_KV_WIDTH), lambda b, i: (b, i, 0)),
            pl.BlockSpec((1, 1, SWA_KV_HEADS * SWA_V_EXT, BLOCK),
                         lambda b, i: (b, prev(i) // per, 0, prev(i) % per)),
            tile4(SWA_KV_HEADS * SWA_V_EXT),
            pl.BlockSpec((1, 1, tm), lambda b, i: (b, 0, i)),
            pl.BlockSpec((1, BLOCK, 1), lambda b, i: (b, prev(i), 0)),
            pl.BlockSpec((1, tm, 1), lambda b, i: (b, i, 0)),
        ],
        out_specs=tile4(SWA_WIDTH),
        out_shape=jax.ShapeDtypeStruct((batch, nt, SWA_WIDTH, tm), bf),
        scratch_shapes=[pltpu.VMEM((_SWA_SLOT_BLOCK + per, SWA_HEADS, 2 * BLOCK, BLOCK), f32)],
        compiler_params=_params(2),
        name="swa",
    )(bias_t, sinks[0].astype(f32), qst, ks, ks, vst, vst, pos_row, pos_col, pos_col)

    return pl.pallas_call(
        _out_kernel,
        grid=(batch, nt),
        in_specs=[
            tile4(MLA_WIDTH),
            tile4(SWA_WIDTH),
            tile4(MIX_WIDTH),
            pl.BlockSpec((1, tm, D_MODEL), lambda b, i: (b, i, 0)),
            _const_spec((MIX_WIDTH, D_MODEL)),
            _const_spec((1, D_MODEL)),
        ],
        out_specs=pl.BlockSpec((1, tm, D_MODEL), lambda b, i: (b, i, 0)),
        out_shape=jax.ShapeDtypeStruct((batch, seq, D_MODEL), x.dtype),
        compiler_params=_params(2),
        name="out",
    )(oat, obt, gt, x, w_out[0].astype(bf), final_norm.reshape(1, D_MODEL))
```

```python
import functools
import math

import jax
import numpy as np
import jax.numpy as jnp
from jax import lax
from jax.experimental import pallas as pl
from jax.experimental.pallas import tpu as pltpu

D_MODEL = 1024
MLA_HEADS = 8
MLA_Q_LORA = 256
MLA_KV_LORA = 128
MLA_NOPE = 64
MLA_ROPE = 32
MLA_V = 64
SWA_HEADS = 8
SWA_KV_HEADS = 2
SWA_HEAD_DIM = 64
WINDOW = 128
BLOCK = 128
REL_BUCKETS = 32
REL_MAX_DIST = 128
ROPE_THETA = 10000.0
EPS = 1e-6

MLA_QK = MLA_NOPE + MLA_ROPE
MLA_WIDTH = MLA_HEADS * MLA_V
SWA_WIDTH = SWA_HEADS * SWA_HEAD_DIM
SWA_KV_WIDTH = SWA_KV_HEADS * SWA_HEAD_DIM
MIX_WIDTH = MLA_WIDTH + SWA_WIDTH
ROPE_HALF = MLA_ROPE // 2

LANES = 128
HEAD_PAD = LANES
BF16_ROWS = 16
V_EXT = MLA_V + BF16_ROWS
SWA_V_EXT = SWA_HEAD_DIM + BF16_ROWS
LOG2E = math.log2(math.e)
SEQ_TILE = 512
MLA_UNROLL = 4
MLA_STAGES = 3
NEG_BIG = -1e30
VMEM_LIMIT = 56 * 1024 * 1024

_T_QLAT = 0
_T_KVLAT = _T_QLAT + MLA_Q_LORA
_T_QS = _T_KVLAT + MLA_KV_LORA
_T_VS = _T_QS + SWA_WIDTH
_T_GATE = _T_VS + SWA_KV_WIDTH
_T_KROPE = _T_GATE + MIX_WIDTH
_T_ROWS = _T_KROPE + MLA_ROPE
_S_KVLAT = 0
_S_KS = LANES
_S_COLS = 2 * LANES


def _rms_rows(v, gain_col):
    ms = jnp.mean(v * v, axis=0, keepdims=True)
    return v * lax.rsqrt(ms + EPS) * gain_col


def _rms_lanes(v, gain_row):
    ms = jnp.mean(v * v, axis=-1, keepdims=True)
    return v * lax.rsqrt(ms + EPS) * gain_row


def _dot(a, b):
    return jnp.dot(a, b, preferred_element_type=jnp.float32)


def _dot_nt(a, b):
    return lax.dot_general(a, b, (((1,), (1,)), ((), ())), preferred_element_type=jnp.float32)


def _dot_tn(a, b):
    return lax.dot_general(a, b, (((0,), (0,)), ((), ())), preferred_element_type=jnp.float32)


def _proj_kernel(x_ref, prow_ref, g1_ref, wt_ref, ws_ref, gq_ref, gkvc_ref, gkvr_ref,
                 wqt_ref, wvt_ref, wk_ref, invc_ref,
                 qt_ref, k_ref, vt_ref, qst_ref, ks_ref, vst_ref, gt_ref):
    bf = jnp.bfloat16
    x = x_ref[0]
    tm = x.shape[0]
    h = _rms_lanes(x, g1_ref[...]).astype(bf)
    pt = _dot_nt(wt_ref[...], h)
    ps = _dot(h, ws_ref[...])

    qn = _rms_rows(pt[_T_QLAT:_T_KVLAT], gq_ref[...]).astype(bf)
    qt = _dot(wqt_ref[...], qn) * (MLA_QK ** -0.5 * LOG2E)
    ang_t = invc_ref[...] * prow_ref[0].astype(jnp.float32)
    cos_t, sin_t = jnp.cos(ang_t), jnp.sin(ang_t)
    for hd in range(MLA_HEADS):
        r0 = hd * HEAD_PAD
        x1 = qt[r0 + MLA_NOPE:r0 + MLA_NOPE + ROPE_HALF]
        x2 = qt[r0 + MLA_NOPE + ROPE_HALF:r0 + MLA_QK]
        qt_ref[0, 0, r0:r0 + MLA_NOPE, :] = qt[r0:r0 + MLA_NOPE].astype(bf)
        qt_ref[0, 0, r0 + MLA_NOPE:r0 + MLA_NOPE + ROPE_HALF, :] = (x1 * cos_t - x2 * sin_t).astype(bf)
        qt_ref[0, 0, r0 + MLA_NOPE + ROPE_HALF:r0 + MLA_QK, :] = (x2 * cos_t + x1 * sin_t).astype(bf)
        qt_ref[0, 0, r0 + MLA_QK:r0 + HEAD_PAD, :] = qt[r0 + MLA_QK:r0 + HEAD_PAD].astype(bf)

    kvn_t = _rms_rows(pt[_T_KVLAT:_T_QS], gkvc_ref[...]).astype(bf)
    vt = _dot(wvt_ref[...], kvn_t).astype(bf)
    ones_row = (lax.broadcasted_iota(jnp.int32, (BF16_ROWS, vt.shape[1]), 0) == 0).astype(bf)
    for hd in range(MLA_HEADS):
        vt_ref[0, 0, hd * V_EXT:hd * V_EXT + MLA_V, :] = vt[hd * MLA_V:(hd + 1) * MLA_V]
        vt_ref[0, 0, hd * V_EXT + MLA_V:(hd + 1) * V_EXT, :] = ones_row
    kvn_s = _rms_lanes(ps[:, _S_KVLAT:_S_KS], gkvr_ref[...]).astype(bf)
    kn = _dot(kvn_s, wk_ref[...])
    y1 = pt[_T_KROPE:_T_KROPE + ROPE_HALF]
    y2 = pt[_T_KROPE + ROPE_HALF:_T_ROWS]
    kpe_t = jnp.concatenate([jnp.zeros((MLA_NOPE, tm), jnp.float32),
                             y1 * cos_t - y2 * sin_t, y2 * cos_t + y1 * sin_t,
                             jnp.zeros((HEAD_PAD - MLA_QK, tm), jnp.float32)], axis=0)
    kpe = kpe_t.T
    for hd in range(MLA_HEADS):
        k_ref[0, hd] = (kn[:, hd * HEAD_PAD:(hd + 1) * HEAD_PAD] + kpe).astype(bf)

    qst_ref[0, 0] = (pt[_T_QS:_T_VS] * (SWA_HEAD_DIM ** -0.5 * LOG2E)).astype(bf)
    ks_ref[0] = ps[:, _S_KS:_S_COLS].astype(bf)
    for hd in range(SWA_KV_HEADS):
        vst_ref[0, 0, hd * SWA_V_EXT:hd * SWA_V_EXT + SWA_HEAD_DIM, :] = (
            pt[_T_VS + hd * SWA_HEAD_DIM:_T_VS + (hd + 1) * SWA_HEAD_DIM].astype(bf))
        vst_ref[0, 0, hd * SWA_V_EXT + SWA_HEAD_DIM:(hd + 1) * SWA_V_EXT, :] = ones_row
    gt_ref[0, 0] = pt[_T_GATE:_T_KROPE].astype(bf)


def _mla_schedule(n_tiles):
    rows, todo = [], list(range(n_tiles))
    while todo:
        gap = -len(rows) % MLA_UNROLL
        q = min(t for t in todo if t + 1 > gap)
        todo.remove(q)
        keys = list(range(q))
        keys.insert(gap, q)
        rows += [(q, kt, int(kt == q), int(j == 0)) for j, kt in enumerate(keys)]
    assert len(rows) % MLA_UNROLL == 0 and all(r[2] == 0 for j, r in enumerate(rows) if j % MLA_UNROLL)
    pad = (MLA_STAGES - 1) * MLA_UNROLL
    rows = [rows[0]] * pad + rows + [rows[-1]] * pad
    return np.asarray(rows, np.int32).T.copy(), len(rows) // MLA_UNROLL - (MLA_STAGES - 1)


def _mla_kernel(tab_ref, qt_ref, k_ref, vt_ref, ot_ref, s_scr, p_scr, mask_scr, *, n_iters):
    bf = jnp.bfloat16
    u_n = MLA_UNROLL
    tq = SEQ_TILE

    @pl.when((pl.program_id(0) == 0) & (pl.program_id(1) == 0))
    def _():
        s_scr[...] = jnp.zeros(s_scr.shape, jnp.float32)
        p_scr[...] = jnp.zeros(p_scr.shape, bf)
        key = lax.broadcasted_iota(jnp.int32, (tq, tq), 0)
        qry = lax.broadcasted_iota(jnp.int32, (tq, tq), 1)
        mask_scr[0] = jnp.zeros((tq, tq), jnp.float32)
        mask_scr[1] = jnp.where(key <= qry, 0.0, NEG_BIG)

    def body(i, carry):
        m, acc, m_tiles, alphas = carry
        new_tiles, new_alphas = [], []
        for u in range(u_n):
            f = i * u_n + u
            pv = _dot(vt_ref[0, tab_ref[1, f]], p_scr[u])
            acc = jnp.where(tab_ref[3, f] == 1, pv, alphas[u] * acc + pv)
            ot_ref[0, tab_ref[0, f]] = (acc[:MLA_V] * (1.0 / acc[MLA_V:MLA_V + 1])).astype(bf)
        for u in range(u_n):
            f = (i + 1) * u_n + u
            m_in = jnp.where(tab_ref[3, f] == 1, NEG_BIG, m)
            m = jnp.maximum(m_in, m_tiles[u])
            new_alphas.append(jnp.exp2(m_in - m))
            p_scr[u] = jnp.exp2(s_scr[u] - m).astype(bf)
        for u in range(u_n):
            f = (i + 2) * u_n + u
            kj = k_ref[0, 0, pl.ds(pl.multiple_of(tab_ref[1, f] * tq, tq), tq), :]
            st = _dot(kj, qt_ref[0, tab_ref[0, f]])
            if u == 0:
                st = st + mask_scr[tab_ref[2, f]]
            s_scr[u] = st
            new_tiles.append(jnp.max(st, axis=0, keepdims=True))
        return m, acc, tuple(new_tiles), tuple(new_alphas)

    row = mask_scr[0, 0:1, :]
    lax.fori_loop(0, n_iters, body,
                  (row, jnp.zeros((V_EXT, tq), jnp.float32), (row,) * u_n, (row,) * u_n))


def _swa_logit_offsets(bias_ref, qpos, kpos, first_key):
    shape = (2 * BLOCK, BLOCK)
    n = jnp.maximum(qpos - kpos, 0)
    max_exact = REL_BUCKETS // 2
    nf = jnp.maximum(n, 1).astype(jnp.float32)
    large = max_exact + (jnp.log(nf / max_exact) / math.log(REL_MAX_DIST / max_exact)
                         * (REL_BUCKETS - max_exact)).astype(jnp.int32)
    bucket = jnp.where(n < max_exact, n, jnp.minimum(large, REL_BUCKETS - 1))
    ki = lax.broadcasted_iota(jnp.int32, shape, 0)
    qi = lax.broadcasted_iota(jnp.int32, shape, 1)
    delta = qi + BLOCK - ki
    valid = (delta >= 0) & (delta < WINDOW) & (ki >= first_key)
    tiles = []
    for hd in range(SWA_HEADS):
        table = jnp.broadcast_to(bias_ref[hd:hd + 1, :], shape)
        bias = jnp.take_along_axis(table, bucket, axis=1, mode="promise_in_bounds")
        tiles.append(jnp.where(valid, bias * LOG2E, NEG_BIG))
    return tiles


_SWA_SLOT_RUN = 0
_SWA_SLOT_RUN_FIRST = 1
_SWA_SLOT_BLOCK = 2


def _swa_kernel(bias_ref, sink_ref, qst_ref, kp_ref, kc_ref, vp_ref, vc_ref,
                prow_ref, ppcol_ref, pccol_ref, ot_ref, off_scr):
    bf = jnp.bfloat16
    step = pl.program_id(1)
    per = SEQ_TILE // BLOCK
    groups = SWA_HEADS // SWA_KV_HEADS

    @pl.when((pl.program_id(0) == 0) & (step == 0))
    def _():
        run_q = lax.broadcasted_iota(jnp.int32, (1, BLOCK), 1) + BLOCK
        run_k = lax.broadcasted_iota(jnp.int32, (2 * BLOCK, 1), 0)
        for slot, first_key in ((_SWA_SLOT_RUN, 0), (_SWA_SLOT_RUN_FIRST, BLOCK)):
            for hd, tile in enumerate(_swa_logit_offsets(bias_ref, run_q, run_k, first_key)):
                off_scr[slot, hd] = tile

    pos = prow_ref[0]
    prev = ppcol_ref[0]
    base = pos[:, 0:1]
    run_pos = pos - base == lax.broadcasted_iota(jnp.int32, pos.shape, 1)
    run_prev = prev - base + BLOCK == lax.broadcasted_iota(jnp.int32, prev.shape, 0)
    consecutive = (jnp.min(jnp.where(run_pos, 1.0, 0.0)) > 0.5) & (
        (jnp.min(jnp.where(run_prev, 1.0, 0.0)) > 0.5) | (step == 0))

    @pl.when(jnp.logical_not(consecutive))
    def _():
        cur = pccol_ref[0]
        for j in range(per):
            kpos = (jnp.concatenate([prev, cur[:BLOCK]], axis=0) if j == 0
                    else cur[(j - 1) * BLOCK:(j + 1) * BLOCK])
            first_key = jnp.where(step == 0, BLOCK, 0) if j == 0 else 0
            tiles = _swa_logit_offsets(bias_ref, pos[:, j * BLOCK:(j + 1) * BLOCK], kpos, first_key)
            for hd, tile in enumerate(tiles):
                off_scr[_SWA_SLOT_BLOCK + j, hd] = tile

    for j in range(per):
        q0 = j * BLOCK
        if j == 0:
            slot = jnp.where(consecutive, jnp.where(step == 0, _SWA_SLOT_RUN_FIRST, _SWA_SLOT_RUN),
                             _SWA_SLOT_BLOCK)
            kband = jnp.concatenate([kp_ref[0], kc_ref[0, :BLOCK]], axis=0)
            vband = jnp.concatenate([vp_ref[0, 0], vc_ref[0, 0, :, :BLOCK]], axis=1)
        else:
            slot = jnp.where(consecutive, _SWA_SLOT_RUN, _SWA_SLOT_BLOCK + j)
            kband = kc_ref[0, q0 - BLOCK:q0 + BLOCK]
            vband = vc_ref[0, 0, :, q0 - BLOCK:q0 + BLOCK]
        for kvh in range(SWA_KV_HEADS):
            k_h = kband[:, kvh * SWA_HEAD_DIM:(kvh + 1) * SWA_HEAD_DIM]
            v_h = vband[kvh * SWA_V_EXT:(kvh + 1) * SWA_V_EXT]
            qcat = jnp.concatenate(
                [qst_ref[0, 0, (kvh * groups + g) * SWA_HEAD_DIM:(kvh * groups + g + 1) * SWA_HEAD_DIM,
                         q0:q0 + BLOCK] for g in range(groups)], axis=1)
            st = _dot(k_h, qcat)
            for g in range(groups):
                hd = kvh * groups + g
                s = st[:, g * BLOCK:(g + 1) * BLOCK] + off_scr[slot, hd]
                sink = sink_ref[hd] * LOG2E
                m = jnp.maximum(jnp.max(s, axis=0, keepdims=True), sink)
                ov = _dot(v_h, jnp.exp2(s - m).astype(bf))
                denom = ov[SWA_HEAD_DIM:SWA_HEAD_DIM + 1] + jnp.exp2(sink - m)
                ot_ref[0, 0, hd * SWA_HEAD_DIM:(hd + 1) * SWA_HEAD_DIM, q0:q0 + BLOCK] = (
                    ov[:SWA_HEAD_DIM] * (1.0 / denom)).astype(bf)


def _out_kernel(oat_ref, obt_ref, gt_ref, x_ref, wo_ref, fn_ref, o_ref):
    g = gt_ref[0, 0].astype(jnp.float32)
    mix = jnp.concatenate([oat_ref[0, 0], obt_ref[0, 0]], axis=0).astype(jnp.float32)
    yt = (mix * (g * jax.nn.sigmoid(g))).astype(jnp.bfloat16)
    z = x_ref[0] + _dot_tn(yt, wo_ref[...])
    o_ref[0] = _rms_lanes(z, fn_ref[...])


def _const_spec(shape):
    return pl.BlockSpec(shape, lambda *_: (0,) * len(shape))


def _params(n_axes):
    return pltpu.CompilerParams(dimension_semantics=("arbitrary",) * n_axes,
                                vmem_limit_bytes=VMEM_LIMIT)


def kernel(x, positions, norm_gain, w_in, q_a_norm, w_q_b, kv_a_norm, w_kv_b, sinks, rel_bias, w_out,
           final_norm):
    bf, f32 = jnp.bfloat16, jnp.float32
    batch, seq, _ = x.shape
    assert norm_gain.shape[0] == 1, "single-layer kernel"
    tm = SEQ_TILE
    nt = seq // tm
    nb = seq // BLOCK
    per = tm // BLOCK

    w = w_in[0]
    c = 0
    w_qlat, c = w[:, c:c + MLA_Q_LORA], c + MLA_Q_LORA
    w_kvlat, c = w[:, c:c + MLA_KV_LORA], c + MLA_KV_LORA
    w_krope, c = w[:, c:c + MLA_ROPE], c + MLA_ROPE
    w_qs, c = w[:, c:c + SWA_WIDTH], c + SWA_WIDTH
    w_ks, c = w[:, c:c + SWA_KV_WIDTH], c + SWA_KV_WIDTH
    w_vs, c = w[:, c:c + SWA_KV_WIDTH], c + SWA_KV_WIDTH
    w_gate = w[:, c:c + MIX_WIDTH]
    wt = jnp.concatenate([w_qlat, w_kvlat, w_qs, w_vs, w_gate, w_krope], axis=1).T.astype(bf)
    ws = jnp.concatenate([w_kvlat, w_ks], axis=1).astype(bf)

    wq = w_q_b[0].T.reshape(MLA_HEADS, MLA_QK, MLA_Q_LORA)
    wqt = jnp.pad(wq, ((0, 0), (0, HEAD_PAD - MLA_QK), (0, 0))).reshape(MLA_HEADS * HEAD_PAD, MLA_Q_LORA)
    wkv = w_kv_b[0].reshape(MLA_KV_LORA, MLA_HEADS, MLA_NOPE + MLA_V)
    wk = jnp.pad(wkv[:, :, :MLA_NOPE], ((0, 0), (0, 0), (0, HEAD_PAD - MLA_NOPE)))
    wk = wk.reshape(MLA_KV_LORA, MLA_HEADS * HEAD_PAD).astype(bf)
    wvt = wkv[:, :, MLA_NOPE:].reshape(MLA_KV_LORA, MLA_WIDTH).T.astype(bf)
    wqt = wqt.astype(bf)

    inv = ROPE_THETA ** (-jnp.arange(ROPE_HALF, dtype=f32) / ROPE_HALF)
    inv_col = inv.reshape(ROPE_HALF, 1)
    pos_row = positions.reshape(batch, 1, seq)
    pos_col = positions.reshape(batch, seq, 1)

    tile4 = lambda rows: pl.BlockSpec((1, 1, rows, tm), lambda b, i: (b, i, 0, 0))
    qt, k, vt, qst, ks, vst, gt = pl.pallas_call(
        _proj_kernel,
        grid=(batch, nt),
        in_specs=[
            pl.BlockSpec((1, tm, D_MODEL), lambda b, i: (b, i, 0)),
            pl.BlockSpec((1, 1, tm), lambda b, i: (b, 0, i)),
            _const_spec((1, D_MODEL)),
            _const_spec((_T_ROWS, D_MODEL)),
            _const_spec((D_MODEL, _S_COLS)),
            _const_spec((MLA_Q_LORA, 1)),
            _const_spec((MLA_KV_LORA, 1)),
            _const_spec((1, MLA_KV_LORA)),
            _const_spec((MLA_HEADS * HEAD_PAD, MLA_Q_LORA)),
            _const_spec((MLA_WIDTH, MLA_KV_LORA)),
            _const_spec((MLA_KV_LORA, MLA_HEADS * HEAD_PAD)),
            _const_spec((ROPE_HALF, 1)),
        ],
        out_specs=[
            tile4(MLA_HEADS * HEAD_PAD),
            pl.BlockSpec((1, MLA_HEADS, tm, HEAD_PAD), lambda b, i: (b, 0, i, 0)),
            tile4(MLA_HEADS * V_EXT),
            tile4(SWA_WIDTH),
            pl.BlockSpec((1, tm, SWA_KV_WIDTH), lambda b, i: (b, i, 0)),
            tile4(SWA_KV_HEADS * SWA_V_EXT),
            tile4(MIX_WIDTH),
        ],
        out_shape=[
            jax.ShapeDtypeStruct((batch, nt, MLA_HEADS * HEAD_PAD, tm), bf),
            jax.ShapeDtypeStruct((batch, MLA_HEADS, seq, HEAD_PAD), bf),
            jax.ShapeDtypeStruct((batch, nt, MLA_HEADS * V_EXT, tm), bf),
            jax.ShapeDtypeStruct((batch, nt, SWA_WIDTH, tm), bf),
            jax.ShapeDtypeStruct((batch, seq, SWA_KV_WIDTH), bf),
            jax.ShapeDtypeStruct((batch, nt, SWA_KV_HEADS * SWA_V_EXT, tm), bf),
            jax.ShapeDtypeStruct((batch, nt, MIX_WIDTH, tm), bf),
        ],
        compiler_params=_params(2),
        name="proj",
    )(x, pos_row, norm_gain[0].reshape(1, D_MODEL), wt, ws,
      q_a_norm[0].reshape(MLA_Q_LORA, 1), kv_a_norm[0].reshape(MLA_KV_LORA, 1),
      kv_a_norm[0].reshape(1, MLA_KV_LORA), wqt, wvt, wk, inv_col)

    schedule, n_iters = _mla_schedule(nt)
    oat = pl.pallas_call(
        functools.partial(_mla_kernel, n_iters=n_iters),
        grid=(batch, MLA_HEADS),
        in_specs=[
            pl.BlockSpec(memory_space=pltpu.SMEM),
            pl.BlockSpec((1, nt, HEAD_PAD, tm), lambda b, h: (b, 0, h, 0)),
            pl.BlockSpec((1, 1, seq, HEAD_PAD), lambda b, h: (b, h, 0, 0)),
            pl.BlockSpec((1, nt, V_EXT, tm), lambda b, h: (b, 0, h, 0)),
        ],
        out_specs=pl.BlockSpec((1, nt, MLA_V, tm), lambda b, h: (b, 0, h, 0)),
        out_shape=jax.ShapeDtypeStruct((batch, nt, MLA_WIDTH, tm), bf),
        scratch_shapes=[pltpu.VMEM((MLA_UNROLL, tm, tm), f32), pltpu.VMEM((MLA_UNROLL, tm, tm), bf),
                        pltpu.VMEM((2, tm, tm), f32)],
        compiler_params=_params(2),
        name="mla",
    )(jnp.asarray(schedule), qt, k, vt)

    prev = lambda i: jnp.maximum(i * per - 1, 0)
    bias_t = jnp.pad(rel_bias.T.astype(f32), ((0, 0), (0, LANES - REL_BUCKETS)))
    obt = pl.pallas_call(
        _swa_kernel,
        grid=(batch, nt),
        in_specs=[
            _const_spec((SWA_HEADS, LANES)),
            pl.BlockSpec(memory_space=pltpu.SMEM),
            tile4(SWA_WIDTH),
            pl.BlockSpec((1, BLOCK, SWA_KV_WIDTH), lambda b, i: (b, prev(i), 0)),
            pl.BlockSpec((1, tm, SWA_KV_WIDTH), lambda b, i: (b, i, 0)),
            pl.BlockSpec((1, 1, SWA_KV_HEADS * SWA_V_EXT, BLOCK),
                         lambda b, i: (b, prev(i) // per, 0, prev(i) % per)),
            tile4(SWA_KV_HEADS * SWA_V_EXT),
            pl.BlockSpec((1, 1, tm), lambda b, i: (b, 0, i)),
            pl.BlockSpec((1, BLOCK, 1), lambda b, i: (b, prev(i), 0)),
            pl.BlockSpec((1, tm, 1), lambda b, i: (b, i, 0)),
        ],
        out_specs=tile4(SWA_WIDTH),
        out_shape=jax.ShapeDtypeStruct((batch, nt, SWA_WIDTH, tm), bf),
        scratch_shapes=[pltpu.VMEM((_SWA_SLOT_BLOCK + per, SWA_HEADS, 2 * BLOCK, BLOCK), f32)],
        compiler_params=_params(2),
        name="swa",
    )(bias_t, sinks[0].astype(f32), qst, ks, ks, vst, vst, pos_row, pos_col, pos_col)

    return pl.pallas_call(
        _out_kernel,
        grid=(batch, nt),
        in_specs=[
            tile4(MLA_WIDTH),
            tile4(SWA_WIDTH),
            tile4(MIX_WIDTH),
            pl.BlockSpec((1, tm, D_MODEL), lambda b, i: (b, i, 0)),
            _const_spec((MIX_WIDTH, D_MODEL)),
            _const_spec((1, D_MODEL)),
        ],
        out_specs=pl.BlockSpec((1, tm, D_MODEL), lambda b, i: (b, i, 0)),
        out_shape=jax.ShapeDtypeStruct((batch, seq, D_MODEL), x.dtype),
        compiler_params=_params(2),
        name="out",
    )(oat, obt, gt, x, w_out[0].astype(bf), final_norm.reshape(1, D_MODEL))
```

```python
import functools
import math

import jax
import numpy as np
import jax.numpy as jnp
from jax import lax
from jax.experimental import pallas as pl
from jax.experimental.pallas import tpu as pltpu

D_MODEL = 1024
MLA_HEADS = 8
MLA_Q_LORA = 256
MLA_KV_LORA = 128
MLA_NOPE = 64
MLA_ROPE = 32
MLA_V = 64
SWA_HEADS = 8
SWA_KV_HEADS = 2
SWA_HEAD_DIM = 64
WINDOW = 128
BLOCK = 128
REL_BUCKETS = 32
REL_MAX_DIST = 128
ROPE_THETA = 10000.0
EPS = 1e-6

MLA_QK = MLA_NOPE + MLA_ROPE
MLA_WIDTH = MLA_HEADS * MLA_V
SWA_WIDTH = SWA_HEADS * SWA_HEAD_DIM
SWA_KV_WIDTH = SWA_KV_HEADS * SWA_HEAD_DIM
MIX_WIDTH = MLA_WIDTH + SWA_WIDTH
ROPE_HALF = MLA_ROPE // 2

LANES = 128
HEAD_PAD = LANES
BF16_ROWS = 16
V_EXT = MLA_V + BF16_ROWS
SWA_V_EXT = SWA_HEAD_DIM + BF16_ROWS
LOG2E = math.log2(math.e)
SEQ_TILE = 512
MLA_UNROLL = 4
MLA_STAGES = 3
SWA_LOOKAHEAD = 2
NEG_BIG = -1e30
VMEM_LIMIT = 56 * 1024 * 1024

_T_QLAT = 0
_T_KVLAT = _T_QLAT + MLA_Q_LORA
_T_QS = _T_KVLAT + MLA_KV_LORA
_T_VS = _T_QS + SWA_WIDTH
_T_GATE = _T_VS + SWA_KV_WIDTH
_T_KROPE = _T_GATE + MIX_WIDTH
_T_ROWS = _T_KROPE + MLA_ROPE
_S_KVLAT = 0
_S_KS = LANES
_S_COLS = 2 * LANES


def _rms_rows(v, gain_col):
    ms = jnp.mean(v * v, axis=0, keepdims=True)
    return v * lax.rsqrt(ms + EPS) * gain_col


def _rms_lanes(v, gain_row):
    ms = jnp.mean(v * v, axis=-1, keepdims=True)
    return v * lax.rsqrt(ms + EPS) * gain_row


def _dot(a, b):
    return jnp.dot(a, b, preferred_element_type=jnp.float32)


def _dot_nt(a, b):
    return lax.dot_general(a, b, (((1,), (1,)), ((), ())), preferred_element_type=jnp.float32)


def _dot_tn(a, b):
    return lax.dot_general(a, b, (((0,), (0,)), ((), ())), preferred_element_type=jnp.float32)


def _proj_kernel(x_ref, prow_ref, g1_ref, wt_ref, ws_ref, gq_ref, gkvc_ref, gkvr_ref,
                 wqt_ref, wvt_ref, wk_ref, invc_ref,
                 qt_ref, k_ref, vt_ref, qst_ref, ks_ref, vst_ref, gt_ref):
    bf = jnp.bfloat16
    x = x_ref[0]
    tm = x.shape[0]
    h = _rms_lanes(x, g1_ref[...]).astype(bf)
    pt = _dot_nt(wt_ref[...], h)
    ps = _dot(h, ws_ref[...])

    qn = _rms_rows(pt[_T_QLAT:_T_KVLAT], gq_ref[...]).astype(bf)
    qt = _dot(wqt_ref[...], qn) * (MLA_QK ** -0.5 * LOG2E)
    ang_t = invc_ref[...] * prow_ref[0].astype(jnp.float32)
    cos_t, sin_t = jnp.cos(ang_t), jnp.sin(ang_t)
    for hd in range(MLA_HEADS):
        r0 = hd * HEAD_PAD
        x1 = qt[r0 + MLA_NOPE:r0 + MLA_NOPE + ROPE_HALF]
        x2 = qt[r0 + MLA_NOPE + ROPE_HALF:r0 + MLA_QK]
        qt_ref[0, 0, r0:r0 + MLA_NOPE, :] = qt[r0:r0 + MLA_NOPE].astype(bf)
        qt_ref[0, 0, r0 + MLA_NOPE:r0 + MLA_NOPE + ROPE_HALF, :] = (x1 * cos_t - x2 * sin_t).astype(bf)
        qt_ref[0, 0, r0 + MLA_NOPE + ROPE_HALF:r0 + MLA_QK, :] = (x2 * cos_t + x1 * sin_t).astype(bf)
        qt_ref[0, 0, r0 + MLA_QK:r0 + HEAD_PAD, :] = qt[r0 + MLA_QK:r0 + HEAD_PAD].astype(bf)

    kvn_t = _rms_rows(pt[_T_KVLAT:_T_QS], gkvc_ref[...]).astype(bf)
    vt = _dot(wvt_ref[...], kvn_t).astype(bf)
    ones_row = (lax.broadcasted_iota(jnp.int32, (BF16_ROWS, vt.shape[1]), 0) == 0).astype(bf)
    for hd in range(MLA_HEADS):
        vt_ref[0, 0, hd * V_EXT:hd * V_EXT + MLA_V, :] = vt[hd * MLA_V:(hd + 1) * MLA_V]
        vt_ref[0, 0, hd * V_EXT + MLA_V:(hd + 1) * V_EXT, :] = ones_row
    kvn_s = _rms_lanes(ps[:, _S_KVLAT:_S_KS], gkvr_ref[...]).astype(bf)
    kn = _dot(kvn_s, wk_ref[...])
    y1 = pt[_T_KROPE:_T_KROPE + ROPE_HALF]
    y2 = pt[_T_KROPE + ROPE_HALF:_T_ROWS]
    kpe_t = jnp.concatenate([jnp.zeros((MLA_NOPE, tm), jnp.float32),
                             y1 * cos_t - y2 * sin_t, y2 * cos_t + y1 * sin_t,
                             jnp.zeros((HEAD_PAD - MLA_QK, tm), jnp.float32)], axis=0)
    kpe = kpe_t.T
    for hd in range(MLA_HEADS):
        k_ref[0, hd] = (kn[:, hd * HEAD_PAD:(hd + 1) * HEAD_PAD] + kpe).astype(bf)

    qst_ref[0, 0] = (pt[_T_QS:_T_VS] * (SWA_HEAD_DIM ** -0.5 * LOG2E)).astype(bf)
    ks_ref[0] = ps[:, _S_KS:_S_COLS].astype(bf)
    for hd in range(SWA_KV_HEADS):
        vst_ref[0, 0, hd * SWA_V_EXT:hd * SWA_V_EXT + SWA_HEAD_DIM, :] = (
            pt[_T_VS + hd * SWA_HEAD_DIM:_T_VS + (hd + 1) * SWA_HEAD_DIM].astype(bf))
        vst_ref[0, 0, hd * SWA_V_EXT + SWA_HEAD_DIM:(hd + 1) * SWA_V_EXT, :] = ones_row
    gt_ref[0, 0] = pt[_T_GATE:_T_KROPE].astype(bf)


def _mla_schedule(n_tiles):
    rows, todo = [], list(range(n_tiles))
    while todo:
        gap = -len(rows) % MLA_UNROLL
        q = min(t for t in todo if t + 1 > gap)
        todo.remove(q)
        keys = list(range(q))
        keys.insert(gap, q)
        rows += [(q, kt, int(kt == q), int(j == 0)) for j, kt in enumerate(keys)]
    assert len(rows) % MLA_UNROLL == 0 and all(r[2] == 0 for j, r in enumerate(rows) if j % MLA_UNROLL)
    pad = (MLA_STAGES - 1) * MLA_UNROLL
    rows = [rows[0]] * pad + rows + [rows[-1]] * pad
    return np.asarray(rows, np.int32).T.copy(), len(rows) // MLA_UNROLL - (MLA_STAGES - 1)


def _mla_kernel(tab_ref, qt_ref, k_ref, vt_ref, ot_ref, s_scr, p_scr, mask_scr, *, n_iters):
    bf = jnp.bfloat16
    u_n = MLA_UNROLL
    tq = SEQ_TILE

    @pl.when((pl.program_id(0) == 0) & (pl.program_id(1) == 0))
    def _():
        s_scr[...] = jnp.zeros(s_scr.shape, jnp.float32)
        p_scr[...] = jnp.zeros(p_scr.shape, bf)
        key = lax.broadcasted_iota(jnp.int32, (tq, tq), 0)
        qry = lax.broadcasted_iota(jnp.int32, (tq, tq), 1)
        mask_scr[0] = jnp.zeros((tq, tq), jnp.float32)
        mask_scr[1] = jnp.where(key <= qry, 0.0, NEG_BIG)

    def body(i, carry):
        m, acc, m_tiles, alphas = carry
        new_tiles, new_alphas = [], []
        for u in range(u_n):
            f = i * u_n + u
            pv = _dot(vt_ref[0, tab_ref[1, f]], p_scr[u])
            acc = jnp.where(tab_ref[3, f] == 1, pv, alphas[u] * acc + pv)
            ot_ref[0, tab_ref[0, f]] = (acc[:MLA_V] * (1.0 / acc[MLA_V:MLA_V + 1])).astype(bf)
        for u in range(u_n):
            f = (i + 1) * u_n + u
            m_in = jnp.where(tab_ref[3, f] == 1, NEG_BIG, m)
            m = jnp.maximum(m_in, m_tiles[u])
            new_alphas.append(jnp.exp2(m_in - m))
            p_scr[u] = jnp.exp2(s_scr[u] - m).astype(bf)
        for u in range(u_n):
            f = (i + 2) * u_n + u
            kj = k_ref[0, 0, pl.ds(pl.multiple_of(tab_ref[1, f] * tq, tq), tq), :]
            st = _dot(kj, qt_ref[0, tab_ref[0, f]])
            if u == 0:
                st = st + mask_scr[tab_ref[2, f]]
            s_scr[u] = st
            new_tiles.append(jnp.max(st, axis=0, keepdims=True))
        return m, acc, tuple(new_tiles), tuple(new_alphas)

    row = mask_scr[0, 0:1, :]
    lax.fori_loop(0, n_iters, body,
                  (row, jnp.zeros((V_EXT, tq), jnp.float32), (row,) * u_n, (row,) * u_n))


def _swa_logit_offsets(bias_ref, qpos, kpos, first_key):
    shape = (2 * BLOCK, BLOCK)
    n = jnp.maximum(qpos - kpos, 0)
    max_exact = REL_BUCKETS // 2
    nf = jnp.maximum(n, 1).astype(jnp.float32)
    large = max_exact + (jnp.log(nf / max_exact) / math.log(REL_MAX_DIST / max_exact)
                         * (REL_BUCKETS - max_exact)).astype(jnp.int32)
    bucket = jnp.where(n < max_exact, n, jnp.minimum(large, REL_BUCKETS - 1))
    ki = lax.broadcasted_iota(jnp.int32, shape, 0)
    qi = lax.broadcasted_iota(jnp.int32, shape, 1)
    delta = qi + BLOCK - ki
    valid = (delta >= 0) & (delta < WINDOW) & (ki >= first_key)
    tiles = []
    for hd in range(SWA_HEADS):
        table = jnp.broadcast_to(bias_ref[hd:hd + 1, :], shape)
        bias = jnp.take_along_axis(table, bucket, axis=1, mode="promise_in_bounds")
        tiles.append(jnp.where(valid, bias * LOG2E, NEG_BIG))
    return tiles


def _swa_store_offsets(off_scr, slot, tiles):
    groups = SWA_HEADS // SWA_KV_HEADS
    for hd, tile in enumerate(tiles):
        off_scr[slot, hd // groups, :, (hd % groups) * BLOCK:(hd % groups + 1) * BLOCK] = tile


_SWA_SLOT_RUN = 0
_SWA_SLOT_RUN_FIRST = 1
_SWA_SLOT_BLOCK = 2


def _swa_kernel(bias_ref, sink_ref, qst_ref, kp_ref, kc_ref, vp_ref, vc_ref,
                prow_ref, ppcol_ref, pccol_ref, ot_ref, off_scr):
    bf = jnp.bfloat16
    step = pl.program_id(1)
    per = SEQ_TILE // BLOCK
    groups = SWA_HEADS // SWA_KV_HEADS

    @pl.when((pl.program_id(0) == 0) & (step == 0))
    def _():
        run_q = lax.broadcasted_iota(jnp.int32, (1, BLOCK), 1) + BLOCK
        run_k = lax.broadcasted_iota(jnp.int32, (2 * BLOCK, 1), 0)
        for slot, first_key in ((_SWA_SLOT_RUN, 0), (_SWA_SLOT_RUN_FIRST, BLOCK)):
            _swa_store_offsets(off_scr, slot, _swa_logit_offsets(bias_ref, run_q, run_k, first_key))

    pos = prow_ref[0]
    prev = ppcol_ref[0]
    base = pos[:, 0:1]
    run_pos = pos - base == lax.broadcasted_iota(jnp.int32, pos.shape, 1)
    run_prev = prev - base + BLOCK == lax.broadcasted_iota(jnp.int32, prev.shape, 0)
    consecutive = (jnp.min(jnp.where(run_pos, 1.0, 0.0)) > 0.5) & (
        (jnp.min(jnp.where(run_prev, 1.0, 0.0)) > 0.5) | (step == 0))

    @pl.when(jnp.logical_not(consecutive))
    def _():
        cur = pccol_ref[0]
        for j in range(per):
            kpos = (jnp.concatenate([prev, cur[:BLOCK]], axis=0) if j == 0
                    else cur[(j - 1) * BLOCK:(j + 1) * BLOCK])
            first_key = jnp.where(step == 0, BLOCK, 0) if j == 0 else 0
            tiles = _swa_logit_offsets(bias_ref, pos[:, j * BLOCK:(j + 1) * BLOCK], kpos, first_key)
            _swa_store_offsets(off_scr, _SWA_SLOT_BLOCK + j, tiles)

    def scores(j, kvh):
        q0 = j * BLOCK
        if j == 0:
            slot = jnp.where(consecutive, jnp.where(step == 0, _SWA_SLOT_RUN_FIRST, _SWA_SLOT_RUN),
                             _SWA_SLOT_BLOCK)
            kband = jnp.concatenate([kp_ref[0], kc_ref[0, :BLOCK]], axis=0)
        else:
            slot = jnp.where(consecutive, _SWA_SLOT_RUN, _SWA_SLOT_BLOCK + j)
            kband = kc_ref[0, q0 - BLOCK:q0 + BLOCK]
        k_h = kband[:, kvh * SWA_HEAD_DIM:(kvh + 1) * SWA_HEAD_DIM]
        qcat = jnp.concatenate(
            [qst_ref[0, 0, (kvh * groups + g) * SWA_HEAD_DIM:(kvh * groups + g + 1) * SWA_HEAD_DIM,
                     q0:q0 + BLOCK] for g in range(groups)], axis=1)
        return _dot(k_h, qcat) + off_scr[slot, kvh]

    def attend(j, kvh, s):
        q0 = j * BLOCK
        vband = (jnp.concatenate([vp_ref[0, 0], vc_ref[0, 0, :, :BLOCK]], axis=1) if j == 0
                 else vc_ref[0, 0, :, q0 - BLOCK:q0 + BLOCK])
        v_h = vband[kvh * SWA_V_EXT:(kvh + 1) * SWA_V_EXT]
        sink = jnp.concatenate([jnp.full((1, BLOCK), sink_ref[kvh * groups + g] * LOG2E, jnp.float32)
                                for g in range(groups)], axis=1)
        m = jnp.maximum(jnp.max(s, axis=0, keepdims=True), sink)
        ov = _dot(v_h, jnp.exp2(s - m).astype(bf))
        denom = ov[SWA_HEAD_DIM:SWA_HEAD_DIM + 1] + jnp.exp2(sink - m)
        out = (ov[:SWA_HEAD_DIM] * (1.0 / denom)).astype(bf)
        for g in range(groups):
            hd = kvh * groups + g
            ot_ref[0, 0, hd * SWA_HEAD_DIM:(hd + 1) * SWA_HEAD_DIM, q0:q0 + BLOCK] = (
                out[:, g * BLOCK:(g + 1) * BLOCK])

    chains = [(j, kvh) for j in range(per) for kvh in range(SWA_KV_HEADS)]
    ahead = [scores(*chain) for chain in chains[:SWA_LOOKAHEAD]]
    for c, chain in enumerate(chains):
        if c + SWA_LOOKAHEAD < len(chains):
            ahead.append(scores(*chains[c + SWA_LOOKAHEAD]))
        attend(*chain, ahead.pop(0))


def _out_kernel(oat_ref, obt_ref, gt_ref, x_ref, wo_ref, fn_ref, o_ref):
    g = gt_ref[0, 0].astype(jnp.float32)
    mix = jnp.concatenate([oat_ref[0, 0], obt_ref[0, 0]], axis=0).astype(jnp.float32)
    yt = (mix * (g * jax.nn.sigmoid(g))).astype(jnp.bfloat16)
    z = x_ref[0] + _dot_tn(yt, wo_ref[...])
    o_ref[0] = _rms_lanes(z, fn_ref[...])


def _const_spec(shape):
    return pl.BlockSpec(shape, lambda *_: (0,) * len(shape))


def _params(n_axes):
    return pltpu.CompilerParams(dimension_semantics=("arbitrary",) * n_axes,
                                vmem_limit_bytes=VMEM_LIMIT)


def kernel(x, positions, norm_gain, w_in, q_a_norm, w_q_b, kv_a_norm, w_kv_b, sinks, rel_bias, w_out,
           final_norm):
    bf, f32 = jnp.bfloat16, jnp.float32
    batch, seq, _ = x.shape
    assert norm_gain.shape[0] == 1, "single-layer kernel"
    tm = SEQ_TILE
    nt = seq // tm
    nb = seq // BLOCK
    per = tm // BLOCK

    w = w_in[0]
    c = 0
    w_qlat, c = w[:, c:c + MLA_Q_LORA], c + MLA_Q_LORA
    w_kvlat, c = w[:, c:c + MLA_KV_LORA], c + MLA_KV_LORA
    w_krope, c = w[:, c:c + MLA_ROPE], c + MLA_ROPE
    w_qs, c = w[:, c:c + SWA_WIDTH], c + SWA_WIDTH
    w_ks, c = w[:, c:c + SWA_KV_WIDTH], c + SWA_KV_WIDTH
    w_vs, c = w[:, c:c + SWA_KV_WIDTH], c + SWA_KV_WIDTH
    w_gate = w[:, c:c + MIX_WIDTH]
    wt = jnp.concatenate([w_qlat, w_kvlat, w_qs, w_vs, w_gate, w_krope], axis=1).T.astype(bf)
    ws = jnp.concatenate([w_kvlat, w_ks], axis=1).astype(bf)

    wq = w_q_b[0].T.reshape(MLA_HEADS, MLA_QK, MLA_Q_LORA)
    wqt = jnp.pad(wq, ((0, 0), (0, HEAD_PAD - MLA_QK), (0, 0))).reshape(MLA_HEADS * HEAD_PAD, MLA_Q_LORA)
    wkv = w_kv_b[0].reshape(MLA_KV_LORA, MLA_HEADS, MLA_NOPE + MLA_V)
    wk = jnp.pad(wkv[:, :, :MLA_NOPE], ((0, 0), (0, 0), (0, HEAD_PAD - MLA_NOPE)))
    wk = wk.reshape(MLA_KV_LORA, MLA_HEADS * HEAD_PAD).astype(bf)
    wvt = wkv[:, :, MLA_NOPE:].reshape(MLA_KV_LORA, MLA_WIDTH).T.astype(bf)
    wqt = wqt.astype(bf)

    inv = ROPE_THETA ** (-jnp.arange(ROPE_HALF, dtype=f32) / ROPE_HALF)
    inv_col = inv.reshape(ROPE_HALF, 1)
    pos_row = positions.reshape(batch, 1, seq)
    pos_col = positions.reshape(batch, seq, 1)

    tile4 = lambda rows: pl.BlockSpec((1, 1, rows, tm), lambda b, i: (b, i, 0, 0))
    qt, k, vt, qst, ks, vst, gt = pl.pallas_call(
        _proj_kernel,
        grid=(batch, nt),
        in_specs=[
            pl.BlockSpec((1, tm, D_MODEL), lambda b, i: (b, i, 0)),
            pl.BlockSpec((1, 1, tm), lambda b, i: (b, 0, i)),
            _const_spec((1, D_MODEL)),
            _const_spec((_T_ROWS, D_MODEL)),
            _const_spec((D_MODEL, _S_COLS)),
            _const_spec((MLA_Q_LORA, 1)),
            _const_spec((MLA_KV_LORA, 1)),
            _const_spec((1, MLA_KV_LORA)),
            _const_spec((MLA_HEADS * HEAD_PAD, MLA_Q_LORA)),
            _const_spec((MLA_WIDTH, MLA_KV_LORA)),
            _const_spec((MLA_KV_LORA, MLA_HEADS * HEAD_PAD)),
            _const_spec((ROPE_HALF, 1)),
        ],
        out_specs=[
            tile4(MLA_HEADS * HEAD_PAD),
            pl.BlockSpec((1, MLA_HEADS, tm, HEAD_PAD), lambda b, i: (b, 0, i, 0)),
            tile4(MLA_HEADS * V_EXT),
            tile4(SWA_WIDTH),
            pl.BlockSpec((1, tm, SWA_KV_WIDTH), lambda b, i: (b, i, 0)),
            tile4(SWA_KV_HEADS * SWA_V_EXT),
            tile4(MIX_WIDTH),
        ],
        out_shape=[
            jax.ShapeDtypeStruct((batch, nt, MLA_HEADS * HEAD_PAD, tm), bf),
            jax.ShapeDtypeStruct((batch, MLA_HEADS, seq, HEAD_PAD), bf),
            jax.ShapeDtypeStruct((batch, nt, MLA_HEADS * V_EXT, tm), bf),
            jax.ShapeDtypeStruct((batch, nt, SWA_WIDTH, tm), bf),
            jax.ShapeDtypeStruct((batch, seq, SWA_KV_WIDTH), bf),
            jax.ShapeDtypeStruct((batch, nt, SWA_KV_HEADS * SWA_V_EXT, tm), bf),
            jax.ShapeDtypeStruct((batch, nt, MIX_WIDTH, tm), bf),
        ],
        compiler_params=_params(2),
        name="proj",
    )(x, pos_row, norm_gain[0].reshape(1, D_MODEL), wt, ws,
      q_a_norm[0].reshape(MLA_Q_LORA, 1), kv_a_norm[0].reshape(MLA_KV_LORA, 1),
      kv_a_norm[0].reshape(1, MLA_KV_LORA), wqt, wvt, wk, inv_col)

    schedule, n_iters = _mla_schedule(nt)
    oat = pl.pallas_call(
        functools.partial(_mla_kernel, n_iters=n_iters),
        grid=(batch, MLA_HEADS),
        in_specs=[
            pl.BlockSpec(memory_space=pltpu.SMEM),
            pl.BlockSpec((1, nt, HEAD_PAD, tm), lambda b, h: (b, 0, h, 0)),
            pl.BlockSpec((1, 1, seq, HEAD_PAD), lambda b, h: (b, h, 0, 0)),
            pl.BlockSpec((1, nt, V_EXT, tm), lambda b, h: (b, 0, h, 0)),
        ],
        out_specs=pl.BlockSpec((1, nt, MLA_V, tm), lambda b, h: (b, 0, h, 0)),
        out_shape=jax.ShapeDtypeStruct((batch, nt, MLA_WIDTH, tm), bf),
        scratch_shapes=[pltpu.VMEM((MLA_UNROLL, tm, tm), f32), pltpu.VMEM((MLA_UNROLL, tm, tm), bf),
                        pltpu.VMEM((2, tm, tm), f32)],
        compiler_params=_params(2),
        name="mla",
    )(jnp.asarray(schedule), qt, k, vt)

    prev = lambda i: jnp.maximum(i * per - 1, 0)
    bias_t = jnp.pad(rel_bias.T.astype(f32), ((0, 0), (0, LANES - REL_BUCKETS)))
    obt = pl.pallas_call(
        _swa_kernel,
        grid=(batch, nt),
        in_specs=[
            _const_spec((SWA_HEADS, LANES)),
            pl.BlockSpec(memory_space=pltpu.SMEM),
            tile4(SWA_WIDTH),
            pl.BlockSpec((1, BLOCK, SWA_KV_WIDTH), lambda b, i: (b, prev(i), 0)),
            pl.BlockSpec((1, tm, SWA_KV_WIDTH), lambda b, i: (b, i, 0)),
            pl.BlockSpec((1, 1, SWA_KV_HEADS * SWA_V_EXT, BLOCK),
                         lambda b, i: (b, prev(i) // per, 0, prev(i) % per)),
            tile4(SWA_KV_HEADS * SWA_V_EXT),
            pl.BlockSpec((1, 1, tm), lambda b, i: (b, 0, i)),
            pl.BlockSpec((1, BLOCK, 1), lambda b, i: (b, prev(i), 0)),
            pl.BlockSpec((1, tm, 1), lambda b, i: (b, i, 0)),
        ],
        out_specs=tile4(SWA_WIDTH),
        out_shape=jax.ShapeDtypeStruct((batch, nt, SWA_WIDTH, tm), bf),
        scratch_shapes=[pltpu.VMEM((_SWA_SLOT_BLOCK + per, SWA_KV_HEADS, 2 * BLOCK,
                                    (SWA_HEADS // SWA_KV_HEADS) * BLOCK), f32)],
        compiler_params=_params(2),
        name="swa",
    )(bias_t, sinks[0].astype(f32), qst, ks, ks, vst, vst, pos_row, pos_col, pos_col)

    return pl.pallas_call(
        _out_kernel,
        grid=(batch, nt),
        in_specs=[
            tile4(MLA_WIDTH),
            tile4(SWA_WIDTH),
            tile4(MIX_WIDTH),
            pl.BlockSpec((1, tm, D_MODEL), lambda b, i: (b, i, 0)),
            _const_spec((MIX_WIDTH, D_MODEL)),
            _const_spec((1, D_MODEL)),
        ],
        out_specs=pl.BlockSpec((1, tm, D_MODEL), lambda b, i: (b, i, 0)),
        out_shape=jax.ShapeDtypeStruct((batch, seq, D_MODEL), x.dtype),
        compiler_params=_params(2),
        name="out",
    )(oat, obt, gt, x, w_out[0].astype(bf), final_norm.reshape(1, D_MODEL))
```

```python
import functools
import math

import jax
import numpy as np
import jax.numpy as jnp
from jax import lax
from jax.experimental import pallas as pl
from jax.experimental.pallas import tpu as pltpu

D_MODEL = 1024
MLA_HEADS = 8
MLA_Q_LORA = 256
MLA_KV_LORA = 128
MLA_NOPE = 64
MLA_ROPE = 32
MLA_V = 64
SWA_HEADS = 8
SWA_KV_HEADS = 2
SWA_HEAD_DIM = 64
WINDOW = 128
BLOCK = 128
REL_BUCKETS = 32
REL_MAX_DIST = 128
ROPE_THETA = 10000.0
EPS = 1e-6

MLA_QK = MLA_NOPE + MLA_ROPE
MLA_WIDTH = MLA_HEADS * MLA_V
SWA_WIDTH = SWA_HEADS * SWA_HEAD_DIM
SWA_KV_WIDTH = SWA_KV_HEADS * SWA_HEAD_DIM
MIX_WIDTH = MLA_WIDTH + SWA_WIDTH
ROPE_HALF = MLA_ROPE // 2

LANES = 128
HEAD_PAD = LANES
BF16_ROWS = 16
V_EXT = MLA_V + BF16_ROWS
SWA_V_EXT = SWA_HEAD_DIM + BF16_ROWS
LOG2E = math.log2(math.e)
SEQ_TILE = 512
MLA_UNROLL = 4
MLA_STAGES = 3
SWA_LOOKAHEAD = 2
NEG_BIG = -1e30
VMEM_LIMIT = 56 * 1024 * 1024

_T_QLAT = 0
_T_KVLAT = _T_QLAT + MLA_Q_LORA
_T_QS = _T_KVLAT + MLA_KV_LORA
_T_VS = _T_QS + SWA_WIDTH
_T_GATE = _T_VS + SWA_KV_WIDTH
_T_KROPE = _T_GATE + MIX_WIDTH
_T_ROWS = _T_KROPE + MLA_ROPE
_S_KVLAT = 0
_S_KS = LANES
_S_COLS = 2 * LANES


def _rms_rows(v, gain_col):
    ms = jnp.mean(v * v, axis=0, keepdims=True)
    return v * lax.rsqrt(ms + EPS) * gain_col


def _rms_lanes(v, gain_row):
    ms = jnp.mean(v * v, axis=-1, keepdims=True)
    return v * lax.rsqrt(ms + EPS) * gain_row


def _dot(a, b):
    return jnp.dot(a, b, preferred_element_type=jnp.float32)


def _dot_nt(a, b):
    return lax.dot_general(a, b, (((1,), (1,)), ((), ())), preferred_element_type=jnp.float32)


def _dot_tn(a, b):
    return lax.dot_general(a, b, (((0,), (0,)), ((), ())), preferred_element_type=jnp.float32)


def _proj_kernel(x_ref, prow_ref, g1_ref, wt_ref, ws_ref, gq_ref, gkvc_ref, gkvr_ref,
                 wqt_ref, wvt_ref, wk_ref, invc_ref,
                 qt_ref, k_ref, vt_ref, qst_ref, ks_ref, vst_ref, gt_ref):
    bf = jnp.bfloat16
    x = x_ref[0]
    tm = x.shape[0]
    h = _rms_lanes(x, g1_ref[...]).astype(bf)
    pt = _dot_nt(wt_ref[...], h)
    ps = _dot(h, ws_ref[...])

    qn = _rms_rows(pt[_T_QLAT:_T_KVLAT], gq_ref[...]).astype(bf)
    qt = _dot(wqt_ref[...], qn) * (MLA_QK ** -0.5 * LOG2E)
    ang_t = invc_ref[...] * prow_ref[0].astype(jnp.float32)
    cos_t, sin_t = jnp.cos(ang_t), jnp.sin(ang_t)
    for hd in range(MLA_HEADS):
        r0 = hd * HEAD_PAD
        x1 = qt[r0 + MLA_NOPE:r0 + MLA_NOPE + ROPE_HALF]
        x2 = qt[r0 + MLA_NOPE + ROPE_HALF:r0 + MLA_QK]
        qt_ref[0, 0, r0:r0 + MLA_NOPE, :] = qt[r0:r0 + MLA_NOPE].astype(bf)
        qt_ref[0, 0, r0 + MLA_NOPE:r0 + MLA_NOPE + ROPE_HALF, :] = (x1 * cos_t - x2 * sin_t).astype(bf)
        qt_ref[0, 0, r0 + MLA_NOPE + ROPE_HALF:r0 + MLA_QK, :] = (x2 * cos_t + x1 * sin_t).astype(bf)
        qt_ref[0, 0, r0 + MLA_QK:r0 + HEAD_PAD, :] = qt[r0 + MLA_QK:r0 + HEAD_PAD].astype(bf)

    kvn_t = _rms_rows(pt[_T_KVLAT:_T_QS], gkvc_ref[...]).astype(bf)
    vt = _dot(wvt_ref[...], kvn_t).astype(bf)
    ones_row = (lax.broadcasted_iota(jnp.int32, (BF16_ROWS, vt.shape[1]), 0) == 0).astype(bf)
    for hd in range(MLA_HEADS):
        vt_ref[0, 0, hd * V_EXT:hd * V_EXT + MLA_V, :] = vt[hd * MLA_V:(hd + 1) * MLA_V]
        vt_ref[0, 0, hd * V_EXT + MLA_V:(hd + 1) * V_EXT, :] = ones_row
    kvn_s = _rms_lanes(ps[:, _S_KVLAT:_S_KS], gkvr_ref[...]).astype(bf)
    kn = _dot(kvn_s, wk_ref[...])
    y1 = pt[_T_KROPE:_T_KROPE + ROPE_HALF]
    y2 = pt[_T_KROPE + ROPE_HALF:_T_ROWS]
    kpe_t = jnp.concatenate([jnp.zeros((MLA_NOPE, tm), jnp.float32),
                             y1 * cos_t - y2 * sin_t, y2 * cos_t + y1 * sin_t,
                             jnp.zeros((HEAD_PAD - MLA_QK, tm), jnp.float32)], axis=0)
    kpe = kpe_t.T
    for hd in range(MLA_HEADS):
        k_ref[0, hd] = (kn[:, hd * HEAD_PAD:(hd + 1) * HEAD_PAD] + kpe).astype(bf)

    qst_ref[0, 0] = (pt[_T_QS:_T_VS] * (SWA_HEAD_DIM ** -0.5 * LOG2E)).astype(bf)
    ks_ref[0] = ps[:, _S_KS:_S_COLS].astype(bf)
    for hd in range(SWA_KV_HEADS):
        vst_ref[0, 0, hd * SWA_V_EXT:hd * SWA_V_EXT + SWA_HEAD_DIM, :] = (
            pt[_T_VS + hd * SWA_HEAD_DIM:_T_VS + (hd + 1) * SWA_HEAD_DIM].astype(bf))
        vst_ref[0, 0, hd * SWA_V_EXT + SWA_HEAD_DIM:(hd + 1) * SWA_V_EXT, :] = ones_row
    gt_ref[0, 0] = pt[_T_GATE:_T_KROPE].astype(bf)


def _mla_schedule(n_tiles):
    rows, todo = [], list(range(n_tiles))
    while todo:
        gap = -len(rows) % MLA_UNROLL
        q = min(t for t in todo if t + 1 > gap)
        todo.remove(q)
        keys = list(range(q))
        keys.insert(gap, q)
        rows += [(q, kt, int(kt == q), int(j == 0)) for j, kt in enumerate(keys)]
    assert len(rows) % MLA_UNROLL == 0 and all(r[2] == 0 for j, r in enumerate(rows) if j % MLA_UNROLL)
    pad = (MLA_STAGES - 1) * MLA_UNROLL
    rows = [rows[0]] * pad + rows + [rows[-1]] * pad
    return np.asarray(rows, np.int32).T.copy(), len(rows) // MLA_UNROLL - (MLA_STAGES - 1)


def _mla_kernel(tab_ref, qt_ref, k_ref, vt_ref, ot_ref, s_scr, p_scr, mask_scr, *, n_iters):
    bf = jnp.bfloat16
    u_n = MLA_UNROLL
    tq = SEQ_TILE

    @pl.when((pl.program_id(0) == 0) & (pl.program_id(1) == 0))
    def _():
        s_scr[...] = jnp.zeros(s_scr.shape, jnp.float32)
        p_scr[...] = jnp.zeros(p_scr.shape, bf)
        key = lax.broadcasted_iota(jnp.int32, (tq, tq), 0)
        qry = lax.broadcasted_iota(jnp.int32, (tq, tq), 1)
        mask_scr[0] = jnp.zeros((tq, tq), jnp.float32)
        mask_scr[1] = jnp.where(key <= qry, 0.0, NEG_BIG)

    def iteration(i, half, carry):
        m, acc, m_tiles, alphas = carry
        state = {"m": m, "acc": acc}
        new_tiles, new_alphas = [None] * u_n, [None] * u_n

        def value_stage(u):
            f = i * u_n + u
            pv = _dot(vt_ref[0, tab_ref[1, f]], p_scr[half, u])
            acc = jnp.where(tab_ref[3, f] == 1, pv, alphas[u] * state["acc"] + pv)
            ot_ref[0, tab_ref[0, f]] = (acc[:MLA_V] * (1.0 / acc[MLA_V:MLA_V + 1])).astype(bf)
            state["acc"] = acc

        def exp_stage(u):
            f = (i + 1) * u_n + u
            m_in = jnp.where(tab_ref[3, f] == 1, NEG_BIG, state["m"])
            m = jnp.maximum(m_in, m_tiles[u])
            new_alphas[u] = jnp.exp2(m_in - m)
            p_scr[1 - half, u] = jnp.exp2(s_scr[u] - m).astype(bf)
            state["m"] = m

        def score_stage(u):
            f = (i + 2) * u_n + u
            kj = k_ref[0, 0, pl.ds(pl.multiple_of(tab_ref[1, f] * tq, tq), tq), :]
            st = _dot(kj, qt_ref[0, tab_ref[0, f]])
            if u == 0:
                st = st + mask_scr[tab_ref[2, f]]
            s_scr[u] = st
            new_tiles[u] = jnp.max(st, axis=0, keepdims=True)

        for u in range(u_n):
            value_stage(u)
            exp_stage(u)
            score_stage(u)
        return state["m"], state["acc"], tuple(new_tiles), tuple(new_alphas)

    def body(i2, carry):
        return iteration(2 * i2 + 1, 1, iteration(2 * i2, 0, carry))

    assert n_iters % 2 == 0

    row = mask_scr[0, 0:1, :]
    lax.fori_loop(0, n_iters // 2, body,
                  (row, jnp.zeros((V_EXT, tq), jnp.float32), (row,) * u_n, (row,) * u_n))


def _swa_logit_offsets(bias_ref, qpos, kpos, first_key):
    shape = (2 * BLOCK, BLOCK)
    n = jnp.maximum(qpos - kpos, 0)
    max_exact = REL_BUCKETS // 2
    nf = jnp.maximum(n, 1).astype(jnp.float32)
    large = max_exact + (jnp.log(nf / max_exact) / math.log(REL_MAX_DIST / max_exact)
                         * (REL_BUCKETS - max_exact)).astype(jnp.int32)
    bucket = jnp.where(n < max_exact, n, jnp.minimum(large, REL_BUCKETS - 1))
    ki = lax.broadcasted_iota(jnp.int32, shape, 0)
    qi = lax.broadcasted_iota(jnp.int32, shape, 1)
    delta = qi + BLOCK - ki
    valid = (delta >= 0) & (delta < WINDOW) & (ki >= first_key)
    tiles = []
    for hd in range(SWA_HEADS):
        table = jnp.broadcast_to(bias_ref[hd:hd + 1, :], shape)
        bias = jnp.take_along_axis(table, bucket, axis=1, mode="promise_in_bounds")
        tiles.append(jnp.where(valid, bias * LOG2E, NEG_BIG))
    return tiles


def _swa_store_offsets(off_scr, slot, tiles):
    groups = SWA_HEADS // SWA_KV_HEADS
    for hd, tile in enumerate(tiles):
        off_scr[slot, hd // groups, :, (hd % groups) * BLOCK:(hd % groups + 1) * BLOCK] = tile


_SWA_SLOT_RUN = 0
_SWA_SLOT_RUN_FIRST = 1
_SWA_SLOT_BLOCK = 2


def _swa_kernel(bias_ref, sink_ref, qst_ref, kp_ref, kc_ref, vp_ref, vc_ref,
                prow_ref, ppcol_ref, pccol_ref, ot_ref, off_scr):
    bf = jnp.bfloat16
    step = pl.program_id(1)
    per = SEQ_TILE // BLOCK
    groups = SWA_HEADS // SWA_KV_HEADS

    @pl.when((pl.program_id(0) == 0) & (step == 0))
    def _():
        run_q = lax.broadcasted_iota(jnp.int32, (1, BLOCK), 1) + BLOCK
        run_k = lax.broadcasted_iota(jnp.int32, (2 * BLOCK, 1), 0)
        for slot, first_key in ((_SWA_SLOT_RUN, 0), (_SWA_SLOT_RUN_FIRST, BLOCK)):
            _swa_store_offsets(off_scr, slot, _swa_logit_offsets(bias_ref, run_q, run_k, first_key))

    pos = prow_ref[0]
    prev = ppcol_ref[0]
    base = pos[:, 0:1]
    run_pos = pos - base == lax.broadcasted_iota(jnp.int32, pos.shape, 1)
    run_prev = prev - base + BLOCK == lax.broadcasted_iota(jnp.int32, prev.shape, 0)
    consecutive = (jnp.min(jnp.where(run_pos, 1.0, 0.0)) > 0.5) & (
        (jnp.min(jnp.where(run_prev, 1.0, 0.0)) > 0.5) | (step == 0))

    @pl.when(jnp.logical_not(consecutive))
    def _():
        cur = pccol_ref[0]
        for j in range(per):
            kpos = (jnp.concatenate([prev, cur[:BLOCK]], axis=0) if j == 0
                    else cur[(j - 1) * BLOCK:(j + 1) * BLOCK])
            first_key = jnp.where(step == 0, BLOCK, 0) if j == 0 else 0
            tiles = _swa_logit_offsets(bias_ref, pos[:, j * BLOCK:(j + 1) * BLOCK], kpos, first_key)
            _swa_store_offsets(off_scr, _SWA_SLOT_BLOCK + j, tiles)

    def scores(j, kvh):
        q0 = j * BLOCK
        if j == 0:
            slot = jnp.where(consecutive, jnp.where(step == 0, _SWA_SLOT_RUN_FIRST, _SWA_SLOT_RUN),
                             _SWA_SLOT_BLOCK)
            kband = jnp.concatenate([kp_ref[0], kc_ref[0, :BLOCK]], axis=0)
        else:
            slot = jnp.where(consecutive, _SWA_SLOT_RUN, _SWA_SLOT_BLOCK + j)
            kband = kc_ref[0, q0 - BLOCK:q0 + BLOCK]
        k_h = kband[:, kvh * SWA_HEAD_DIM:(kvh + 1) * SWA_HEAD_DIM]
        qcat = jnp.concatenate(
            [qst_ref[0, 0, (kvh * groups + g) * SWA_HEAD_DIM:(kvh * groups + g + 1) * SWA_HEAD_DIM,
                     q0:q0 + BLOCK] for g in range(groups)], axis=1)
        return _dot(k_h, qcat) + off_scr[slot, kvh]

    def attend(j, kvh, s):
        q0 = j * BLOCK
        vband = (jnp.concatenate([vp_ref[0, 0], vc_ref[0, 0, :, :BLOCK]], axis=1) if j == 0
                 else vc_ref[0, 0, :, q0 - BLOCK:q0 + BLOCK])
        v_h = vband[kvh * SWA_V_EXT:(kvh + 1) * SWA_V_EXT]
        sink = jnp.concatenate([jnp.full((1, BLOCK), sink_ref[kvh * groups + g] * LOG2E, jnp.float32)
                                for g in range(groups)], axis=1)
        m = jnp.maximum(jnp.max(s, axis=0, keepdims=True), sink)
        ov = _dot(v_h, jnp.exp2(s - m).astype(bf))
        denom = ov[SWA_HEAD_DIM:SWA_HEAD_DIM + 1] + jnp.exp2(sink - m)
        out = (ov[:SWA_HEAD_DIM] * (1.0 / denom)).astype(bf)
        for g in range(groups):
            hd = kvh * groups + g
            ot_ref[0, 0, hd * SWA_HEAD_DIM:(hd + 1) * SWA_HEAD_DIM, q0:q0 + BLOCK] = (
                out[:, g * BLOCK:(g + 1) * BLOCK])

    chains = [(j, kvh) for j in range(per) for kvh in range(SWA_KV_HEADS)]
    ahead = [scores(*chain) for chain in chains[:SWA_LOOKAHEAD]]
    for c, chain in enumerate(chains):
        if c + SWA_LOOKAHEAD < len(chains):
            ahead.append(scores(*chains[c + SWA_LOOKAHEAD]))
        attend(*chain, ahead.pop(0))


def _out_kernel(oat_ref, obt_ref, gt_ref, x_ref, wo_ref, fn_ref, o_ref):
    g = gt_ref[0, 0].astype(jnp.float32)
    mix = jnp.concatenate([oat_ref[0, 0], obt_ref[0, 0]], axis=0).astype(jnp.float32)
    yt = (mix * (g * jax.nn.sigmoid(g))).astype(jnp.bfloat16)
    z = x_ref[0] + _dot_tn(yt, wo_ref[...])
    o_ref[0] = _rms_lanes(z, fn_ref[...])


def _const_spec(shape):
    return pl.BlockSpec(shape, lambda *_: (0,) * len(shape))


def _params(n_axes):
    return pltpu.CompilerParams(dimension_semantics=("arbitrary",) * n_axes,
                                vmem_limit_bytes=VMEM_LIMIT)


def kernel(x, positions, norm_gain, w_in, q_a_norm, w_q_b, kv_a_norm, w_kv_b, sinks, rel_bias, w_out,
           final_norm):
    bf, f32 = jnp.bfloat16, jnp.float32
    batch, seq, _ = x.shape
    assert norm_gain.shape[0] == 1, "single-layer kernel"
    tm = SEQ_TILE
    nt = seq // tm
    nb = seq // BLOCK
    per = tm // BLOCK

    w = w_in[0]
    c = 0
    w_qlat, c = w[:, c:c + MLA_Q_LORA], c + MLA_Q_LORA
    w_kvlat, c = w[:, c:c + MLA_KV_LORA], c + MLA_KV_LORA
    w_krope, c = w[:, c:c + MLA_ROPE], c + MLA_ROPE
    w_qs, c = w[:, c:c + SWA_WIDTH], c + SWA_WIDTH
    w_ks, c = w[:, c:c + SWA_KV_WIDTH], c + SWA_KV_WIDTH
    w_vs, c = w[:, c:c + SWA_KV_WIDTH], c + SWA_KV_WIDTH
    w_gate = w[:, c:c + MIX_WIDTH]
    wt = jnp.concatenate([w_qlat, w_kvlat, w_qs, w_vs, w_gate, w_krope], axis=1).T.astype(bf)
    ws = jnp.concatenate([w_kvlat, w_ks], axis=1).astype(bf)

    wq = w_q_b[0].T.reshape(MLA_HEADS, MLA_QK, MLA_Q_LORA)
    wqt = jnp.pad(wq, ((0, 0), (0, HEAD_PAD - MLA_QK), (0, 0))).reshape(MLA_HEADS * HEAD_PAD, MLA_Q_LORA)
    wkv = w_kv_b[0].reshape(MLA_KV_LORA, MLA_HEADS, MLA_NOPE + MLA_V)
    wk = jnp.pad(wkv[:, :, :MLA_NOPE], ((0, 0), (0, 0), (0, HEAD_PAD - MLA_NOPE)))
    wk = wk.reshape(MLA_KV_LORA, MLA_HEADS * HEAD_PAD).astype(bf)
    wvt = wkv[:, :, MLA_NOPE:].reshape(MLA_KV_LORA, MLA_WIDTH).T.astype(bf)
    wqt = wqt.astype(bf)

    inv = ROPE_THETA ** (-jnp.arange(ROPE_HALF, dtype=f32) / ROPE_HALF)
    inv_col = inv.reshape(ROPE_HALF, 1)
    pos_row = positions.reshape(batch, 1, seq)
    pos_col = positions.reshape(batch, seq, 1)

    tile4 = lambda rows: pl.BlockSpec((1, 1, rows, tm), lambda b, i: (b, i, 0, 0))
    qt, k, vt, qst, ks, vst, gt = pl.pallas_call(
        _proj_kernel,
        grid=(batch, nt),
        in_specs=[
            pl.BlockSpec((1, tm, D_MODEL), lambda b, i: (b, i, 0)),
            pl.BlockSpec((1, 1, tm), lambda b, i: (b, 0, i)),
            _const_spec((1, D_MODEL)),
            _const_spec((_T_ROWS, D_MODEL)),
            _const_spec((D_MODEL, _S_COLS)),
            _const_spec((MLA_Q_LORA, 1)),
            _const_spec((MLA_KV_LORA, 1)),
            _const_spec((1, MLA_KV_LORA)),
            _const_spec((MLA_HEADS * HEAD_PAD, MLA_Q_LORA)),
            _const_spec((MLA_WIDTH, MLA_KV_LORA)),
            _const_spec((MLA_KV_LORA, MLA_HEADS * HEAD_PAD)),
            _const_spec((ROPE_HALF, 1)),
        ],
        out_specs=[
            tile4(MLA_HEADS * HEAD_PAD),
            pl.BlockSpec((1, MLA_HEADS, tm, HEAD_PAD), lambda b, i: (b, 0, i, 0)),
            tile4(MLA_HEADS * V_EXT),
            tile4(SWA_WIDTH),
            pl.BlockSpec((1, tm, SWA_KV_WIDTH), lambda b, i: (b, i, 0)),
            tile4(SWA_KV_HEADS * SWA_V_EXT),
            tile4(MIX_WIDTH),
        ],
        out_shape=[
            jax.ShapeDtypeStruct((batch, nt, MLA_HEADS * HEAD_PAD, tm), bf),
            jax.ShapeDtypeStruct((batch, MLA_HEADS, seq, HEAD_PAD), bf),
            jax.ShapeDtypeStruct((batch, nt, MLA_HEADS * V_EXT, tm), bf),
            jax.ShapeDtypeStruct((batch, nt, SWA_WIDTH, tm), bf),
            jax.ShapeDtypeStruct((batch, seq, SWA_KV_WIDTH), bf),
            jax.ShapeDtypeStruct((batch, nt, SWA_KV_HEADS * SWA_V_EXT, tm), bf),
            jax.ShapeDtypeStruct((batch, nt, MIX_WIDTH, tm), bf),
        ],
        compiler_params=_params(2),
        name="proj",
    )(x, pos_row, norm_gain[0].reshape(1, D_MODEL), wt, ws,
      q_a_norm[0].reshape(MLA_Q_LORA, 1), kv_a_norm[0].reshape(MLA_KV_LORA, 1),
      kv_a_norm[0].reshape(1, MLA_KV_LORA), wqt, wvt, wk, inv_col)

    schedule, n_iters = _mla_schedule(nt)
    oat = pl.pallas_call(
        functools.partial(_mla_kernel, n_iters=n_iters),
        grid=(batch, MLA_HEADS),
        in_specs=[
            pl.BlockSpec(memory_space=pltpu.SMEM),
            pl.BlockSpec((1, nt, HEAD_PAD, tm), lambda b, h: (b, 0, h, 0)),
            pl.BlockSpec((1, 1, seq, HEAD_PAD), lambda b, h: (b, h, 0, 0)),
            pl.BlockSpec((1, nt, V_EXT, tm), lambda b, h: (b, 0, h, 0)),
        ],
        out_specs=pl.BlockSpec((1, nt, MLA_V, tm), lambda b, h: (b, 0, h, 0)),
        out_shape=jax.ShapeDtypeStruct((batch, nt, MLA_WIDTH, tm), bf),
        scratch_shapes=[pltpu.VMEM((MLA_UNROLL, tm, tm), f32), pltpu.VMEM((2, MLA_UNROLL, tm, tm), bf),
                        pltpu.VMEM((2, tm, tm), f32)],
        compiler_params=_params(2),
        name="mla",
    )(jnp.asarray(schedule), qt, k, vt)

    prev = lambda i: jnp.maximum(i * per - 1, 0)
    bias_t = jnp.pad(rel_bias.T.astype(f32), ((0, 0), (0, LANES - REL_BUCKETS)))
    obt = pl.pallas_call(
        _swa_kernel,
        grid=(batch, nt),
        in_specs=[
            _const_spec((SWA_HEADS, LANES)),
            pl.BlockSpec(memory_space=pltpu.SMEM),
            tile4(SWA_WIDTH),
            pl.BlockSpec((1, BLOCK, SWA_KV_WIDTH), lambda b, i: (b, prev(i), 0)),
            pl.BlockSpec((1, tm, SWA_KV_WIDTH), lambda b, i: (b, i, 0)),
            pl.BlockSpec((1, 1, SWA_KV_HEADS * SWA_V_EXT, BLOCK),
                         lambda b, i: (b, prev(i) // per, 0, prev(i) % per)),
            tile4(SWA_KV_HEADS * SWA_V_EXT),
            pl.BlockSpec((1, 1, tm), lambda b, i: (b, 0, i)),
            pl.BlockSpec((1, BLOCK, 1), lambda b, i: (b, prev(i), 0)),
            pl.BlockSpec((1, tm, 1), lambda b, i: (b, i, 0)),
        ],
        out_specs=tile4(SWA_WIDTH),
        out_shape=jax.ShapeDtypeStruct((batch, nt, SWA_WIDTH, tm), bf),
        scratch_shapes=[pltpu.VMEM((_SWA_SLOT_BLOCK + per, SWA_KV_HEADS, 2 * BLOCK,
                                    (SWA_HEADS // SWA_KV_HEADS) * BLOCK), f32)],
        compiler_params=_params(2),
        name="swa",
    )(bias_t, sinks[0].astype(f32), qst, ks, ks, vst, vst, pos_row, pos_col, pos_col)

    return pl.pallas_call(
        _out_kernel,
        grid=(batch, nt),
        in_specs=[
            tile4(MLA_WIDTH),
            tile4(SWA_WIDTH),
            tile4(MIX_WIDTH),
            pl.BlockSpec((1, tm, D_MODEL), lambda b, i: (b, i, 0)),
            _const_spec((MIX_WIDTH, D_MODEL)),
            _const_spec((1, D_MODEL)),
        ],
        out_specs=pl.BlockSpec((1, tm, D_MODEL), lambda b, i: (b, i, 0)),
        out_shape=jax.ShapeDtypeStruct((batch, seq, D_MODEL), x.dtype),
        compiler_params=_params(2),
        name="out",
    )(oat, obt, gt, x, w_out[0].astype(bf), final_norm.reshape(1, D_MODEL))
```

```python
import functools
import math

import jax
import numpy as np
import jax.numpy as jnp
from jax import lax
from jax.experimental import pallas as pl
from jax.experimental.pallas import tpu as pltpu

D_MODEL = 1024
MLA_HEADS = 8
MLA_Q_LORA = 256
MLA_KV_LORA = 128
MLA_NOPE = 64
MLA_ROPE = 32
MLA_V = 64
SWA_HEADS = 8
SWA_KV_HEADS = 2
SWA_HEAD_DIM = 64
WINDOW = 128
BLOCK = 128
REL_BUCKETS = 32
REL_MAX_DIST = 128
ROPE_THETA = 10000.0
EPS = 1e-6

MLA_QK = MLA_NOPE + MLA_ROPE
MLA_WIDTH = MLA_HEADS * MLA_V
SWA_WIDTH = SWA_HEADS * SWA_HEAD_DIM
SWA_KV_WIDTH = SWA_KV_HEADS * SWA_HEAD_DIM
MIX_WIDTH = MLA_WIDTH + SWA_WIDTH
ROPE_HALF = MLA_ROPE // 2

LANES = 128
HEAD_PAD = LANES
BF16_ROWS = 16
V_EXT = MLA_V + BF16_ROWS
SWA_V_EXT = SWA_HEAD_DIM + BF16_ROWS
LOG2E = math.log2(math.e)
SEQ_TILE = 512
MLA_UNROLL = 4
MLA_STAGES = 3
PROJ_TILES = 2
OUT_TILES = 2
SWA_LOOKAHEAD = 2
NEG_BIG = -1e30
VMEM_LIMIT = 56 * 1024 * 1024

_T_QLAT = 0
_T_KVLAT = _T_QLAT + MLA_Q_LORA
_T_QS = _T_KVLAT + MLA_KV_LORA
_T_VS = _T_QS + SWA_WIDTH
_T_GATE = _T_VS + SWA_KV_WIDTH
_T_KROPE = _T_GATE + MIX_WIDTH
_T_ROWS = _T_KROPE + MLA_ROPE
_S_KVLAT = 0
_S_KS = LANES
_S_COLS = 2 * LANES


def _rms_rows(v, gain_col):
    ms = jnp.mean(v * v, axis=0, keepdims=True)
    return v * lax.rsqrt(ms + EPS) * gain_col


def _rms_lanes(v, gain_row):
    ms = jnp.mean(v * v, axis=-1, keepdims=True)
    return v * lax.rsqrt(ms + EPS) * gain_row


def _dot(a, b):
    return jnp.dot(a, b, preferred_element_type=jnp.float32)


def _dot_nt(a, b):
    return lax.dot_general(a, b, (((1,), (1,)), ((), ())), preferred_element_type=jnp.float32)


def _dot_tn(a, b):
    return lax.dot_general(a, b, (((0,), (0,)), ((), ())), preferred_element_type=jnp.float32)


def _proj_kernel(x_ref, prow_ref, g1_ref, wt_ref, ws_ref, gq_ref, gkvc_ref, gkvr_ref,
                 wqt_ref, wvt_ref, wk_ref, invc_ref,
                 qt_ref, k_ref, vt_ref, qst_ref, ks_ref, vst_ref, gt_ref):
    bf = jnp.bfloat16
    part = x_ref.shape[1] // PROJ_TILES

    def normed(p):
        return _rms_lanes(x_ref[0, p * part:(p + 1) * part], g1_ref[...]).astype(bf)

    def project(h):
        return _dot_nt(wt_ref[...], h), _dot(h, ws_ref[...])

    def finish(p, pt, ps):
        tok = slice(p * part, (p + 1) * part)
        qn = _rms_rows(pt[_T_QLAT:_T_KVLAT], gq_ref[...]).astype(bf)
        qt = _dot(wqt_ref[...], qn) * (MLA_QK ** -0.5 * LOG2E)
        ang_t = invc_ref[...] * prow_ref[0, :, tok].astype(jnp.float32)
        cos_t, sin_t = jnp.cos(ang_t), jnp.sin(ang_t)
        for hd in range(MLA_HEADS):
            r0 = hd * HEAD_PAD
            x1 = qt[r0 + MLA_NOPE:r0 + MLA_NOPE + ROPE_HALF]
            x2 = qt[r0 + MLA_NOPE + ROPE_HALF:r0 + MLA_QK]
            qt_ref[0, p, r0:r0 + MLA_NOPE, :] = qt[r0:r0 + MLA_NOPE].astype(bf)
            qt_ref[0, p, r0 + MLA_NOPE:r0 + MLA_NOPE + ROPE_HALF, :] = (x1 * cos_t - x2 * sin_t).astype(bf)
            qt_ref[0, p, r0 + MLA_NOPE + ROPE_HALF:r0 + MLA_QK, :] = (x2 * cos_t + x1 * sin_t).astype(bf)
            qt_ref[0, p, r0 + MLA_QK:r0 + HEAD_PAD, :] = qt[r0 + MLA_QK:r0 + HEAD_PAD].astype(bf)

        kvn_t = _rms_rows(pt[_T_KVLAT:_T_QS], gkvc_ref[...]).astype(bf)
        vt = _dot(wvt_ref[...], kvn_t).astype(bf)
        ones_row = (lax.broadcasted_iota(jnp.int32, (BF16_ROWS, part), 0) == 0).astype(bf)
        for hd in range(MLA_HEADS):
            vt_ref[0, p, hd * V_EXT:hd * V_EXT + MLA_V, :] = vt[hd * MLA_V:(hd + 1) * MLA_V]
            vt_ref[0, p, hd * V_EXT + MLA_V:(hd + 1) * V_EXT, :] = ones_row
        kvn_s = _rms_lanes(ps[:, _S_KVLAT:_S_KS], gkvr_ref[...]).astype(bf)
        kn = _dot(kvn_s, wk_ref[...])
        y1 = pt[_T_KROPE:_T_KROPE + ROPE_HALF]
        y2 = pt[_T_KROPE + ROPE_HALF:_T_ROWS]
        kpe_t = jnp.concatenate([jnp.zeros((MLA_NOPE, part), jnp.float32),
                                 y1 * cos_t - y2 * sin_t, y2 * cos_t + y1 * sin_t,
                                 jnp.zeros((HEAD_PAD - MLA_QK, part), jnp.float32)], axis=0)
        kpe = kpe_t.T
        for hd in range(MLA_HEADS):
            k_ref[0, hd, tok] = (kn[:, hd * HEAD_PAD:(hd + 1) * HEAD_PAD] + kpe).astype(bf)

        qst_ref[0, p] = (pt[_T_QS:_T_VS] * (SWA_HEAD_DIM ** -0.5 * LOG2E)).astype(bf)
        ks_ref[0, tok] = ps[:, _S_KS:_S_COLS].astype(bf)
        for hd in range(SWA_KV_HEADS):
            vst_ref[0, p, hd * SWA_V_EXT:hd * SWA_V_EXT + SWA_HEAD_DIM, :] = (
                pt[_T_VS + hd * SWA_HEAD_DIM:_T_VS + (hd + 1) * SWA_HEAD_DIM].astype(bf))
            vst_ref[0, p, hd * SWA_V_EXT + SWA_HEAD_DIM:(hd + 1) * SWA_V_EXT, :] = ones_row
        gt_ref[0, p] = pt[_T_GATE:_T_KROPE].astype(bf)

    pt, ps = project(normed(0))
    for p in range(1, PROJ_TILES):
        h_next = normed(p)
        finish(p - 1, pt, ps)
        pt, ps = project(h_next)
    finish(PROJ_TILES - 1, pt, ps)


def _mla_schedule(n_tiles):
    rows, todo = [], list(range(n_tiles))
    while todo:
        gap = -len(rows) % MLA_UNROLL
        q = min(t for t in todo if t + 1 > gap)
        todo.remove(q)
        keys = list(range(q))
        keys.insert(gap, q)
        rows += [(q, kt, int(kt == q), int(j == 0)) for j, kt in enumerate(keys)]
    assert len(rows) % MLA_UNROLL == 0 and all(r[2] == 0 for j, r in enumerate(rows) if j % MLA_UNROLL)
    pad = (MLA_STAGES - 1) * MLA_UNROLL
    rows = [rows[0]] * pad + rows + [rows[-1]] * pad
    return np.asarray(rows, np.int32).T.copy(), len(rows) // MLA_UNROLL - (MLA_STAGES - 1)


def _mla_kernel(tab_ref, qt_ref, k_ref, vt_ref, ot_ref, s_scr, p_scr, mask_scr, *, n_iters):
    bf = jnp.bfloat16
    u_n = MLA_UNROLL
    tq = SEQ_TILE

    @pl.when((pl.program_id(0) == 0) & (pl.program_id(1) == 0))
    def _():
        s_scr[...] = jnp.zeros(s_scr.shape, jnp.float32)
        p_scr[...] = jnp.zeros(p_scr.shape, bf)
        key = lax.broadcasted_iota(jnp.int32, (tq, tq), 0)
        qry = lax.broadcasted_iota(jnp.int32, (tq, tq), 1)
        mask_scr[0] = jnp.zeros((tq, tq), jnp.float32)
        mask_scr[1] = jnp.where(key <= qry, 0.0, NEG_BIG)

    def iteration(i, half, carry):
        m, acc, m_tiles, alphas = carry
        state = {"m": m, "acc": acc}
        new_tiles, new_alphas = [None] * u_n, [None] * u_n

        def value_stage(u):
            f = i * u_n + u
            pv = _dot(vt_ref[0, tab_ref[1, f]], p_scr[half, u])
            acc = jnp.where(tab_ref[3, f] == 1, pv, alphas[u] * state["acc"] + pv)
            ot_ref[0, tab_ref[0, f]] = (acc[:MLA_V] * (1.0 / acc[MLA_V:MLA_V + 1])).astype(bf)
            state["acc"] = acc

        def exp_stage(u):
            f = (i + 1) * u_n + u
            m_in = jnp.where(tab_ref[3, f] == 1, NEG_BIG, state["m"])
            m = jnp.maximum(m_in, m_tiles[u])
            new_alphas[u] = jnp.exp2(m_in - m)
            p_scr[1 - half, u] = jnp.exp2(s_scr[u] - m).astype(bf)
            state["m"] = m

        def score_stage(u):
            f = (i + 2) * u_n + u
            kj = k_ref[0, 0, pl.ds(pl.multiple_of(tab_ref[1, f] * tq, tq), tq), :]
            st = _dot(kj, qt_ref[0, tab_ref[0, f]])
            if u == 0:
                st = st + mask_scr[tab_ref[2, f]]
            s_scr[u] = st
            new_tiles[u] = jnp.max(st, axis=0, keepdims=True)

        for u in range(u_n):
            value_stage(u)
            exp_stage(u)
            score_stage(u)
        return state["m"], state["acc"], tuple(new_tiles), tuple(new_alphas)

    def body(i2, carry):
        return iteration(2 * i2 + 1, 1, iteration(2 * i2, 0, carry))

    assert n_iters % 2 == 0

    row = mask_scr[0, 0:1, :]
    lax.fori_loop(0, n_iters // 2, body,
                  (row, jnp.zeros((V_EXT, tq), jnp.float32), (row,) * u_n, (row,) * u_n))


def _swa_logit_offsets(bias_ref, qpos, kpos, first_key):
    shape = (2 * BLOCK, BLOCK)
    n = jnp.maximum(qpos - kpos, 0)
    max_exact = REL_BUCKETS // 2
    nf = jnp.maximum(n, 1).astype(jnp.float32)
    large = max_exact + (jnp.log(nf / max_exact) / math.log(REL_MAX_DIST / max_exact)
                         * (REL_BUCKETS - max_exact)).astype(jnp.int32)
    bucket = jnp.where(n < max_exact, n, jnp.minimum(large, REL_BUCKETS - 1))
    ki = lax.broadcasted_iota(jnp.int32, shape, 0)
    qi = lax.broadcasted_iota(jnp.int32, shape, 1)
    delta = qi + BLOCK - ki
    valid = (delta >= 0) & (delta < WINDOW) & (ki >= first_key)
    tiles = []
    for hd in range(SWA_HEADS):
        table = jnp.broadcast_to(bias_ref[hd:hd + 1, :], shape)
        bias = jnp.take_along_axis(table, bucket, axis=1, mode="promise_in_bounds")
        tiles.append(jnp.where(valid, bias * LOG2E, NEG_BIG))
    return tiles


def _swa_store_offsets(off_scr, slot, tiles):
    groups = SWA_HEADS // SWA_KV_HEADS
    for hd, tile in enumerate(tiles):
        off_scr[slot, hd // groups, :, (hd % groups) * BLOCK:(hd % groups + 1) * BLOCK] = tile


_SWA_SLOT_RUN = 0
_SWA_SLOT_RUN_FIRST = 1
_SWA_SLOT_BLOCK = 2


def _swa_kernel(bias_ref, sink_ref, qst_ref, kp_ref, kc_ref, vp_ref, vc_ref,
                prow_ref, ppcol_ref, pccol_ref, ot_ref, off_scr):
    bf = jnp.bfloat16
    step = pl.program_id(1)
    per = SEQ_TILE // BLOCK
    groups = SWA_HEADS // SWA_KV_HEADS

    @pl.when((pl.program_id(0) == 0) & (step == 0))
    def _():
        run_q = lax.broadcasted_iota(jnp.int32, (1, BLOCK), 1) + BLOCK
        run_k = lax.broadcasted_iota(jnp.int32, (2 * BLOCK, 1), 0)
        for slot, first_key in ((_SWA_SLOT_RUN, 0), (_SWA_SLOT_RUN_FIRST, BLOCK)):
            _swa_store_offsets(off_scr, slot, _swa_logit_offsets(bias_ref, run_q, run_k, first_key))

    pos = prow_ref[0]
    prev = ppcol_ref[0]
    base = pos[:, 0:1]
    run_pos = pos - base == lax.broadcasted_iota(jnp.int32, pos.shape, 1)
    run_prev = prev - base + BLOCK == lax.broadcasted_iota(jnp.int32, prev.shape, 0)
    consecutive = (jnp.min(jnp.where(run_pos, 1.0, 0.0)) > 0.5) & (
        (jnp.min(jnp.where(run_prev, 1.0, 0.0)) > 0.5) | (step == 0))

    @pl.when(jnp.logical_not(consecutive))
    def _():
        cur = pccol_ref[0]
        for j in range(per):
            kpos = (jnp.concatenate([prev, cur[:BLOCK]], axis=0) if j == 0
                    else cur[(j - 1) * BLOCK:(j + 1) * BLOCK])
            first_key = jnp.where(step == 0, BLOCK, 0) if j == 0 else 0
            tiles = _swa_logit_offsets(bias_ref, pos[:, j * BLOCK:(j + 1) * BLOCK], kpos, first_key)
            _swa_store_offsets(off_scr, _SWA_SLOT_BLOCK + j, tiles)

    def scores(j, kvh):
        q0 = j * BLOCK
        if j == 0:
            slot = jnp.where(consecutive, jnp.where(step == 0, _SWA_SLOT_RUN_FIRST, _SWA_SLOT_RUN),
                             _SWA_SLOT_BLOCK)
            kband = jnp.concatenate([kp_ref[0], kc_ref[0, :BLOCK]], axis=0)
        else:
            slot = jnp.where(consecutive, _SWA_SLOT_RUN, _SWA_SLOT_BLOCK + j)
            kband = kc_ref[0, q0 - BLOCK:q0 + BLOCK]
        k_h = kband[:, kvh * SWA_HEAD_DIM:(kvh + 1) * SWA_HEAD_DIM]
        qcat = jnp.concatenate(
            [qst_ref[0, 0, (kvh * groups + g) * SWA_HEAD_DIM:(kvh * groups + g + 1) * SWA_HEAD_DIM,
                     q0:q0 + BLOCK] for g in range(groups)], axis=1)
        return _dot(k_h, qcat) + off_scr[slot, kvh]

    def attend(j, kvh, s):
        q0 = j * BLOCK
        vband = (jnp.concatenate([vp_ref[0, 0], vc_ref[0, 0, :, :BLOCK]], axis=1) if j == 0
                 else vc_ref[0, 0, :, q0 - BLOCK:q0 + BLOCK])
        v_h = vband[kvh * SWA_V_EXT:(kvh + 1) * SWA_V_EXT]
        sink = jnp.concatenate([jnp.full((1, BLOCK), sink_ref[kvh * groups + g] * LOG2E, jnp.float32)
                                for g in range(groups)], axis=1)
        m = jnp.maximum(jnp.max(s, axis=0, keepdims=True), sink)
        ov = _dot(v_h, jnp.exp2(s - m).astype(bf))
        denom = ov[SWA_HEAD_DIM:SWA_HEAD_DIM + 1] + jnp.exp2(sink - m)
        out = (ov[:SWA_HEAD_DIM] * (1.0 / denom)).astype(bf)
        for g in range(groups):
            hd = kvh * groups + g
            ot_ref[0, 0, hd * SWA_HEAD_DIM:(hd + 1) * SWA_HEAD_DIM, q0:q0 + BLOCK] = (
                out[:, g * BLOCK:(g + 1) * BLOCK])

    chains = [(j, kvh) for j in range(per) for kvh in range(SWA_KV_HEADS)]
    ahead = [scores(*chain) for chain in chains[:SWA_LOOKAHEAD]]
    for c, chain in enumerate(chains):
        if c + SWA_LOOKAHEAD < len(chains):
            ahead.append(scores(*chains[c + SWA_LOOKAHEAD]))
        attend(*chain, ahead.pop(0))


def _out_kernel(oat_ref, obt_ref, gt_ref, x_ref, wo_ref, fn_ref, o_ref):
    tm = SEQ_TILE

    def gated(t):
        h = gt_ref[0, t].astype(jnp.float32) * 0.5
        silu = (h * (1.0 + jnp.tanh(h))).astype(jnp.bfloat16)
        return jnp.concatenate([oat_ref[0, t], obt_ref[0, t]], axis=0) * silu

    def finish(t, proj):
        z = x_ref[0, t * tm:(t + 1) * tm] + proj
        o_ref[0, t * tm:(t + 1) * tm] = _rms_lanes(z, fn_ref[...])

    proj = _dot_tn(gated(0), wo_ref[...])
    for t in range(1, OUT_TILES):
        y_next = gated(t)
        finish(t - 1, proj)
        proj = _dot_tn(y_next, wo_ref[...])
    finish(OUT_TILES - 1, proj)


def _const_spec(shape):
    return pl.BlockSpec(shape, lambda *_: (0,) * len(shape))


def _params(n_axes):
    return pltpu.CompilerParams(dimension_semantics=("arbitrary",) * n_axes,
                                vmem_limit_bytes=VMEM_LIMIT)


def kernel(x, positions, norm_gain, w_in, q_a_norm, w_q_b, kv_a_norm, w_kv_b, sinks, rel_bias, w_out,
           final_norm):
    bf, f32 = jnp.bfloat16, jnp.float32
    batch, seq, _ = x.shape
    assert norm_gain.shape[0] == 1, "single-layer kernel"
    tm = SEQ_TILE
    nt = seq // tm
    nb = seq // BLOCK
    per = tm // BLOCK

    w = w_in[0]
    c = 0
    w_qlat, c = w[:, c:c + MLA_Q_LORA], c + MLA_Q_LORA
    w_kvlat, c = w[:, c:c + MLA_KV_LORA], c + MLA_KV_LORA
    w_krope, c = w[:, c:c + MLA_ROPE], c + MLA_ROPE
    w_qs, c = w[:, c:c + SWA_WIDTH], c + SWA_WIDTH
    w_ks, c = w[:, c:c + SWA_KV_WIDTH], c + SWA_KV_WIDTH
    w_vs, c = w[:, c:c + SWA_KV_WIDTH], c + SWA_KV_WIDTH
    w_gate = w[:, c:c + MIX_WIDTH]
    wt = jnp.concatenate([w_qlat, w_kvlat, w_qs, w_vs, w_gate, w_krope], axis=1).T.astype(bf)
    ws = jnp.concatenate([w_kvlat, w_ks], axis=1).astype(bf)

    wq = w_q_b[0].T.reshape(MLA_HEADS, MLA_QK, MLA_Q_LORA)
    wqt = jnp.pad(wq, ((0, 0), (0, HEAD_PAD - MLA_QK), (0, 0))).reshape(MLA_HEADS * HEAD_PAD, MLA_Q_LORA)
    wkv = w_kv_b[0].reshape(MLA_KV_LORA, MLA_HEADS, MLA_NOPE + MLA_V)
    wk = jnp.pad(wkv[:, :, :MLA_NOPE], ((0, 0), (0, 0), (0, HEAD_PAD - MLA_NOPE)))
    wk = wk.reshape(MLA_KV_LORA, MLA_HEADS * HEAD_PAD).astype(bf)
    wvt = wkv[:, :, MLA_NOPE:].reshape(MLA_KV_LORA, MLA_WIDTH).T.astype(bf)
    wqt = wqt.astype(bf)

    inv = ROPE_THETA ** (-jnp.arange(ROPE_HALF, dtype=f32) / ROPE_HALF)
    inv_col = inv.reshape(ROPE_HALF, 1)
    pos_row = positions.reshape(batch, 1, seq)
    pos_col = positions.reshape(batch, seq, 1)

    tile4 = lambda rows: pl.BlockSpec((1, 1, rows, tm), lambda b, i: (b, i, 0, 0))
    ptiles = lambda rows: pl.BlockSpec((1, PROJ_TILES, rows, tm), lambda b, i: (b, i, 0, 0))
    ptm = PROJ_TILES * tm
    qt, k, vt, qst, ks, vst, gt = pl.pallas_call(
        _proj_kernel,
        grid=(batch, nt // PROJ_TILES),
        in_specs=[
            pl.BlockSpec((1, ptm, D_MODEL), lambda b, i: (b, i, 0)),
            pl.BlockSpec((1, 1, ptm), lambda b, i: (b, 0, i)),
            _const_spec((1, D_MODEL)),
            _const_spec((_T_ROWS, D_MODEL)),
            _const_spec((D_MODEL, _S_COLS)),
            _const_spec((MLA_Q_LORA, 1)),
            _const_spec((MLA_KV_LORA, 1)),
            _const_spec((1, MLA_KV_LORA)),
            _const_spec((MLA_HEADS * HEAD_PAD, MLA_Q_LORA)),
            _const_spec((MLA_WIDTH, MLA_KV_LORA)),
            _const_spec((MLA_KV_LORA, MLA_HEADS * HEAD_PAD)),
            _const_spec((ROPE_HALF, 1)),
        ],
        out_specs=[
            ptiles(MLA_HEADS * HEAD_PAD),
            pl.BlockSpec((1, MLA_HEADS, ptm, HEAD_PAD), lambda b, i: (b, 0, i, 0)),
            ptiles(MLA_HEADS * V_EXT),
            ptiles(SWA_WIDTH),
            pl.BlockSpec((1, ptm, SWA_KV_WIDTH), lambda b, i: (b, i, 0)),
            ptiles(SWA_KV_HEADS * SWA_V_EXT),
            ptiles(MIX_WIDTH),
        ],
        out_shape=[
            jax.ShapeDtypeStruct((batch, nt, MLA_HEADS * HEAD_PAD, tm), bf),
            jax.ShapeDtypeStruct((batch, MLA_HEADS, seq, HEAD_PAD), bf),
            jax.ShapeDtypeStruct((batch, nt, MLA_HEADS * V_EXT, tm), bf),
            jax.ShapeDtypeStruct((batch, nt, SWA_WIDTH, tm), bf),
            jax.ShapeDtypeStruct((batch, seq, SWA_KV_WIDTH), bf),
            jax.ShapeDtypeStruct((batch, nt, SWA_KV_HEADS * SWA_V_EXT, tm), bf),
            jax.ShapeDtypeStruct((batch, nt, MIX_WIDTH, tm), bf),
        ],
        compiler_params=_params(2),
        name="proj",
    )(x, pos_row, norm_gain[0].reshape(1, D_MODEL), wt, ws,
      q_a_norm[0].reshape(MLA_Q_LORA, 1), kv_a_norm[0].reshape(MLA_KV_LORA, 1),
      kv_a_norm[0].reshape(1, MLA_KV_LORA), wqt, wvt, wk, inv_col)

    schedule, n_iters = _mla_schedule(nt)
    oat = pl.pallas_call(
        functools.partial(_mla_kernel, n_iters=n_iters),
        grid=(batch, MLA_HEADS),
        in_specs=[
            pl.BlockSpec(memory_space=pltpu.SMEM),
            pl.BlockSpec((1, nt, HEAD_PAD, tm), lambda b, h: (b, 0, h, 0)),
            pl.BlockSpec((1, 1, seq, HEAD_PAD), lambda b, h: (b, h, 0, 0)),
            pl.BlockSpec((1, nt, V_EXT, tm), lambda b, h: (b, 0, h, 0)),
        ],
        out_specs=pl.BlockSpec((1, nt, MLA_V, tm), lambda b, h: (b, 0, h, 0)),
        out_shape=jax.ShapeDtypeStruct((batch, nt, MLA_WIDTH, tm), bf),
        scratch_shapes=[pltpu.VMEM((MLA_UNROLL, tm, tm), f32), pltpu.VMEM((2, MLA_UNROLL, tm, tm), bf),
                        pltpu.VMEM((2, tm, tm), f32)],
        compiler_params=_params(2),
        name="mla",
    )(jnp.asarray(schedule), qt, k, vt)

    prev = lambda i: jnp.maximum(i * per - 1, 0)
    bias_t = jnp.pad(rel_bias.T.astype(f32), ((0, 0), (0, LANES - REL_BUCKETS)))
    obt = pl.pallas_call(
        _swa_kernel,
        grid=(batch, nt),
        in_specs=[
            _const_spec((SWA_HEADS, LANES)),
            pl.BlockSpec(memory_space=pltpu.SMEM),
            tile4(SWA_WIDTH),
            pl.BlockSpec((1, BLOCK, SWA_KV_WIDTH), lambda b, i: (b, prev(i), 0)),
            pl.BlockSpec((1, tm, SWA_KV_WIDTH), lambda b, i: (b, i, 0)),
            pl.BlockSpec((1, 1, SWA_KV_HEADS * SWA_V_EXT, BLOCK),
                         lambda b, i: (b, prev(i) // per, 0, prev(i) % per)),
            tile4(SWA_KV_HEADS * SWA_V_EXT),
            pl.BlockSpec((1, 1, tm), lambda b, i: (b, 0, i)),
            pl.BlockSpec((1, BLOCK, 1), lambda b, i: (b, prev(i), 0)),
            pl.BlockSpec((1, tm, 1), lambda b, i: (b, i, 0)),
        ],
        out_specs=tile4(SWA_WIDTH),
        out_shape=jax.ShapeDtypeStruct((batch, nt, SWA_WIDTH, tm), bf),
        scratch_shapes=[pltpu.VMEM((_SWA_SLOT_BLOCK + per, SWA_KV_HEADS, 2 * BLOCK,
                                    (SWA_HEADS // SWA_KV_HEADS) * BLOCK), f32)],
        compiler_params=_params(2),
        name="swa",
    )(bias_t, sinks[0].astype(f32), qst, ks, ks, vst, vst, pos_row, pos_col, pos_col)

    tiles = lambda rows: pl.BlockSpec((1, OUT_TILES, rows, tm), lambda b, i: (b, i, 0, 0))
    return pl.pallas_call(
        _out_kernel,
        grid=(batch, nt // OUT_TILES),
        in_specs=[
            tiles(MLA_WIDTH),
            tiles(SWA_WIDTH),
            tiles(MIX_WIDTH),
            pl.BlockSpec((1, OUT_TILES * tm, D_MODEL), lambda b, i: (b, i, 0)),
            _const_spec((MIX_WIDTH, D_MODEL)),
            _const_spec((1, D_MODEL)),
        ],
        out_specs=pl.BlockSpec((1, OUT_TILES * tm, D_MODEL), lambda b, i: (b, i, 0)),
        out_shape=jax.ShapeDtypeStruct((batch, seq, D_MODEL), x.dtype),
        compiler_params=_params(2),
        name="out",
    )(oat, obt, gt, x, w_out[0].astype(bf), final_norm.reshape(1, D_MODEL))
```

```python
import functools
import math

import jax
import numpy as np
import jax.numpy as jnp
from jax import lax
from jax.experimental import pallas as pl
from jax.experimental.pallas import tpu as pltpu

D_MODEL = 1024
MLA_HEADS = 8
MLA_Q_LORA = 256
MLA_KV_LORA = 128
MLA_NOPE = 64
MLA_ROPE = 32
MLA_V = 64
SWA_HEADS = 8
SWA_KV_HEADS = 2
SWA_HEAD_DIM = 64
WINDOW = 128
BLOCK = 128
REL_BUCKETS = 32
REL_MAX_DIST = 128
ROPE_THETA = 10000.0
EPS = 1e-6

MLA_QK = MLA_NOPE + MLA_ROPE
MLA_WIDTH = MLA_HEADS * MLA_V
SWA_WIDTH = SWA_HEADS * SWA_HEAD_DIM
SWA_KV_WIDTH = SWA_KV_HEADS * SWA_HEAD_DIM
MIX_WIDTH = MLA_WIDTH + SWA_WIDTH
ROPE_HALF = MLA_ROPE // 2

LANES = 128
HEAD_PAD = LANES
BF16_ROWS = 16
V_EXT = MLA_V + BF16_ROWS
SWA_V_EXT = SWA_HEAD_DIM + BF16_ROWS
LOG2E = math.log2(math.e)
SEQ_TILE = 512
MLA_UNROLL = 4
MLA_STAGES = 3
MLA_GROUP = 2
PROJ_TILES = 2
OUT_TILES = 2
SWA_LOOKAHEAD = 2
NEG_BIG = -1e30
VMEM_LIMIT = 56 * 1024 * 1024

_T_QLAT = 0
_T_KVLAT = _T_QLAT + MLA_Q_LORA
_T_QS = _T_KVLAT + MLA_KV_LORA
_T_VS = _T_QS + SWA_WIDTH
_T_GATE = _T_VS + SWA_KV_WIDTH
_T_KROPE = _T_GATE + MIX_WIDTH
_T_ROWS = _T_KROPE + MLA_ROPE
_S_KVLAT = 0
_S_KS = LANES
_S_COLS = 2 * LANES


def _rms_rows(v, gain_col):
    ms = jnp.mean(v * v, axis=0, keepdims=True)
    return v * lax.rsqrt(ms + EPS) * gain_col


def _rms_lanes(v, gain_row):
    ms = jnp.mean(v * v, axis=-1, keepdims=True)
    return v * lax.rsqrt(ms + EPS) * gain_row


def _dot(a, b):
    return jnp.dot(a, b, preferred_element_type=jnp.float32)


def _dot_nt(a, b):
    return lax.dot_general(a, b, (((1,), (1,)), ((), ())), preferred_element_type=jnp.float32)


def _dot_tn(a, b):
    return lax.dot_general(a, b, (((0,), (0,)), ((), ())), preferred_element_type=jnp.float32)


def _proj_kernel(x_ref, prow_ref, g1_ref, wt_ref, ws_ref, gq_ref, gkvc_ref, gkvr_ref,
                 wqt_ref, wvt_ref, wk_ref, invc_ref,
                 qt_ref, k_ref, vt_ref, qst_ref, ks_ref, vst_ref, gt_ref):
    bf = jnp.bfloat16
    part = x_ref.shape[1] // PROJ_TILES

    def normed(p):
        return _rms_lanes(x_ref[0, p * part:(p + 1) * part], g1_ref[...]).astype(bf)

    def project(h):
        return _dot_nt(wt_ref[...], h), _dot(h, ws_ref[...])

    def finish(p, pt, ps):
        tok = slice(p * part, (p + 1) * part)
        qn = _rms_rows(pt[_T_QLAT:_T_KVLAT], gq_ref[...]).astype(bf)
        qt = _dot(wqt_ref[...], qn) * (MLA_QK ** -0.5 * LOG2E)
        ang_t = invc_ref[...] * prow_ref[0, :, tok].astype(jnp.float32)
        cos_t, sin_t = jnp.cos(ang_t), jnp.sin(ang_t)
        for hd in range(MLA_HEADS):
            r0 = hd * HEAD_PAD
            x1 = qt[r0 + MLA_NOPE:r0 + MLA_NOPE + ROPE_HALF]
            x2 = qt[r0 + MLA_NOPE + ROPE_HALF:r0 + MLA_QK]
            qt_ref[0, p, r0:r0 + MLA_NOPE, :] = qt[r0:r0 + MLA_NOPE].astype(bf)
            qt_ref[0, p, r0 + MLA_NOPE:r0 + MLA_NOPE + ROPE_HALF, :] = (x1 * cos_t - x2 * sin_t).astype(bf)
            qt_ref[0, p, r0 + MLA_NOPE + ROPE_HALF:r0 + MLA_QK, :] = (x2 * cos_t + x1 * sin_t).astype(bf)
            qt_ref[0, p, r0 + MLA_QK:r0 + HEAD_PAD, :] = qt[r0 + MLA_QK:r0 + HEAD_PAD].astype(bf)

        kvn_t = _rms_rows(pt[_T_KVLAT:_T_QS], gkvc_ref[...]).astype(bf)
        vt = _dot(wvt_ref[...], kvn_t).astype(bf)
        ones_row = (lax.broadcasted_iota(jnp.int32, (BF16_ROWS, part), 0) == 0).astype(bf)
        for hd in range(MLA_HEADS):
            vt_ref[0, p, hd * V_EXT:hd * V_EXT + MLA_V, :] = vt[hd * MLA_V:(hd + 1) * MLA_V]
            vt_ref[0, p, hd * V_EXT + MLA_V:(hd + 1) * V_EXT, :] = ones_row
        kvn_s = _rms_lanes(ps[:, _S_KVLAT:_S_KS], gkvr_ref[...]).astype(bf)
        kn = _dot(kvn_s, wk_ref[...])
        y1 = pt[_T_KROPE:_T_KROPE + ROPE_HALF]
        y2 = pt[_T_KROPE + ROPE_HALF:_T_ROWS]
        kpe_t = jnp.concatenate([jnp.zeros((MLA_NOPE, part), jnp.float32),
                                 y1 * cos_t - y2 * sin_t, y2 * cos_t + y1 * sin_t,
                                 jnp.zeros((HEAD_PAD - MLA_QK, part), jnp.float32)], axis=0)
        kpe = kpe_t.T
        for hd in range(MLA_HEADS):
            k_ref[0, hd, tok] = (kn[:, hd * HEAD_PAD:(hd + 1) * HEAD_PAD] + kpe).astype(bf)

        qst_ref[0, p] = (pt[_T_QS:_T_VS] * (SWA_HEAD_DIM ** -0.5 * LOG2E)).astype(bf)
        ks_ref[0, tok] = ps[:, _S_KS:_S_COLS].astype(bf)
        for hd in range(SWA_KV_HEADS):
            vst_ref[0, p, hd * SWA_V_EXT:hd * SWA_V_EXT + SWA_HEAD_DIM, :] = (
                pt[_T_VS + hd * SWA_HEAD_DIM:_T_VS + (hd + 1) * SWA_HEAD_DIM].astype(bf))
            vst_ref[0, p, hd * SWA_V_EXT + SWA_HEAD_DIM:(hd + 1) * SWA_V_EXT, :] = ones_row
        gt_ref[0, p] = pt[_T_GATE:_T_KROPE].astype(bf)

    pt, ps = project(normed(0))
    for p in range(1, PROJ_TILES):
        h_next = normed(p)
        finish(p - 1, pt, ps)
        pt, ps = project(h_next)
    finish(PROJ_TILES - 1, pt, ps)


def _mla_schedule(n_tiles):
    rows, todo = [], list(range(n_tiles))
    while todo:
        gap = -len(rows) % MLA_UNROLL
        q = min(t for t in todo if t + 1 > gap)
        todo.remove(q)
        keys = list(range(q))
        keys.insert(gap, q)
        rows += [(q, kt, int(kt == q), int(j == 0)) for j, kt in enumerate(keys)]
    assert len(rows) % MLA_UNROLL == 0 and all(r[2] == 0 for j, r in enumerate(rows) if j % MLA_UNROLL)
    pad = (MLA_STAGES - 1) * MLA_UNROLL
    rows = [rows[0]] * pad + rows + [rows[-1]] * pad
    return np.asarray(rows, np.int32).T.copy(), len(rows) // MLA_UNROLL - (MLA_STAGES - 1)


def _mla_kernel(tab_ref, qt_ref, k_ref, vt_ref, ot_ref, s_scr, p_scr, mask_scr, *, n_iters):
    bf = jnp.bfloat16
    u_n = MLA_UNROLL
    tq = SEQ_TILE

    @pl.when((pl.program_id(0) == 0) & (pl.program_id(1) == 0))
    def _():
        s_scr[...] = jnp.zeros(s_scr.shape, jnp.float32)
        p_scr[...] = jnp.zeros(p_scr.shape, bf)
        key = lax.broadcasted_iota(jnp.int32, (tq, tq), 0)
        qry = lax.broadcasted_iota(jnp.int32, (tq, tq), 1)
        mask_scr[0] = jnp.zeros((tq, tq), jnp.float32)
        mask_scr[1] = jnp.where(key <= qry, 0.0, NEG_BIG)

    def iteration(i, half, carry):
        ms, accs, m_tiles, alphas = [list(c) for c in carry]
        new_tiles = [[None] * u_n for _ in range(MLA_GROUP)]
        new_alphas = [[None] * u_n for _ in range(MLA_GROUP)]

        def value_stage(g, u):
            f = i * u_n + u
            v_tile = vt_ref[0, tab_ref[1, f], g * V_EXT:(g + 1) * V_EXT, :]
            pv = _dot(v_tile, p_scr[half, g, u])
            acc = jnp.where(tab_ref[3, f] == 1, pv, alphas[g][u] * accs[g] + pv)
            ot_ref[0, tab_ref[0, f], g * MLA_V:(g + 1) * MLA_V, :] = (
                acc[:MLA_V] * (1.0 / acc[MLA_V:MLA_V + 1])).astype(bf)
            accs[g] = acc

        def exp_stage(g, u):
            f = (i + 1) * u_n + u
            m_in = jnp.where(tab_ref[3, f] == 1, NEG_BIG, ms[g])
            m = jnp.maximum(m_in, m_tiles[g][u])
            new_alphas[g][u] = jnp.exp2(m_in - m)
            p_scr[1 - half, g, u] = jnp.exp2(s_scr[g, u] - m).astype(bf)
            ms[g] = m

        def score_stage(g, u):
            f = (i + 2) * u_n + u
            kj = k_ref[0, g, pl.ds(pl.multiple_of(tab_ref[1, f] * tq, tq), tq), :]
            st = _dot(kj, qt_ref[0, tab_ref[0, f], g * HEAD_PAD:(g + 1) * HEAD_PAD, :])
            if u == 0:
                st = st + mask_scr[tab_ref[2, f]]
            s_scr[g, u] = st
            new_tiles[g][u] = jnp.max(st, axis=0, keepdims=True)

        for u in range(u_n):
            for g in range(MLA_GROUP):
                value_stage(g, u)
                exp_stage(g, u)
                score_stage(g, u)
        return (tuple(ms), tuple(accs), tuple(tuple(t) for t in new_tiles),
                tuple(tuple(a) for a in new_alphas))

    def body(i2, carry):
        return iteration(2 * i2 + 1, 1, iteration(2 * i2, 0, carry))

    assert n_iters % 2 == 0
    row = mask_scr[0, 0:1, :]
    acc0 = jnp.zeros((V_EXT, tq), jnp.float32)
    lax.fori_loop(0, n_iters // 2, body,
                  ((row,) * MLA_GROUP, (acc0,) * MLA_GROUP,
                   ((row,) * u_n,) * MLA_GROUP, ((row,) * u_n,) * MLA_GROUP))


def _swa_logit_offsets(bias_ref, qpos, kpos, first_key):
    shape = (2 * BLOCK, BLOCK)
    n = jnp.maximum(qpos - kpos, 0)
    max_exact = REL_BUCKETS // 2
    nf = jnp.maximum(n, 1).astype(jnp.float32)
    large = max_exact + (jnp.log(nf / max_exact) / math.log(REL_MAX_DIST / max_exact)
                         * (REL_BUCKETS - max_exact)).astype(jnp.int32)
    bucket = jnp.where(n < max_exact, n, jnp.minimum(large, REL_BUCKETS - 1))
    ki = lax.broadcasted_iota(jnp.int32, shape, 0)
    qi = lax.broadcasted_iota(jnp.int32, shape, 1)
    delta = qi + BLOCK - ki
    valid = (delta >= 0) & (delta < WINDOW) & (ki >= first_key)
    tiles = []
    for hd in range(SWA_HEADS):
        table = jnp.broadcast_to(bias_ref[hd:hd + 1, :], shape)
        bias = jnp.take_along_axis(table, bucket, axis=1, mode="promise_in_bounds")
        tiles.append(jnp.where(valid, bias * LOG2E, NEG_BIG))
    return tiles


def _swa_store_offsets(off_scr, slot, tiles):
    groups = SWA_HEADS // SWA_KV_HEADS
    for hd, tile in enumerate(tiles):
        off_scr[slot, hd // groups, :, (hd % groups) * BLOCK:(hd % groups + 1) * BLOCK] = tile


_SWA_SLOT_RUN = 0
_SWA_SLOT_RUN_FIRST = 1
_SWA_SLOT_BLOCK = 2


def _swa_kernel(bias_ref, sink_ref, qst_ref, kp_ref, kc_ref, vp_ref, vc_ref,
                prow_ref, ppcol_ref, pccol_ref, ot_ref, off_scr):
    bf = jnp.bfloat16
    step = pl.program_id(1)
    per = SEQ_TILE // BLOCK
    groups = SWA_HEADS // SWA_KV_HEADS

    @pl.when((pl.program_id(0) == 0) & (step == 0))
    def _():
        run_q = lax.broadcasted_iota(jnp.int32, (1, BLOCK), 1) + BLOCK
        run_k = lax.broadcasted_iota(jnp.int32, (2 * BLOCK, 1), 0)
        for slot, first_key in ((_SWA_SLOT_RUN, 0), (_SWA_SLOT_RUN_FIRST, BLOCK)):
            _swa_store_offsets(off_scr, slot, _swa_logit_offsets(bias_ref, run_q, run_k, first_key))

    pos = prow_ref[0]
    prev = ppcol_ref[0]
    base = pos[:, 0:1]
    run_pos = pos - base == lax.broadcasted_iota(jnp.int32, pos.shape, 1)
    run_prev = prev - base + BLOCK == lax.broadcasted_iota(jnp.int32, prev.shape, 0)
    consecutive = (jnp.min(jnp.where(run_pos, 1.0, 0.0)) > 0.5) & (
        (jnp.min(jnp.where(run_prev, 1.0, 0.0)) > 0.5) | (step == 0))

    @pl.when(jnp.logical_not(consecutive))
    def _():
        cur = pccol_ref[0]
        for j in range(per):
            kpos = (jnp.concatenate([prev, cur[:BLOCK]], axis=0) if j == 0
                    else cur[(j - 1) * BLOCK:(j + 1) * BLOCK])
            first_key = jnp.where(step == 0, BLOCK, 0) if j == 0 else 0
            tiles = _swa_logit_offsets(bias_ref, pos[:, j * BLOCK:(j + 1) * BLOCK], kpos, first_key)
            _swa_store_offsets(off_scr, _SWA_SLOT_BLOCK + j, tiles)

    def scores(j, kvh):
        q0 = j * BLOCK
        if j == 0:
            slot = jnp.where(consecutive, jnp.where(step == 0, _SWA_SLOT_RUN_FIRST, _SWA_SLOT_RUN),
                             _SWA_SLOT_BLOCK)
            kband = jnp.concatenate([kp_ref[0], kc_ref[0, :BLOCK]], axis=0)
        else:
            slot = jnp.where(consecutive, _SWA_SLOT_RUN, _SWA_SLOT_BLOCK + j)
            kband = kc_ref[0, q0 - BLOCK:q0 + BLOCK]
        k_h = kband[:, kvh * SWA_HEAD_DIM:(kvh + 1) * SWA_HEAD_DIM]
        qcat = jnp.concatenate(
            [qst_ref[0, 0, (kvh * groups + g) * SWA_HEAD_DIM:(kvh * groups + g + 1) * SWA_HEAD_DIM,
                     q0:q0 + BLOCK] for g in range(groups)], axis=1)
        return _dot(k_h, qcat) + off_scr[slot, kvh]

    def attend(j, kvh, s):
        q0 = j * BLOCK
        vband = (jnp.concatenate([vp_ref[0, 0], vc_ref[0, 0, :, :BLOCK]], axis=1) if j == 0
                 else vc_ref[0, 0, :, q0 - BLOCK:q0 + BLOCK])
        v_h = vband[kvh * SWA_V_EXT:(kvh + 1) * SWA_V_EXT]
        sink = jnp.concatenate([jnp.full((1, BLOCK), sink_ref[kvh * groups + g] * LOG2E, jnp.float32)
                                for g in range(groups)], axis=1)
        m = jnp.maximum(jnp.max(s, axis=0, keepdims=True), sink)
        ov = _dot(v_h, jnp.exp2(s - m).astype(bf))
        denom = ov[SWA_HEAD_DIM:SWA_HEAD_DIM + 1] + jnp.exp2(sink - m)
        out = (ov[:SWA_HEAD_DIM] * (1.0 / denom)).astype(bf)
        for g in range(groups):
            hd = kvh * groups + g
            ot_ref[0, 0, hd * SWA_HEAD_DIM:(hd + 1) * SWA_HEAD_DIM, q0:q0 + BLOCK] = (
                out[:, g * BLOCK:(g + 1) * BLOCK])

    chains = [(j, kvh) for j in range(per) for kvh in range(SWA_KV_HEADS)]
    ahead = [scores(*chain) for chain in chains[:SWA_LOOKAHEAD]]
    for c, chain in enumerate(chains):
        if c + SWA_LOOKAHEAD < len(chains):
            ahead.append(scores(*chains[c + SWA_LOOKAHEAD]))
        attend(*chain, ahead.pop(0))


def _out_kernel(oat_ref, obt_ref, gt_ref, x_ref, wo_ref, fn_ref, o_ref):
    tm = SEQ_TILE

    def gated(t):
        h = gt_ref[0, t].astype(jnp.float32) * 0.5
        silu = (h * (1.0 + jnp.tanh(h))).astype(jnp.bfloat16)
        return jnp.concatenate([oat_ref[0, t], obt_ref[0, t]], axis=0) * silu

    def finish(t, proj):
        z = x_ref[0, t * tm:(t + 1) * tm] + proj
        o_ref[0, t * tm:(t + 1) * tm] = _rms_lanes(z, fn_ref[...])

    proj = _dot_tn(gated(0), wo_ref[...])
    for t in range(1, OUT_TILES):
        y_next = gated(t)
        finish(t - 1, proj)
        proj = _dot_tn(y_next, wo_ref[...])
    finish(OUT_TILES - 1, proj)


def _const_spec(shape):
    return pl.BlockSpec(shape, lambda *_: (0,) * len(shape))


def _params(n_axes):
    return pltpu.CompilerParams(dimension_semantics=("arbitrary",) * n_axes,
                                vmem_limit_bytes=VMEM_LIMIT)


def kernel(x, positions, norm_gain, w_in, q_a_norm, w_q_b, kv_a_norm, w_kv_b, sinks, rel_bias, w_out,
           final_norm):
    bf, f32 = jnp.bfloat16, jnp.float32
    batch, seq, _ = x.shape
    assert norm_gain.shape[0] == 1, "single-layer kernel"
    tm = SEQ_TILE
    nt = seq // tm
    nb = seq // BLOCK
    per = tm // BLOCK

    w = w_in[0]
    c = 0
    w_qlat, c = w[:, c:c + MLA_Q_LORA], c + MLA_Q_LORA
    w_kvlat, c = w[:, c:c + MLA_KV_LORA], c + MLA_KV_LORA
    w_krope, c = w[:, c:c + MLA_ROPE], c + MLA_ROPE
    w_qs, c = w[:, c:c + SWA_WIDTH], c + SWA_WIDTH
    w_ks, c = w[:, c:c + SWA_KV_WIDTH], c + SWA_KV_WIDTH
    w_vs, c = w[:, c:c + SWA_KV_WIDTH], c + SWA_KV_WIDTH
    w_gate = w[:, c:c + MIX_WIDTH]
    wt = jnp.concatenate([w_qlat, w_kvlat, w_qs, w_vs, w_gate, w_krope], axis=1).T.astype(bf)
    ws = jnp.concatenate([w_kvlat, w_ks], axis=1).astype(bf)

    wq = w_q_b[0].T.reshape(MLA_HEADS, MLA_QK, MLA_Q_LORA)
    wqt = jnp.pad(wq, ((0, 0), (0, HEAD_PAD - MLA_QK), (0, 0))).reshape(MLA_HEADS * HEAD_PAD, MLA_Q_LORA)
    wkv = w_kv_b[0].reshape(MLA_KV_LORA, MLA_HEADS, MLA_NOPE + MLA_V)
    wk = jnp.pad(wkv[:, :, :MLA_NOPE], ((0, 0), (0, 0), (0, HEAD_PAD - MLA_NOPE)))
    wk = wk.reshape(MLA_KV_LORA, MLA_HEADS * HEAD_PAD).astype(bf)
    wvt = wkv[:, :, MLA_NOPE:].reshape(MLA_KV_LORA, MLA_WIDTH).T.astype(bf)
    wqt = wqt.astype(bf)

    inv = ROPE_THETA ** (-jnp.arange(ROPE_HALF, dtype=f32) / ROPE_HALF)
    inv_col = inv.reshape(ROPE_HALF, 1)
    pos_row = positions.reshape(batch, 1, seq)
    pos_col = positions.reshape(batch, seq, 1)

    tile4 = lambda rows: pl.BlockSpec((1, 1, rows, tm), lambda b, i: (b, i, 0, 0))
    ptiles = lambda rows: pl.BlockSpec((1, PROJ_TILES, rows, tm), lambda b, i: (b, i, 0, 0))
    ptm = PROJ_TILES * tm
    qt, k, vt, qst, ks, vst, gt = pl.pallas_call(
        _proj_kernel,
        grid=(batch, nt // PROJ_TILES),
        in_specs=[
            pl.BlockSpec((1, ptm, D_MODEL), lambda b, i: (b, i, 0)),
            pl.BlockSpec((1, 1, ptm), lambda b, i: (b, 0, i)),
            _const_spec((1, D_MODEL)),
            _const_spec((_T_ROWS, D_MODEL)),
            _const_spec((D_MODEL, _S_COLS)),
            _const_spec((MLA_Q_LORA, 1)),
            _const_spec((MLA_KV_LORA, 1)),
            _const_spec((1, MLA_KV_LORA)),
            _const_spec((MLA_HEADS * HEAD_PAD, MLA_Q_LORA)),
            _const_spec((MLA_WIDTH, MLA_KV_LORA)),
            _const_spec((MLA_KV_LORA, MLA_HEADS * HEAD_PAD)),
            _const_spec((ROPE_HALF, 1)),
        ],
        out_specs=[
            ptiles(MLA_HEADS * HEAD_PAD),
            pl.BlockSpec((1, MLA_HEADS, ptm, HEAD_PAD), lambda b, i: (b, 0, i, 0)),
            ptiles(MLA_HEADS * V_EXT),
            ptiles(SWA_WIDTH),
            pl.BlockSpec((1, ptm, SWA_KV_WIDTH), lambda b, i: (b, i, 0)),
            ptiles(SWA_KV_HEADS * SWA_V_EXT),
            ptiles(MIX_WIDTH),
        ],
        out_shape=[
            jax.ShapeDtypeStruct((batch, nt, MLA_HEADS * HEAD_PAD, tm), bf),
            jax.ShapeDtypeStruct((batch, MLA_HEADS, seq, HEAD_PAD), bf),
            jax.ShapeDtypeStruct((batch, nt, MLA_HEADS * V_EXT, tm), bf),
            jax.ShapeDtypeStruct((batch, nt, SWA_WIDTH, tm), bf),
            jax.ShapeDtypeStruct((batch, seq, SWA_KV_WIDTH), bf),
            jax.ShapeDtypeStruct((batch, nt, SWA_KV_HEADS * SWA_V_EXT, tm), bf),
            jax.ShapeDtypeStruct((batch, nt, MIX_WIDTH, tm), bf),
        ],
        compiler_params=_params(2),
        name="proj",
    )(x, pos_row, norm_gain[0].reshape(1, D_MODEL), wt, ws,
      q_a_norm[0].reshape(MLA_Q_LORA, 1), kv_a_norm[0].reshape(MLA_KV_LORA, 1),
      kv_a_norm[0].reshape(1, MLA_KV_LORA), wqt, wvt, wk, inv_col)

    schedule, n_iters = _mla_schedule(nt)
    oat = pl.pallas_call(
        functools.partial(_mla_kernel, n_iters=n_iters),
        grid=(batch, MLA_HEADS // MLA_GROUP),
        in_specs=[
            pl.BlockSpec(memory_space=pltpu.SMEM),
            pl.BlockSpec((1, nt, MLA_GROUP * HEAD_PAD, tm), lambda b, h: (b, 0, h, 0)),
            pl.BlockSpec((1, MLA_GROUP, seq, HEAD_PAD), lambda b, h: (b, h, 0, 0)),
            pl.BlockSpec((1, nt, MLA_GROUP * V_EXT, tm), lambda b, h: (b, 0, h, 0)),
        ],
        out_specs=pl.BlockSpec((1, nt, MLA_GROUP * MLA_V, tm), lambda b, h: (b, 0, h, 0)),
        out_shape=jax.ShapeDtypeStruct((batch, nt, MLA_WIDTH, tm), bf),
        scratch_shapes=[pltpu.VMEM((MLA_GROUP, MLA_UNROLL, tm, tm), f32),
                        pltpu.VMEM((2, MLA_GROUP, MLA_UNROLL, tm, tm), bf),
                        pltpu.VMEM((2, tm, tm), f32)],
        compiler_params=_params(2),
        name="mla",
    )(jnp.asarray(schedule), qt, k, vt)

    prev = lambda i: jnp.maximum(i * per - 1, 0)
    bias_t = jnp.pad(rel_bias.T.astype(f32), ((0, 0), (0, LANES - REL_BUCKETS)))
    obt = pl.pallas_call(
        _swa_kernel,
        grid=(batch, nt),
        in_specs=[
            _const_spec((SWA_HEADS, LANES)),
            pl.BlockSpec(memory_space=pltpu.SMEM),
            tile4(SWA_WIDTH),
            pl.BlockSpec((1, BLOCK, SWA_KV_WIDTH), lambda b, i: (b, prev(i), 0)),
            pl.BlockSpec((1, tm, SWA_KV_WIDTH), lambda b, i: (b, i, 0)),
            pl.BlockSpec((1, 1, SWA_KV_HEADS * SWA_V_EXT, BLOCK),
                         lambda b, i: (b, prev(i) // per, 0, prev(i) % per)),
            tile4(SWA_KV_HEADS * SWA_V_EXT),
            pl.BlockSpec((1, 1, tm), lambda b, i: (b, 0, i)),
            pl.BlockSpec((1, BLOCK, 1), lambda b, i: (b, prev(i), 0)),
            pl.BlockSpec((1, tm, 1), lambda b, i: (b, i, 0)),
        ],
        out_specs=tile4(SWA_WIDTH),
        out_shape=jax.ShapeDtypeStruct((batch, nt, SWA_WIDTH, tm), bf),
        scratch_shapes=[pltpu.VMEM((_SWA_SLOT_BLOCK + per, SWA_KV_HEADS, 2 * BLOCK,
                                    (SWA_HEADS // SWA_KV_HEADS) * BLOCK), f32)],
        compiler_params=_params(2),
        name="swa",
    )(bias_t, sinks[0].astype(f32), qst, ks, ks, vst, vst, pos_row, pos_col, pos_col)

    tiles = lambda rows: pl.BlockSpec((1, OUT_TILES, rows, tm), lambda b, i: (b, i, 0, 0))
    return pl.pallas_call(
        _out_kernel,
        grid=(batch, nt // OUT_TILES),
        in_specs=[
            tiles(MLA_WIDTH),
            tiles(SWA_WIDTH),
            tiles(MIX_WIDTH),
            pl.BlockSpec((1, OUT_TILES * tm, D_MODEL), lambda b, i: (b, i, 0)),
            _const_spec((MIX_WIDTH, D_MODEL)),
            _const_spec((1, D_MODEL)),
        ],
        out_specs=pl.BlockSpec((1, OUT_TILES * tm, D_MODEL), lambda b, i: (b, i, 0)),
        out_shape=jax.ShapeDtypeStruct((batch, seq, D_MODEL), x.dtype),
        compiler_params=_params(2),
        name="out",
    )(oat, obt, gt, x, w_out[0].astype(bf), final_norm.reshape(1, D_MODEL))
```

```python
import functools
import math

import jax
import numpy as np
import jax.numpy as jnp
from jax import lax
from jax.experimental import pallas as pl
from jax.experimental.pallas import tpu as pltpu

D_MODEL = 1024
MLA_HEADS = 8
MLA_Q_LORA = 256
MLA_KV_LORA = 128
MLA_NOPE = 64
MLA_ROPE = 32
MLA_V = 64
SWA_HEADS = 8
SWA_KV_HEADS = 2
SWA_HEAD_DIM = 64
WINDOW = 128
BLOCK = 128
REL_BUCKETS = 32
REL_MAX_DIST = 128
ROPE_THETA = 10000.0
EPS = 1e-6

MLA_QK = MLA_NOPE + MLA_ROPE
MLA_WIDTH = MLA_HEADS * MLA_V
SWA_WIDTH = SWA_HEADS * SWA_HEAD_DIM
SWA_KV_WIDTH = SWA_KV_HEADS * SWA_HEAD_DIM
MIX_WIDTH = MLA_WIDTH + SWA_WIDTH
ROPE_HALF = MLA_ROPE // 2

LANES = 128
HEAD_PAD = LANES
BF16_ROWS = 16
V_EXT = MLA_V + BF16_ROWS
SWA_V_EXT = SWA_HEAD_DIM + BF16_ROWS
LOG2E = math.log2(math.e)
SEQ_TILE = 512
MLA_UNROLL = 4
MLA_STAGES = 3
MLA_GROUP = 2
MLA_LANE_PAD = LANES
PROJ_TILES = 2
OUT_TILES = 2
SWA_LOOKAHEAD = 2
NEG_BIG = -1e30
VMEM_LIMIT = 56 * 1024 * 1024

_T_QLAT = 0
_T_KVLAT = _T_QLAT + MLA_Q_LORA
_T_QS = _T_KVLAT + MLA_KV_LORA
_T_VS = _T_QS + SWA_WIDTH
_T_GATE = _T_VS + SWA_KV_WIDTH
_T_KROPE = _T_GATE + MIX_WIDTH
_T_ROWS = _T_KROPE + MLA_ROPE
_S_KVLAT = 0
_S_KS = LANES
_S_COLS = 2 * LANES


def _rms_rows(v, gain_col):
    ms = jnp.mean(v * v, axis=0, keepdims=True)
    return v * lax.rsqrt(ms + EPS) * gain_col


def _rms_lanes(v, gain_row):
    ms = jnp.mean(v * v, axis=-1, keepdims=True)
    return v * lax.rsqrt(ms + EPS) * gain_row


def _dot(a, b):
    return jnp.dot(a, b, preferred_element_type=jnp.float32)


def _dot_nt(a, b):
    return lax.dot_general(a, b, (((1,), (1,)), ((), ())), preferred_element_type=jnp.float32)


def _dot_tn(a, b):
    return lax.dot_general(a, b, (((0,), (0,)), ((), ())), preferred_element_type=jnp.float32)


def _proj_kernel(x_ref, prow_ref, g1_ref, wt_ref, ws_ref, gq_ref, gkvc_ref, gkvr_ref,
                 wqt_ref, wvt_ref, wk_ref, invc_ref,
                 qt_ref, k_ref, vt_ref, qst_ref, ks_ref, vst_ref, gt_ref):
    bf = jnp.bfloat16
    part = x_ref.shape[1] // PROJ_TILES

    def normed(p):
        return _rms_lanes(x_ref[0, p * part:(p + 1) * part], g1_ref[...]).astype(bf)

    def project(h):
        return _dot_nt(wt_ref[...], h), _dot(h, ws_ref[...])

    def finish(p, pt, ps):
        tok = slice(p * part, (p + 1) * part)
        qn = _rms_rows(pt[_T_QLAT:_T_KVLAT], gq_ref[...]).astype(bf)
        qt = _dot(wqt_ref[...], qn) * (MLA_QK ** -0.5 * LOG2E)
        ang_t = invc_ref[...] * prow_ref[0, :, tok].astype(jnp.float32)
        cos_t, sin_t = jnp.cos(ang_t), jnp.sin(ang_t)
        for hd in range(MLA_HEADS):
            r0 = hd * HEAD_PAD
            x1 = qt[r0 + MLA_NOPE:r0 + MLA_NOPE + ROPE_HALF]
            x2 = qt[r0 + MLA_NOPE + ROPE_HALF:r0 + MLA_QK]
            qt_ref[0, p, r0:r0 + MLA_NOPE, :] = qt[r0:r0 + MLA_NOPE].astype(bf)
            qt_ref[0, p, r0 + MLA_NOPE:r0 + MLA_NOPE + ROPE_HALF, :] = (x1 * cos_t - x2 * sin_t).astype(bf)
            qt_ref[0, p, r0 + MLA_NOPE + ROPE_HALF:r0 + MLA_QK, :] = (x2 * cos_t + x1 * sin_t).astype(bf)
            qt_ref[0, p, r0 + MLA_QK:r0 + HEAD_PAD, :] = qt[r0 + MLA_QK:r0 + HEAD_PAD].astype(bf)

        kvn_t = _rms_rows(pt[_T_KVLAT:_T_QS], gkvc_ref[...]).astype(bf)
        vt = _dot(wvt_ref[...], kvn_t).astype(bf)
        ones_row = (lax.broadcasted_iota(jnp.int32, (BF16_ROWS, part), 0) == 0).astype(bf)
        for hd in range(MLA_HEADS):
            vt_ref[0, p, hd * V_EXT:hd * V_EXT + MLA_V, :] = vt[hd * MLA_V:(hd + 1) * MLA_V]
            vt_ref[0, p, hd * V_EXT + MLA_V:(hd + 1) * V_EXT, :] = ones_row
        kvn_s = _rms_lanes(ps[:, _S_KVLAT:_S_KS], gkvr_ref[...]).astype(bf)
        kn = _dot(kvn_s, wk_ref[...])
        y1 = pt[_T_KROPE:_T_KROPE + ROPE_HALF]
        y2 = pt[_T_KROPE + ROPE_HALF:_T_ROWS]
        kpe_t = jnp.concatenate([jnp.zeros((MLA_NOPE, part), jnp.float32),
                                 y1 * cos_t - y2 * sin_t, y2 * cos_t + y1 * sin_t,
                                 jnp.zeros((HEAD_PAD - MLA_QK, part), jnp.float32)], axis=0)
        kpe = kpe_t.T
        for hd in range(MLA_HEADS):
            k_ref[0, hd, tok] = (kn[:, hd * HEAD_PAD:(hd + 1) * HEAD_PAD] + kpe).astype(bf)

        qst_ref[0, p] = (pt[_T_QS:_T_VS] * (SWA_HEAD_DIM ** -0.5 * LOG2E)).astype(bf)
        ks_ref[0, tok] = ps[:, _S_KS:_S_COLS].astype(bf)
        for hd in range(SWA_KV_HEADS):
            vst_ref[0, p, hd * SWA_V_EXT:hd * SWA_V_EXT + SWA_HEAD_DIM, :] = (
                pt[_T_VS + hd * SWA_HEAD_DIM:_T_VS + (hd + 1) * SWA_HEAD_DIM].astype(bf))
            vst_ref[0, p, hd * SWA_V_EXT + SWA_HEAD_DIM:(hd + 1) * SWA_V_EXT, :] = ones_row
        gt_ref[0, p] = pt[_T_GATE:_T_KROPE].astype(bf)

    pt, ps = project(normed(0))
    for p in range(1, PROJ_TILES):
        h_next = normed(p)
        finish(p - 1, pt, ps)
        pt, ps = project(h_next)
    finish(PROJ_TILES - 1, pt, ps)


def _mla_schedule(n_tiles):
    rows, todo = [], list(range(n_tiles))
    while todo:
        gap = -len(rows) % MLA_UNROLL
        q = min(t for t in todo if t + 1 > gap)
        todo.remove(q)
        keys = list(range(q))
        keys.insert(gap, q)
        rows += [(q, kt, int(kt == q), int(j == 0)) for j, kt in enumerate(keys)]
    assert len(rows) % MLA_UNROLL == 0 and all(r[2] == 0 for j, r in enumerate(rows) if j % MLA_UNROLL)
    pad = (MLA_STAGES - 1) * MLA_UNROLL
    rows = [rows[0]] * pad + rows + [rows[-1]] * pad
    return np.asarray(rows, np.int32).T.copy(), len(rows) // MLA_UNROLL - (MLA_STAGES - 1)


def _mla_kernel(tab_ref, qt_ref, k_ref, vt_ref, ot_ref, s_scr, p_scr, mask_scr, *, n_iters):
    bf = jnp.bfloat16
    u_n = MLA_UNROLL
    tq = SEQ_TILE

    @pl.when((pl.program_id(0) == 0) & (pl.program_id(1) == 0))
    def _():
        s_scr[...] = jnp.zeros(s_scr.shape, jnp.float32)
        p_scr[...] = jnp.zeros(p_scr.shape, bf)
        key = lax.broadcasted_iota(jnp.int32, (tq, tq), 0)
        qry = lax.broadcasted_iota(jnp.int32, (tq, tq), 1)
        mask_scr[0] = jnp.zeros((tq, tq), jnp.float32)
        mask_scr[1] = jnp.where(key <= qry, 0.0, NEG_BIG)

    def iteration(i, half, carry):
        ms, accs, m_tiles, alphas = [list(c) for c in carry]
        new_tiles = [[None] * u_n for _ in range(MLA_GROUP)]
        new_alphas = [[None] * u_n for _ in range(MLA_GROUP)]

        def value_stage(g, u):
            f = i * u_n + u
            v_tile = vt_ref[0, tab_ref[1, f], g * V_EXT:(g + 1) * V_EXT, :]
            pv = _dot(v_tile, p_scr[half, g, u, :, :tq])
            acc = jnp.where(tab_ref[3, f] == 1, pv, alphas[g][u] * accs[g] + pv)
            ot_ref[0, tab_ref[0, f], g * MLA_V:(g + 1) * MLA_V, :] = (
                acc[:MLA_V] * (1.0 / acc[MLA_V:MLA_V + 1])).astype(bf)
            accs[g] = acc

        def exp_stage(g, u):
            f = (i + 1) * u_n + u
            m_in = jnp.where(tab_ref[3, f] == 1, NEG_BIG, ms[g])
            m = jnp.maximum(m_in, m_tiles[g][u])
            new_alphas[g][u] = jnp.exp2(m_in - m)
            p_scr[1 - half, g, u, :, :tq] = jnp.exp2(s_scr[g, u, :, :tq] - m).astype(bf)
            ms[g] = m

        def score_stage(g, u):
            f = (i + 2) * u_n + u
            kj = k_ref[0, g, pl.ds(pl.multiple_of(tab_ref[1, f] * tq, tq), tq), :]
            st = _dot(kj, qt_ref[0, tab_ref[0, f], g * HEAD_PAD:(g + 1) * HEAD_PAD, :])
            if u == 0:
                st = st + mask_scr[tab_ref[2, f]]
            s_scr[g, u, :, :tq] = st
            new_tiles[g][u] = jnp.max(st, axis=0, keepdims=True)

        for u in range(u_n):
            for g in range(MLA_GROUP):
                value_stage(g, u)
                exp_stage(g, u)
                score_stage(g, u)
        return (tuple(ms), tuple(accs), tuple(tuple(t) for t in new_tiles),
                tuple(tuple(a) for a in new_alphas))

    def body(i2, carry):
        return iteration(2 * i2 + 1, 1, iteration(2 * i2, 0, carry))

    assert n_iters % 2 == 0
    row = mask_scr[0, 0:1, :]
    acc0 = jnp.zeros((V_EXT, tq), jnp.float32)
    lax.fori_loop(0, n_iters // 2, body,
                  ((row,) * MLA_GROUP, (acc0,) * MLA_GROUP,
                   ((row,) * u_n,) * MLA_GROUP, ((row,) * u_n,) * MLA_GROUP))


def _swa_logit_offsets(bias_ref, qpos, kpos, first_key):
    shape = (2 * BLOCK, BLOCK)
    n = jnp.maximum(qpos - kpos, 0)
    max_exact = REL_BUCKETS // 2
    nf = jnp.maximum(n, 1).astype(jnp.float32)
    large = max_exact + (jnp.log(nf / max_exact) / math.log(REL_MAX_DIST / max_exact)
                         * (REL_BUCKETS - max_exact)).astype(jnp.int32)
    bucket = jnp.where(n < max_exact, n, jnp.minimum(large, REL_BUCKETS - 1))
    ki = lax.broadcasted_iota(jnp.int32, shape, 0)
    qi = lax.broadcasted_iota(jnp.int32, shape, 1)
    delta = qi + BLOCK - ki
    valid = (delta >= 0) & (delta < WINDOW) & (ki >= first_key)
    tiles = []
    for hd in range(SWA_HEADS):
        table = jnp.broadcast_to(bias_ref[hd:hd + 1, :], shape)
        bias = jnp.take_along_axis(table, bucket, axis=1, mode="promise_in_bounds")
        tiles.append(jnp.where(valid, bias * LOG2E, NEG_BIG))
    return tiles


def _swa_store_offsets(off_scr, slot, tiles):
    groups = SWA_HEADS // SWA_KV_HEADS
    for hd, tile in enumerate(tiles):
        off_scr[slot, hd // groups, :, (hd % groups) * BLOCK:(hd % groups + 1) * BLOCK] = tile


_SWA_SLOT_RUN = 0
_SWA_SLOT_RUN_FIRST = 1
_SWA_SLOT_BLOCK = 2


def _swa_kernel(bias_ref, sink_ref, qst_ref, kp_ref, kc_ref, vp_ref, vc_ref,
                prow_ref, ppcol_ref, pccol_ref, ot_ref, off_scr):
    bf = jnp.bfloat16
    step = pl.program_id(1)
    per = SEQ_TILE // BLOCK
    groups = SWA_HEADS // SWA_KV_HEADS

    @pl.when((pl.program_id(0) == 0) & (step == 0))
    def _():
        run_q = lax.broadcasted_iota(jnp.int32, (1, BLOCK), 1) + BLOCK
        run_k = lax.broadcasted_iota(jnp.int32, (2 * BLOCK, 1), 0)
        for slot, first_key in ((_SWA_SLOT_RUN, 0), (_SWA_SLOT_RUN_FIRST, BLOCK)):
            _swa_store_offsets(off_scr, slot, _swa_logit_offsets(bias_ref, run_q, run_k, first_key))

    pos = prow_ref[0]
    prev = ppcol_ref[0]
    base = pos[:, 0:1]
    run_pos = pos - base == lax.broadcasted_iota(jnp.int32, pos.shape, 1)
    run_prev = prev - base + BLOCK == lax.broadcasted_iota(jnp.int32, prev.shape, 0)
    consecutive = (jnp.min(jnp.where(run_pos, 1.0, 0.0)) > 0.5) & (
        (jnp.min(jnp.where(run_prev, 1.0, 0.0)) > 0.5) | (step == 0))

    @pl.when(jnp.logical_not(consecutive))
    def _():
        cur = pccol_ref[0]
        for j in range(per):
            kpos = (jnp.concatenate([prev, cur[:BLOCK]], axis=0) if j == 0
                    else cur[(j - 1) * BLOCK:(j + 1) * BLOCK])
            first_key = jnp.where(step == 0, BLOCK, 0) if j == 0 else 0
            tiles = _swa_logit_offsets(bias_ref, pos[:, j * BLOCK:(j + 1) * BLOCK], kpos, first_key)
            _swa_store_offsets(off_scr, _SWA_SLOT_BLOCK + j, tiles)

    def scores(j, kvh):
        q0 = j * BLOCK
        if j == 0:
            slot = jnp.where(consecutive, jnp.where(step == 0, _SWA_SLOT_RUN_FIRST, _SWA_SLOT_RUN),
                             _SWA_SLOT_BLOCK)
            kband = jnp.concatenate([kp_ref[0], kc_ref[0, :BLOCK]], axis=0)
        else:
            slot = jnp.where(consecutive, _SWA_SLOT_RUN, _SWA_SLOT_BLOCK + j)
            kband = kc_ref[0, q0 - BLOCK:q0 + BLOCK]
        k_h = kband[:, kvh * SWA_HEAD_DIM:(kvh + 1) * SWA_HEAD_DIM]
        qcat = jnp.concatenate(
            [qst_ref[0, 0, (kvh * groups + g) * SWA_HEAD_DIM:(kvh * groups + g + 1) * SWA_HEAD_DIM,
                     q0:q0 + BLOCK] for g in range(groups)], axis=1)
        return _dot(k_h, qcat) + off_scr[slot, kvh]

    def attend(j, kvh, s):
        q0 = j * BLOCK
        vband = (jnp.concatenate([vp_ref[0, 0], vc_ref[0, 0, :, :BLOCK]], axis=1) if j == 0
                 else vc_ref[0, 0, :, q0 - BLOCK:q0 + BLOCK])
        v_h = vband[kvh * SWA_V_EXT:(kvh + 1) * SWA_V_EXT]
        sink = jnp.concatenate([jnp.full((1, BLOCK), sink_ref[kvh * groups + g] * LOG2E, jnp.float32)
                                for g in range(groups)], axis=1)
        m = jnp.maximum(jnp.max(s, axis=0, keepdims=True), sink)
        ov = _dot(v_h, jnp.exp2(s - m).astype(bf))
        denom = ov[SWA_HEAD_DIM:SWA_HEAD_DIM + 1] + jnp.exp2(sink - m)
        out = (ov[:SWA_HEAD_DIM] * (1.0 / denom)).astype(bf)
        for g in range(groups):
            hd = kvh * groups + g
            ot_ref[0, 0, hd * SWA_HEAD_DIM:(hd + 1) * SWA_HEAD_DIM, q0:q0 + BLOCK] = (
                out[:, g * BLOCK:(g + 1) * BLOCK])

    chains = [(j, kvh) for j in range(per) for kvh in range(SWA_KV_HEADS)]
    ahead = [scores(*chain) for chain in chains[:SWA_LOOKAHEAD]]
    for c, chain in enumerate(chains):
        if c + SWA_LOOKAHEAD < len(chains):
            ahead.append(scores(*chains[c + SWA_LOOKAHEAD]))
        attend(*chain, ahead.pop(0))


def _out_kernel(oat_ref, obt_ref, gt_ref, x_ref, wo_ref, fn_ref, o_ref):
    tm = SEQ_TILE

    def gated(t):
        h = gt_ref[0, t].astype(jnp.float32) * 0.5
        silu = (h * (1.0 + jnp.tanh(h))).astype(jnp.bfloat16)
        return jnp.concatenate([oat_ref[0, t], obt_ref[0, t]], axis=0) * silu

    def finish(t, proj):
        z = x_ref[0, t * tm:(t + 1) * tm] + proj
        o_ref[0, t * tm:(t + 1) * tm] = _rms_lanes(z, fn_ref[...])

    proj = _dot_tn(gated(0), wo_ref[...])
    for t in range(1, OUT_TILES):
        y_next = gated(t)
        finish(t - 1, proj)
        proj = _dot_tn(y_next, wo_ref[...])
    finish(OUT_TILES - 1, proj)


def _const_spec(shape):
    return pl.BlockSpec(shape, lambda *_: (0,) * len(shape))


def _params(n_axes):
    return pltpu.CompilerParams(dimension_semantics=("arbitrary",) * n_axes,
                                vmem_limit_bytes=VMEM_LIMIT)


def kernel(x, positions, norm_gain, w_in, q_a_norm, w_q_b, kv_a_norm, w_kv_b, sinks, rel_bias, w_out,
           final_norm):
    bf, f32 = jnp.bfloat16, jnp.float32
    batch, seq, _ = x.shape
    assert norm_gain.shape[0] == 1, "single-layer kernel"
    tm = SEQ_TILE
    nt = seq // tm
    nb = seq // BLOCK
    per = tm // BLOCK

    w = w_in[0]
    c = 0
    w_qlat, c = w[:, c:c + MLA_Q_LORA], c + MLA_Q_LORA
    w_kvlat, c = w[:, c:c + MLA_KV_LORA], c + MLA_KV_LORA
    w_krope, c = w[:, c:c + MLA_ROPE], c + MLA_ROPE
    w_qs, c = w[:, c:c + SWA_WIDTH], c + SWA_WIDTH
    w_ks, c = w[:, c:c + SWA_KV_WIDTH], c + SWA_KV_WIDTH
    w_vs, c = w[:, c:c + SWA_KV_WIDTH], c + SWA_KV_WIDTH
    w_gate = w[:, c:c + MIX_WIDTH]
    wt = jnp.concatenate([w_qlat, w_kvlat, w_qs, w_vs, w_gate, w_krope], axis=1).T.astype(bf)
    ws = jnp.concatenate([w_kvlat, w_ks], axis=1).astype(bf)

    wq = w_q_b[0].T.reshape(MLA_HEADS, MLA_QK, MLA_Q_LORA)
    wqt = jnp.pad(wq, ((0, 0), (0, HEAD_PAD - MLA_QK), (0, 0))).reshape(MLA_HEADS * HEAD_PAD, MLA_Q_LORA)
    wkv = w_kv_b[0].reshape(MLA_KV_LORA, MLA_HEADS, MLA_NOPE + MLA_V)
    wk = jnp.pad(wkv[:, :, :MLA_NOPE], ((0, 0), (0, 0), (0, HEAD_PAD - MLA_NOPE)))
    wk = wk.reshape(MLA_KV_LORA, MLA_HEADS * HEAD_PAD).astype(bf)
    wvt = wkv[:, :, MLA_NOPE:].reshape(MLA_KV_LORA, MLA_WIDTH).T.astype(bf)
    wqt = wqt.astype(bf)

    inv = ROPE_THETA ** (-jnp.arange(ROPE_HALF, dtype=f32) / ROPE_HALF)
    inv_col = inv.reshape(ROPE_HALF, 1)
    pos_row = positions.reshape(batch, 1, seq)
    pos_col = positions.reshape(batch, seq, 1)

    tile4 = lambda rows: pl.BlockSpec((1, 1, rows, tm), lambda b, i: (b, i, 0, 0))
    ptiles = lambda rows: pl.BlockSpec((1, PROJ_TILES, rows, tm), lambda b, i: (b, i, 0, 0))
    ptm = PROJ_TILES * tm
    qt, k, vt, qst, ks, vst, gt = pl.pallas_call(
        _proj_kernel,
        grid=(batch, nt // PROJ_TILES),
        in_specs=[
            pl.BlockSpec((1, ptm, D_MODEL), lambda b, i: (b, i, 0)),
            pl.BlockSpec((1, 1, ptm), lambda b, i: (b, 0, i)),
            _const_spec((1, D_MODEL)),
            _const_spec((_T_ROWS, D_MODEL)),
            _const_spec((D_MODEL, _S_COLS)),
            _const_spec((MLA_Q_LORA, 1)),
            _const_spec((MLA_KV_LORA, 1)),
            _const_spec((1, MLA_KV_LORA)),
            _const_spec((MLA_HEADS * HEAD_PAD, MLA_Q_LORA)),
            _const_spec((MLA_WIDTH, MLA_KV_LORA)),
            _const_spec((MLA_KV_LORA, MLA_HEADS * HEAD_PAD)),
            _const_spec((ROPE_HALF, 1)),
        ],
        out_specs=[
            ptiles(MLA_HEADS * HEAD_PAD),
            pl.BlockSpec((1, MLA_HEADS, ptm, HEAD_PAD), lambda b, i: (b, 0, i, 0)),
            ptiles(MLA_HEADS * V_EXT),
            ptiles(SWA_WIDTH),
            pl.BlockSpec((1, ptm, SWA_KV_WIDTH), lambda b, i: (b, i, 0)),
            ptiles(SWA_KV_HEADS * SWA_V_EXT),
            ptiles(MIX_WIDTH),
        ],
        out_shape=[
            jax.ShapeDtypeStruct((batch, nt, MLA_HEADS * HEAD_PAD, tm), bf),
            jax.ShapeDtypeStruct((batch, MLA_HEADS, seq, HEAD_PAD), bf),
            jax.ShapeDtypeStruct((batch, nt, MLA_HEADS * V_EXT, tm), bf),
            jax.ShapeDtypeStruct((batch, nt, SWA_WIDTH, tm), bf),
            jax.ShapeDtypeStruct((batch, seq, SWA_KV_WIDTH), bf),
            jax.ShapeDtypeStruct((batch, nt, SWA_KV_HEADS * SWA_V_EXT, tm), bf),
            jax.ShapeDtypeStruct((batch, nt, MIX_WIDTH, tm), bf),
        ],
        compiler_params=_params(2),
        name="proj",
    )(x, pos_row, norm_gain[0].reshape(1, D_MODEL), wt, ws,
      q_a_norm[0].reshape(MLA_Q_LORA, 1), kv_a_norm[0].reshape(MLA_KV_LORA, 1),
      kv_a_norm[0].reshape(1, MLA_KV_LORA), wqt, wvt, wk, inv_col)

    schedule, n_iters = _mla_schedule(nt)
    oat = pl.pallas_call(
        functools.partial(_mla_kernel, n_iters=n_iters),
        grid=(batch, MLA_HEADS // MLA_GROUP),
        in_specs=[
            pl.BlockSpec(memory_space=pltpu.SMEM),
            pl.BlockSpec((1, nt, MLA_GROUP * HEAD_PAD, tm), lambda b, h: (b, 0, h, 0)),
            pl.BlockSpec((1, MLA_GROUP, seq, HEAD_PAD), lambda b, h: (b, h, 0, 0)),
            pl.BlockSpec((1, nt, MLA_GROUP * V_EXT, tm), lambda b, h: (b, 0, h, 0)),
        ],
        out_specs=pl.BlockSpec((1, nt, MLA_GROUP * MLA_V, tm), lambda b, h: (b, 0, h, 0)),
        out_shape=jax.ShapeDtypeStruct((batch, nt, MLA_WIDTH, tm), bf),
        scratch_shapes=[pltpu.VMEM((MLA_GROUP, MLA_UNROLL, tm, tm + MLA_LANE_PAD), f32),
                        pltpu.VMEM((2, MLA_GROUP, MLA_UNROLL, tm, tm + MLA_LANE_PAD), bf),
                        pltpu.VMEM((2, tm, tm), f32)],
        compiler_params=_params(2),
        name="mla",
    )(jnp.asarray(schedule), qt, k, vt)

    prev = lambda i: jnp.maximum(i * per - 1, 0)
    bias_t = jnp.pad(rel_bias.T.astype(f32), ((0, 0), (0, LANES - REL_BUCKETS)))
    obt = pl.pallas_call(
        _swa_kernel,
        grid=(batch, nt),
        in_specs=[
            _const_spec((SWA_HEADS, LANES)),
            pl.BlockSpec(memory_space=pltpu.SMEM),
            tile4(SWA_WIDTH),
            pl.BlockSpec((1, BLOCK, SWA_KV_WIDTH), lambda b, i: (b, prev(i), 0)),
            pl.BlockSpec((1, tm, SWA_KV_WIDTH), lambda b, i: (b, i, 0)),
            pl.BlockSpec((1, 1, SWA_KV_HEADS * SWA_V_EXT, BLOCK),
                         lambda b, i: (b, prev(i) // per, 0, prev(i) % per)),
            tile4(SWA_KV_HEADS * SWA_V_EXT),
            pl.BlockSpec((1, 1, tm), lambda b, i: (b, 0, i)),
            pl.BlockSpec((1, BLOCK, 1), lambda b, i: (b, prev(i), 0)),
            pl.BlockSpec((1, tm, 1), lambda b, i: (b, i, 0)),
        ],
        out_specs=tile4(SWA_WIDTH),
        out_shape=jax.ShapeDtypeStruct((batch, nt, SWA_WIDTH, tm), bf),
        scratch_shapes=[pltpu.VMEM((_SWA_SLOT_BLOCK + per, SWA_KV_HEADS, 2 * BLOCK,
                                    (SWA_HEADS // SWA_KV_HEADS) * BLOCK), f32)],
        compiler_params=_params(2),
        name="swa",
    )(bias_t, sinks[0].astype(f32), qst, ks, ks, vst, vst, pos_row, pos_col, pos_col)

    tiles = lambda rows: pl.BlockSpec((1, OUT_TILES, rows, tm), lambda b, i: (b, i, 0, 0))
    return pl.pallas_call(
        _out_kernel,
        grid=(batch, nt // OUT_TILES),
        in_specs=[
            tiles(MLA_WIDTH),
            tiles(SWA_WIDTH),
            tiles(MIX_WIDTH),
            pl.BlockSpec((1, OUT_TILES * tm, D_MODEL), lambda b, i: (b, i, 0)),
            _const_spec((MIX_WIDTH, D_MODEL)),
            _const_spec((1, D_MODEL)),
        ],
        out_specs=pl.BlockSpec((1, OUT_TILES * tm, D_MODEL), lambda b, i: (b, i, 0)),
        out_shape=jax.ShapeDtypeStruct((batch, seq, D_MODEL), x.dtype),
        compiler_params=_params(2),
        name="out",
    )(oat, obt, gt, x, w_out[0].astype(bf), final_norm.reshape(1, D_MODEL))
```

```python
import functools
import math

import jax
import numpy as np
import jax.numpy as jnp
from jax import lax
from jax.experimental import pallas as pl
from jax.experimental.pallas import tpu as pltpu

D_MODEL = 1024
MLA_HEADS = 8
MLA_Q_LORA = 256
MLA_KV_LORA = 128
MLA_NOPE = 64
MLA_ROPE = 32
MLA_V = 64
SWA_HEADS = 8
SWA_KV_HEADS = 2
SWA_HEAD_DIM = 64
WINDOW = 128
BLOCK = 128
REL_BUCKETS = 32
REL_MAX_DIST = 128
ROPE_THETA = 10000.0
EPS = 1e-6

MLA_QK = MLA_NOPE + MLA_ROPE
MLA_WIDTH = MLA_HEADS * MLA_V
SWA_WIDTH = SWA_HEADS * SWA_HEAD_DIM
SWA_KV_WIDTH = SWA_KV_HEADS * SWA_HEAD_DIM
MIX_WIDTH = MLA_WIDTH + SWA_WIDTH
ROPE_HALF = MLA_ROPE // 2

LANES = 128
HEAD_PAD = LANES
BF16_ROWS = 16
V_EXT = MLA_V + BF16_ROWS
SWA_V_EXT = SWA_HEAD_DIM + BF16_ROWS
LOG2E = math.log2(math.e)
SEQ_TILE = 512
MLA_UNROLL = 4
MLA_STAGES = 3
MLA_GROUP = 2
MLA_LANE_PAD = LANES
PROJ_TILES = 2
OUT_TILES = 2
SWA_LOOKAHEAD = 2
NEG_BIG = -1e30
VMEM_LIMIT = 56 * 1024 * 1024

_T_QLAT = 0
_T_KVLAT = _T_QLAT + MLA_Q_LORA
_T_QS = _T_KVLAT + MLA_KV_LORA
_T_VS = _T_QS + SWA_WIDTH
_T_GATE = _T_VS + SWA_KV_WIDTH
_T_KROPE = _T_GATE + MIX_WIDTH
_T_ROWS = _T_KROPE + MLA_ROPE
_S_KVLAT = 0
_S_KS = LANES
_S_COLS = 2 * LANES


def _rms_rows(v, gain_col):
    ms = jnp.mean(v * v, axis=0, keepdims=True)
    return v * lax.rsqrt(ms + EPS) * gain_col


def _rms_lanes(v, gain_row):
    ms = jnp.mean(v * v, axis=-1, keepdims=True)
    return v * lax.rsqrt(ms + EPS) * gain_row


def _dot(a, b):
    return jnp.dot(a, b, preferred_element_type=jnp.float32)


def _dot_nt(a, b):
    return lax.dot_general(a, b, (((1,), (1,)), ((), ())), preferred_element_type=jnp.float32)


def _dot_tn(a, b):
    return lax.dot_general(a, b, (((0,), (0,)), ((), ())), preferred_element_type=jnp.float32)


def _proj_kernel(x_ref, prow_ref, g1_ref, wt_ref, ws_ref, gq_ref, gkvc_ref, gkvr_ref,
                 wqt_ref, wvt_ref, wk_ref, invc_ref,
                 qt_ref, k_ref, vt_ref, qst_ref, ks_ref, vst_ref, gt_ref):
    bf = jnp.bfloat16
    part = x_ref.shape[1] // PROJ_TILES

    def normed(p):
        return _rms_lanes(x_ref[0, p * part:(p + 1) * part], g1_ref[...]).astype(bf)

    def project(h):
        return _dot_nt(wt_ref[...], h), _dot(h, ws_ref[...])

    def finish(p, pt, ps):
        tok = slice(p * part, (p + 1) * part)
        qn = _rms_rows(pt[_T_QLAT:_T_KVLAT], gq_ref[...]).astype(bf)
        qt = _dot(wqt_ref[...], qn) * (MLA_QK ** -0.5 * LOG2E)
        ang_t = invc_ref[...] * prow_ref[0, :, tok].astype(jnp.float32)
        cos_t, sin_t = jnp.cos(ang_t), jnp.sin(ang_t)
        for hd in range(MLA_HEADS):
            r0 = hd * HEAD_PAD
            x1 = qt[r0 + MLA_NOPE:r0 + MLA_NOPE + ROPE_HALF]
            x2 = qt[r0 + MLA_NOPE + ROPE_HALF:r0 + MLA_QK]
            qt_ref[0, p, r0:r0 + MLA_NOPE, :] = qt[r0:r0 + MLA_NOPE].astype(bf)
            qt_ref[0, p, r0 + MLA_NOPE:r0 + MLA_NOPE + ROPE_HALF, :] = (x1 * cos_t - x2 * sin_t).astype(bf)
            qt_ref[0, p, r0 + MLA_NOPE + ROPE_HALF:r0 + MLA_QK, :] = (x2 * cos_t + x1 * sin_t).astype(bf)
            qt_ref[0, p, r0 + MLA_QK:r0 + HEAD_PAD, :] = qt[r0 + MLA_QK:r0 + HEAD_PAD].astype(bf)

        kvn_t = _rms_rows(pt[_T_KVLAT:_T_QS], gkvc_ref[...]).astype(bf)
        vt = _dot(wvt_ref[...], kvn_t).astype(bf)
        ones_row = (lax.broadcasted_iota(jnp.int32, (BF16_ROWS, part), 0) == 0).astype(bf)
        for hd in range(MLA_HEADS):
            vt_ref[0, p, hd * V_EXT:hd * V_EXT + MLA_V, :] = vt[hd * MLA_V:(hd + 1) * MLA_V]
            vt_ref[0, p, hd * V_EXT + MLA_V:(hd + 1) * V_EXT, :] = ones_row
        kvn_s = _rms_lanes(ps[:, _S_KVLAT:_S_KS], gkvr_ref[...]).astype(bf)
        kn = _dot(kvn_s, wk_ref[...])
        y1 = pt[_T_KROPE:_T_KROPE + ROPE_HALF]
        y2 = pt[_T_KROPE + ROPE_HALF:_T_ROWS]
        kpe_t = jnp.concatenate([jnp.zeros((MLA_NOPE, part), jnp.float32),
                                 y1 * cos_t - y2 * sin_t, y2 * cos_t + y1 * sin_t,
                                 jnp.zeros((HEAD_PAD - MLA_QK, part), jnp.float32)], axis=0)
        kpe = kpe_t.T
        for hd in range(MLA_HEADS):
            k_ref[0, hd, tok] = (kn[:, hd * HEAD_PAD:(hd + 1) * HEAD_PAD] + kpe).astype(bf)

        qst_ref[0, p] = (pt[_T_QS:_T_VS] * (SWA_HEAD_DIM ** -0.5 * LOG2E)).astype(bf)
        ks_ref[0, tok] = ps[:, _S_KS:_S_COLS].astype(bf)
        for hd in range(SWA_KV_HEADS):
            vst_ref[0, p, hd * SWA_V_EXT:hd * SWA_V_EXT + SWA_HEAD_DIM, :] = (
                pt[_T_VS + hd * SWA_HEAD_DIM:_T_VS + (hd + 1) * SWA_HEAD_DIM].astype(bf))
            vst_ref[0, p, hd * SWA_V_EXT + SWA_HEAD_DIM:(hd + 1) * SWA_V_EXT, :] = ones_row
        gt_ref[0, p] = pt[_T_GATE:_T_KROPE].astype(bf)

    pt, ps = project(normed(0))
    for p in range(1, PROJ_TILES):
        h_next = normed(p)
        finish(p - 1, pt, ps)
        pt, ps = project(h_next)
    finish(PROJ_TILES - 1, pt, ps)


def _mla_schedule(n_tiles):
    rows, todo = [], list(range(n_tiles))
    while todo:
        gap = -len(rows) % MLA_UNROLL
        q = min(t for t in todo if t + 1 > gap)
        todo.remove(q)
        keys = list(range(q))
        keys.insert(gap, q)
        rows += [(q, kt, int(kt == q), int(j == 0)) for j, kt in enumerate(keys)]
    assert len(rows) % MLA_UNROLL == 0 and all(r[2] == 0 for j, r in enumerate(rows) if j % MLA_UNROLL)
    pad = (MLA_STAGES - 1) * MLA_UNROLL
    n_iters = len(rows) // MLA_UNROLL + (MLA_STAGES - 1)
    extra = (n_iters % 2) * MLA_UNROLL
    rows = [rows[0]] * (pad + extra) + rows + [rows[-1]] * pad
    return np.asarray(rows, np.int32).T.copy(), n_iters + n_iters % 2


def _mla_kernel(tab_ref, qt_ref, k_ref, vt_ref, ot_ref, s_scr, p_scr, mask_scr, *, n_iters):
    bf = jnp.bfloat16
    u_n = MLA_UNROLL
    tq = SEQ_TILE

    @pl.when((pl.program_id(0) == 0) & (pl.program_id(1) == 0))
    def _():
        s_scr[...] = jnp.zeros(s_scr.shape, jnp.float32)
        p_scr[...] = jnp.zeros(p_scr.shape, bf)
        key = lax.broadcasted_iota(jnp.int32, (tq, tq), 0)
        qry = lax.broadcasted_iota(jnp.int32, (tq, tq), 1)
        mask_scr[0] = jnp.zeros((tq, tq), jnp.float32)
        mask_scr[1] = jnp.where(key <= qry, 0.0, NEG_BIG)

    def iteration(i, half, carry):
        ms, accs, m_tiles, alphas = [list(c) for c in carry]
        new_tiles = [[None] * u_n for _ in range(MLA_GROUP)]
        new_alphas = [[None] * u_n for _ in range(MLA_GROUP)]

        def value_stage(g, u):
            f = i * u_n + u
            v_tile = vt_ref[0, tab_ref[1, f], g * V_EXT:(g + 1) * V_EXT, :]
            pv = _dot(v_tile, p_scr[half, g, u, :, :tq])
            acc = jnp.where(tab_ref[3, f] == 1, pv, alphas[g][u] * accs[g] + pv)
            ot_ref[0, tab_ref[0, f], g * MLA_V:(g + 1) * MLA_V, :] = (
                acc[:MLA_V] * (1.0 / acc[MLA_V:MLA_V + 1])).astype(bf)
            accs[g] = acc

        def exp_stage(g, u):
            f = (i + 1) * u_n + u
            m_in = jnp.where(tab_ref[3, f] == 1, NEG_BIG, ms[g])
            m = jnp.maximum(m_in, m_tiles[g][u])
            new_alphas[g][u] = jnp.exp2(m_in - m)
            p_scr[1 - half, g, u, :, :tq] = jnp.exp2(s_scr[g, u, :, :tq] - m).astype(bf)
            ms[g] = m

        def score_stage(g, u):
            f = (i + 2) * u_n + u
            kj = k_ref[0, g, pl.ds(pl.multiple_of(tab_ref[1, f] * tq, tq), tq), :]
            st = _dot(kj, qt_ref[0, tab_ref[0, f], g * HEAD_PAD:(g + 1) * HEAD_PAD, :])
            if u == 0:
                st = st + mask_scr[tab_ref[2, f]]
            s_scr[g, u, :, :tq] = st
            new_tiles[g][u] = jnp.max(st, axis=0, keepdims=True)

        for u in range(u_n):
            for g in range(MLA_GROUP):
                value_stage(g, u)
                exp_stage(g, u)
                score_stage(g, u)
        return (tuple(ms), tuple(accs), tuple(tuple(t) for t in new_tiles),
                tuple(tuple(a) for a in new_alphas))

    def body(i2, carry):
        return iteration(2 * i2 + 1, 1, iteration(2 * i2, 0, carry))

    assert n_iters % 2 == 0
    row = mask_scr[0, 0:1, :]
    acc0 = jnp.zeros((V_EXT, tq), jnp.float32)
    lax.fori_loop(0, n_iters // 2, body,
                  ((row,) * MLA_GROUP, (acc0,) * MLA_GROUP,
                   ((row,) * u_n,) * MLA_GROUP, ((row,) * u_n,) * MLA_GROUP))


def _swa_logit_offsets(bias_ref, qpos, kpos, first_key):
    shape = (2 * BLOCK, BLOCK)
    n = jnp.maximum(qpos - kpos, 0)
    max_exact = REL_BUCKETS // 2
    nf = jnp.maximum(n, 1).astype(jnp.float32)
    large = max_exact + (jnp.log(nf / max_exact) / math.log(REL_MAX_DIST / max_exact)
                         * (REL_BUCKETS - max_exact)).astype(jnp.int32)
    bucket = jnp.where(n < max_exact, n, jnp.minimum(large, REL_BUCKETS - 1))
    ki = lax.broadcasted_iota(jnp.int32, shape, 0)
    qi = lax.broadcasted_iota(jnp.int32, shape, 1)
    delta = qi + BLOCK - ki
    valid = (delta >= 0) & (delta < WINDOW) & (ki >= first_key)
    tiles = []
    for hd in range(SWA_HEADS):
        table = jnp.broadcast_to(bias_ref[hd:hd + 1, :], shape)
        bias = jnp.take_along_axis(table, bucket, axis=1, mode="promise_in_bounds")
        tiles.append(jnp.where(valid, bias * LOG2E, NEG_BIG))
    return tiles


def _swa_store_offsets(off_scr, slot, tiles):
    groups = SWA_HEADS // SWA_KV_HEADS
    for hd, tile in enumerate(tiles):
        off_scr[slot, hd // groups, :, (hd % groups) * BLOCK:(hd % groups + 1) * BLOCK] = tile


_SWA_SLOT_RUN = 0
_SWA_SLOT_RUN_FIRST = 1
_SWA_SLOT_BLOCK = 2


def _swa_kernel(bias_ref, sink_ref, qst_ref, kp_ref, kc_ref, vp_ref, vc_ref,
                prow_ref, pprev_ref, ot_ref, off_scr):
    bf = jnp.bfloat16
    step = pl.program_id(1)
    per = SEQ_TILE // BLOCK
    groups = SWA_HEADS // SWA_KV_HEADS

    @pl.when((pl.program_id(0) == 0) & (step == 0))
    def _():
        run_q = lax.broadcasted_iota(jnp.int32, (1, BLOCK), 1) + BLOCK
        run_k = lax.broadcasted_iota(jnp.int32, (2 * BLOCK, 1), 0)
        for slot, first_key in ((_SWA_SLOT_RUN, 0), (_SWA_SLOT_RUN_FIRST, BLOCK)):
            _swa_store_offsets(off_scr, slot, _swa_logit_offsets(bias_ref, run_q, run_k, first_key))

    pos = prow_ref[0]
    prev = pprev_ref[0]
    base = pos[:, 0:1]
    run_pos = pos - base == lax.broadcasted_iota(jnp.int32, pos.shape, 1)
    run_prev = prev - base + BLOCK == lax.broadcasted_iota(jnp.int32, prev.shape, 1)
    consecutive = (jnp.min(jnp.where(run_pos, 1.0, 0.0)) > 0.5) & (
        (jnp.min(jnp.where(run_prev, 1.0, 0.0)) > 0.5) | (step == 0))

    @pl.when(jnp.logical_not(consecutive))
    def _():
        band = jnp.concatenate([prev, pos], axis=1)
        for j in range(per):
            keys = band[:, j * BLOCK:(j + 2) * BLOCK]
            kpos = jnp.broadcast_to(keys, (2 * BLOCK, 2 * BLOCK)).T[:, 0:1]
            first_key = jnp.where(step == 0, BLOCK, 0) if j == 0 else 0
            tiles = _swa_logit_offsets(bias_ref, pos[:, j * BLOCK:(j + 1) * BLOCK], kpos, first_key)
            _swa_store_offsets(off_scr, _SWA_SLOT_BLOCK + j, tiles)

    def scores(j, kvh):
        q0 = j * BLOCK
        if j == 0:
            slot = jnp.where(consecutive, jnp.where(step == 0, _SWA_SLOT_RUN_FIRST, _SWA_SLOT_RUN),
                             _SWA_SLOT_BLOCK)
            kband = jnp.concatenate([kp_ref[0], kc_ref[0, :BLOCK]], axis=0)
        else:
            slot = jnp.where(consecutive, _SWA_SLOT_RUN, _SWA_SLOT_BLOCK + j)
            kband = kc_ref[0, q0 - BLOCK:q0 + BLOCK]
        k_h = kband[:, kvh * SWA_HEAD_DIM:(kvh + 1) * SWA_HEAD_DIM]
        qcat = jnp.concatenate(
            [qst_ref[0, 0, (kvh * groups + g) * SWA_HEAD_DIM:(kvh * groups + g + 1) * SWA_HEAD_DIM,
                     q0:q0 + BLOCK] for g in range(groups)], axis=1)
        return _dot(k_h, qcat) + off_scr[slot, kvh]

    def attend(j, kvh, s):
        q0 = j * BLOCK
        vband = (jnp.concatenate([vp_ref[0, 0], vc_ref[0, 0, :, :BLOCK]], axis=1) if j == 0
                 else vc_ref[0, 0, :, q0 - BLOCK:q0 + BLOCK])
        v_h = vband[kvh * SWA_V_EXT:(kvh + 1) * SWA_V_EXT]
        sink = jnp.concatenate([jnp.full((1, BLOCK), sink_ref[kvh * groups + g] * LOG2E, jnp.float32)
                                for g in range(groups)], axis=1)
        m = jnp.maximum(jnp.max(s, axis=0, keepdims=True), sink)
        ov = _dot(v_h, jnp.exp2(s - m).astype(bf))
        denom = ov[SWA_HEAD_DIM:SWA_HEAD_DIM + 1] + jnp.exp2(sink - m)
        out = (ov[:SWA_HEAD_DIM] * (1.0 / denom)).astype(bf)
        for g in range(groups):
            hd = kvh * groups + g
            ot_ref[0, 0, hd * SWA_HEAD_DIM:(hd + 1) * SWA_HEAD_DIM, q0:q0 + BLOCK] = (
                out[:, g * BLOCK:(g + 1) * BLOCK])

    chains = [(j, kvh) for j in range(per) for kvh in range(SWA_KV_HEADS)]
    ahead = [scores(*chain) for chain in chains[:SWA_LOOKAHEAD]]
    for c, chain in enumerate(chains):
        if c + SWA_LOOKAHEAD < len(chains):
            ahead.append(scores(*chains[c + SWA_LOOKAHEAD]))
        attend(*chain, ahead.pop(0))


def _out_kernel(oat_ref, obt_ref, gt_ref, x_ref, wo_ref, fn_ref, o_ref):
    tm = SEQ_TILE

    def gated(t):
        h = gt_ref[0, t].astype(jnp.float32) * 0.5
        silu = (h * (1.0 + jnp.tanh(h))).astype(jnp.bfloat16)
        return jnp.concatenate([oat_ref[0, t], obt_ref[0, t]], axis=0) * silu

    def finish(t, proj):
        z = x_ref[0, t * tm:(t + 1) * tm] + proj
        o_ref[0, t * tm:(t + 1) * tm] = _rms_lanes(z, fn_ref[...])

    proj = _dot_tn(gated(0), wo_ref[...])
    for t in range(1, OUT_TILES):
        y_next = gated(t)
        finish(t - 1, proj)
        proj = _dot_tn(y_next, wo_ref[...])
    finish(OUT_TILES - 1, proj)


def _const_spec(shape):
    return pl.BlockSpec(shape, lambda *_: (0,) * len(shape))


def _params(n_axes):
    return pltpu.CompilerParams(dimension_semantics=("arbitrary",) * n_axes,
                                vmem_limit_bytes=VMEM_LIMIT)


def kernel(x, positions, norm_gain, w_in, q_a_norm, w_q_b, kv_a_norm, w_kv_b, sinks, rel_bias, w_out,
           final_norm):
    bf, f32 = jnp.bfloat16, jnp.float32
    batch, seq, _ = x.shape
    assert norm_gain.shape[0] == 1, "single-layer kernel"
    tm = SEQ_TILE
    nt = seq // tm
    per = tm // BLOCK

    w = w_in[0]
    c = 0
    w_qlat, c = w[:, c:c + MLA_Q_LORA], c + MLA_Q_LORA
    w_kvlat, c = w[:, c:c + MLA_KV_LORA], c + MLA_KV_LORA
    w_krope, c = w[:, c:c + MLA_ROPE], c + MLA_ROPE
    w_qs, c = w[:, c:c + SWA_WIDTH], c + SWA_WIDTH
    w_ks, c = w[:, c:c + SWA_KV_WIDTH], c + SWA_KV_WIDTH
    w_vs, c = w[:, c:c + SWA_KV_WIDTH], c + SWA_KV_WIDTH
    w_gate = w[:, c:c + MIX_WIDTH]
    wt = jnp.concatenate([w_qlat, w_kvlat, w_qs, w_vs, w_gate, w_krope], axis=1).T.astype(bf)
    ws = jnp.concatenate([w_kvlat, w_ks], axis=1).astype(bf)

    wq = w_q_b[0].T.reshape(MLA_HEADS, MLA_QK, MLA_Q_LORA)
    wqt = jnp.pad(wq, ((0, 0), (0, HEAD_PAD - MLA_QK), (0, 0))).reshape(MLA_HEADS * HEAD_PAD, MLA_Q_LORA)
    wkv = w_kv_b[0].reshape(MLA_KV_LORA, MLA_HEADS, MLA_NOPE + MLA_V)
    wk = jnp.pad(wkv[:, :, :MLA_NOPE], ((0, 0), (0, 0), (0, HEAD_PAD - MLA_NOPE)))
    wk = wk.reshape(MLA_KV_LORA, MLA_HEADS * HEAD_PAD).astype(bf)
    wvt = wkv[:, :, MLA_NOPE:].reshape(MLA_KV_LORA, MLA_WIDTH).T.astype(bf)
    wqt = wqt.astype(bf)

    inv = ROPE_THETA ** (-jnp.arange(ROPE_HALF, dtype=f32) / ROPE_HALF)
    inv_col = inv.reshape(ROPE_HALF, 1)
    pos_row = positions.reshape(batch, 1, seq)

    tile4 = lambda rows: pl.BlockSpec((1, 1, rows, tm), lambda b, i: (b, i, 0, 0))
    ptiles = lambda rows: pl.BlockSpec((1, PROJ_TILES, rows, tm), lambda b, i: (b, i, 0, 0))
    ptm = PROJ_TILES * tm
    qt, k, vt, qst, ks, vst, gt = pl.pallas_call(
        _proj_kernel,
        grid=(batch, nt // PROJ_TILES),
        in_specs=[
            pl.BlockSpec((1, ptm, D_MODEL), lambda b, i: (b, i, 0)),
            pl.BlockSpec((1, 1, ptm), lambda b, i: (b, 0, i)),
            _const_spec((1, D_MODEL)),
            _const_spec((_T_ROWS, D_MODEL)),
            _const_spec((D_MODEL, _S_COLS)),
            _const_spec((MLA_Q_LORA, 1)),
            _const_spec((MLA_KV_LORA, 1)),
            _const_spec((1, MLA_KV_LORA)),
            _const_spec((MLA_HEADS * HEAD_PAD, MLA_Q_LORA)),
            _const_spec((MLA_WIDTH, MLA_KV_LORA)),
            _const_spec((MLA_KV_LORA, MLA_HEADS * HEAD_PAD)),
            _const_spec((ROPE_HALF, 1)),
        ],
        out_specs=[
            ptiles(MLA_HEADS * HEAD_PAD),
            pl.BlockSpec((1, MLA_HEADS, ptm, HEAD_PAD), lambda b, i: (b, 0, i, 0)),
            ptiles(MLA_HEADS * V_EXT),
            ptiles(SWA_WIDTH),
            pl.BlockSpec((1, ptm, SWA_KV_WIDTH), lambda b, i: (b, i, 0)),
            ptiles(SWA_KV_HEADS * SWA_V_EXT),
            ptiles(MIX_WIDTH),
        ],
        out_shape=[
            jax.ShapeDtypeStruct((batch, nt, MLA_HEADS * HEAD_PAD, tm), bf),
            jax.ShapeDtypeStruct((batch, MLA_HEADS, seq, HEAD_PAD), bf),
            jax.ShapeDtypeStruct((batch, nt, MLA_HEADS * V_EXT, tm), bf),
            jax.ShapeDtypeStruct((batch, nt, SWA_WIDTH, tm), bf),
            jax.ShapeDtypeStruct((batch, seq, SWA_KV_WIDTH), bf),
            jax.ShapeDtypeStruct((batch, nt, SWA_KV_HEADS * SWA_V_EXT, tm), bf),
            jax.ShapeDtypeStruct((batch, nt, MIX_WIDTH, tm), bf),
        ],
        compiler_params=_params(2),
        name="proj",
    )(x, pos_row, norm_gain[0].reshape(1, D_MODEL), wt, ws,
      q_a_norm[0].reshape(MLA_Q_LORA, 1), kv_a_norm[0].reshape(MLA_KV_LORA, 1),
      kv_a_norm[0].reshape(1, MLA_KV_LORA), wqt, wvt, wk, inv_col)

    schedule, n_iters = _mla_schedule(nt)
    oat = pl.pallas_call(
        functools.partial(_mla_kernel, n_iters=n_iters),
        grid=(batch, MLA_HEADS // MLA_GROUP),
        in_specs=[
            pl.BlockSpec(memory_space=pltpu.SMEM),
            pl.BlockSpec((1, nt, MLA_GROUP * HEAD_PAD, tm), lambda b, h: (b, 0, h, 0)),
            pl.BlockSpec((1, MLA_GROUP, seq, HEAD_PAD), lambda b, h: (b, h, 0, 0)),
            pl.BlockSpec((1, nt, MLA_GROUP * V_EXT, tm), lambda b, h: (b, 0, h, 0)),
        ],
        out_specs=pl.BlockSpec((1, nt, MLA_GROUP * MLA_V, tm), lambda b, h: (b, 0, h, 0)),
        out_shape=jax.ShapeDtypeStruct((batch, nt, MLA_WIDTH, tm), bf),
        scratch_shapes=[pltpu.VMEM((MLA_GROUP, MLA_UNROLL, tm, tm + MLA_LANE_PAD), f32),
                        pltpu.VMEM((2, MLA_GROUP, MLA_UNROLL, tm, tm + MLA_LANE_PAD), bf),
                        pltpu.VMEM((2, tm, tm), f32)],
        compiler_params=_params(2),
        name="mla",
    )(jnp.asarray(schedule), qt, k, vt)

    prev = lambda i: jnp.maximum(i * per - 1, 0)
    bias_t = jnp.pad(rel_bias.T.astype(f32), ((0, 0), (0, LANES - REL_BUCKETS)))
    obt = pl.pallas_call(
        _swa_kernel,
        grid=(batch, nt),
        in_specs=[
            _const_spec((SWA_HEADS, LANES)),
            pl.BlockSpec(memory_space=pltpu.SMEM),
            tile4(SWA_WIDTH),
            pl.BlockSpec((1, BLOCK, SWA_KV_WIDTH), lambda b, i: (b, prev(i), 0)),
            pl.BlockSpec((1, tm, SWA_KV_WIDTH), lambda b, i: (b, i, 0)),
            pl.BlockSpec((1, 1, SWA_KV_HEADS * SWA_V_EXT, BLOCK),
                         lambda b, i: (b, prev(i) // per, 0, prev(i) % per)),
            tile4(SWA_KV_HEADS * SWA_V_EXT),
            pl.BlockSpec((1, 1, tm), lambda b, i: (b, 0, i)),
            pl.BlockSpec((1, 1, BLOCK), lambda b, i: (b, 0, prev(i))),
        ],
        out_specs=tile4(SWA_WIDTH),
        out_shape=jax.ShapeDtypeStruct((batch, nt, SWA_WIDTH, tm), bf),
        scratch_shapes=[pltpu.VMEM((_SWA_SLOT_BLOCK + per, SWA_KV_HEADS, 2 * BLOCK,
                                    (SWA_HEADS // SWA_KV_HEADS) * BLOCK), f32)],
        compiler_params=_params(2),
        name="swa",
    )(bias_t, sinks[0].astype(f32), qst, ks, ks, vst, vst, pos_row, pos_row)

    tiles = lambda rows: pl.BlockSpec((1, OUT_TILES, rows, tm), lambda b, i: (b, i, 0, 0))
    return pl.pallas_call(
        _out_kernel,
        grid=(batch, nt // OUT_TILES),
        in_specs=[
            tiles(MLA_WIDTH),
            tiles(SWA_WIDTH),
            tiles(MIX_WIDTH),
            pl.BlockSpec((1, OUT_TILES * tm, D_MODEL), lambda b, i: (b, i, 0)),
            _const_spec((MIX_WIDTH, D_MODEL)),
            _const_spec((1, D_MODEL)),
        ],
        out_specs=pl.BlockSpec((1, OUT_TILES * tm, D_MODEL), lambda b, i: (b, i, 0)),
        out_shape=jax.ShapeDtypeStruct((batch, seq, D_MODEL), x.dtype),
        compiler_params=_params(2),
        name="out",
    )(oat, obt, gt, x, w_out[0].astype(bf), final_norm.reshape(1, D_MODEL))
```

```python
import functools
import math

import jax
import numpy as np
import jax.numpy as jnp
from jax import lax
from jax.experimental import pallas as pl
from jax.experimental.pallas import tpu as pltpu

D_MODEL = 1024
MLA_HEADS = 8
MLA_Q_LORA = 256
MLA_KV_LORA = 128
MLA_NOPE = 64
MLA_ROPE = 32
MLA_V = 64
SWA_HEADS = 8
SWA_KV_HEADS = 2
SWA_HEAD_DIM = 64
WINDOW = 128
BLOCK = 128
REL_BUCKETS = 32
REL_MAX_DIST = 128
ROPE_THETA = 10000.0
EPS = 1e-6

MLA_QK = MLA_NOPE + MLA_ROPE
MLA_WIDTH = MLA_HEADS * MLA_V
SWA_WIDTH = SWA_HEADS * SWA_HEAD_DIM
SWA_KV_WIDTH = SWA_KV_HEADS * SWA_HEAD_DIM
MIX_WIDTH = MLA_WIDTH + SWA_WIDTH
ROPE_HALF = MLA_ROPE // 2

LANES = 128
HEAD_PAD = LANES
BF16_ROWS = 16
V_EXT = MLA_V + BF16_ROWS
SWA_V_EXT = SWA_HEAD_DIM + BF16_ROWS
LOG2E = math.log2(math.e)
SEQ_TILE = 512
MLA_UNROLL = 4
MLA_STAGES = 3
MLA_GROUP = 2
MLA_LANE_PAD = LANES
PROJ_TILES = 2
OUT_TILES = 2
SWA_LOOKAHEAD = 2
NEG_BIG = -1e30
VMEM_LIMIT = 56 * 1024 * 1024

_T_QLAT = 0
_T_KVLAT = _T_QLAT + MLA_Q_LORA
_T_QS = _T_KVLAT + MLA_KV_LORA
_T_VS = _T_QS + SWA_WIDTH
_T_GATE = _T_VS + SWA_KV_WIDTH
_T_KROPE = _T_GATE + MIX_WIDTH
_T_ROWS = _T_KROPE + MLA_ROPE
_S_KVLAT = 0
_S_KS = LANES
_S_COLS = 2 * LANES


def _rms_rows(v, gain_col):
    ms = jnp.mean(v * v, axis=0, keepdims=True)
    return v * lax.rsqrt(ms + EPS) * gain_col


def _rms_lanes(v, gain_row):
    ms = jnp.mean(v * v, axis=-1, keepdims=True)
    return v * lax.rsqrt(ms + EPS) * gain_row


def _dot(a, b):
    return jnp.dot(a, b, preferred_element_type=jnp.float32)


def _dot_nt(a, b):
    return lax.dot_general(a, b, (((1,), (1,)), ((), ())), preferred_element_type=jnp.float32)


def _dot_tn(a, b):
    return lax.dot_general(a, b, (((0,), (0,)), ((), ())), preferred_element_type=jnp.float32)


def _proj_kernel(x_ref, prow_ref, g1_ref, wt_ref, ws_ref, gq_ref, gkvc_ref, gkvr_ref,
                 wqt_ref, wvt_ref, wk_ref, invc_ref,
                 qt_ref, k_ref, vt_ref, qst_ref, ks_ref, vst_ref, gt_ref):
    bf = jnp.bfloat16
    part = x_ref.shape[1] // PROJ_TILES

    def normed(p):
        return _rms_lanes(x_ref[0, p * part:(p + 1) * part], g1_ref[...]).astype(bf)

    def project(h):
        return _dot_nt(wt_ref[...], h), _dot(h, ws_ref[...])

    def finish(p, pt, ps):
        tok = slice(p * part, (p + 1) * part)
        qn = _rms_rows(pt[_T_QLAT:_T_KVLAT], gq_ref[...]).astype(bf)
        qt = _dot(wqt_ref[...], qn) * (MLA_QK ** -0.5 * LOG2E)
        ang_t = invc_ref[...] * prow_ref[0, :, tok].astype(jnp.float32)
        cos_t, sin_t = jnp.cos(ang_t), jnp.sin(ang_t)
        for hd in range(MLA_HEADS):
            r0 = hd * HEAD_PAD
            x1 = qt[r0 + MLA_NOPE:r0 + MLA_NOPE + ROPE_HALF]
            x2 = qt[r0 + MLA_NOPE + ROPE_HALF:r0 + MLA_QK]
            qt_ref[0, p, r0:r0 + MLA_NOPE, :] = qt[r0:r0 + MLA_NOPE].astype(bf)
            qt_ref[0, p, r0 + MLA_NOPE:r0 + MLA_NOPE + ROPE_HALF, :] = (x1 * cos_t - x2 * sin_t).astype(bf)
            qt_ref[0, p, r0 + MLA_NOPE + ROPE_HALF:r0 + MLA_QK, :] = (x2 * cos_t + x1 * sin_t).astype(bf)
            qt_ref[0, p, r0 + MLA_QK:r0 + HEAD_PAD, :] = qt[r0 + MLA_QK:r0 + HEAD_PAD].astype(bf)

        kvn_t = _rms_rows(pt[_T_KVLAT:_T_QS], gkvc_ref[...]).astype(bf)
        vt = _dot(wvt_ref[...], kvn_t).astype(bf)
        ones_row = (lax.broadcasted_iota(jnp.int32, (BF16_ROWS, part), 0) == 0).astype(bf)
        for hd in range(MLA_HEADS):
            vt_ref[0, p, hd * V_EXT:hd * V_EXT + MLA_V, :] = vt[hd * MLA_V:(hd + 1) * MLA_V]
            vt_ref[0, p, hd * V_EXT + MLA_V:(hd + 1) * V_EXT, :] = ones_row
        kvn_s = _rms_lanes(ps[:, _S_KVLAT:_S_KS], gkvr_ref[...]).astype(bf)
        kn = _dot(kvn_s, wk_ref[...])
        y1 = pt[_T_KROPE:_T_KROPE + ROPE_HALF]
        y2 = pt[_T_KROPE + ROPE_HALF:_T_ROWS]
        kpe_t = jnp.concatenate([jnp.zeros((MLA_NOPE, part), jnp.float32),
                                 y1 * cos_t - y2 * sin_t, y2 * cos_t + y1 * sin_t,
                                 jnp.zeros((HEAD_PAD - MLA_QK, part), jnp.float32)], axis=0)
        kpe = kpe_t.T
        for hd in range(MLA_HEADS):
            k_ref[0, hd, tok] = (kn[:, hd * HEAD_PAD:(hd + 1) * HEAD_PAD] + kpe).astype(bf)

        qst_ref[0, p] = (pt[_T_QS:_T_VS] * (SWA_HEAD_DIM ** -0.5 * LOG2E)).astype(bf)
        ks_ref[0, tok] = ps[:, _S_KS:_S_COLS].astype(bf)
        for hd in range(SWA_KV_HEADS):
            vst_ref[0, p, hd * SWA_V_EXT:hd * SWA_V_EXT + SWA_HEAD_DIM, :] = (
                pt[_T_VS + hd * SWA_HEAD_DIM:_T_VS + (hd + 1) * SWA_HEAD_DIM].astype(bf))
            vst_ref[0, p, hd * SWA_V_EXT + SWA_HEAD_DIM:(hd + 1) * SWA_V_EXT, :] = ones_row
        gt_ref[0, p] = pt[_T_GATE:_T_KROPE].astype(bf)

    pt, ps = project(normed(0))
    for p in range(1, PROJ_TILES):
        h_next = normed(p)
        finish(p - 1, pt, ps)
        pt, ps = project(h_next)
    finish(PROJ_TILES - 1, pt, ps)


def _mla_schedule(n_tiles):
    rows, todo = [], list(range(n_tiles))
    while todo:
        gap = -len(rows) % MLA_UNROLL
        q = min(t for t in todo if t + 1 > gap)
        todo.remove(q)
        keys = list(range(q))
        keys.insert(gap, q)
        rows += [(q, kt, int(kt == q), int(j == 0)) for j, kt in enumerate(keys)]
    assert len(rows) % MLA_UNROLL == 0 and all(r[2] == 0 for j, r in enumerate(rows) if j % MLA_UNROLL)
    pad = (MLA_STAGES - 1) * MLA_UNROLL
    n_iters = len(rows) // MLA_UNROLL + (MLA_STAGES - 1)
    extra = (n_iters % 2) * MLA_UNROLL
    rows = [rows[0]] * (pad + extra) + rows + [rows[-1]] * pad
    return np.asarray(rows, np.int32).T.copy(), n_iters + n_iters % 2


def _mla_kernel(tab_ref, qt_ref, k_ref, vt_ref, ot_ref, s_scr, p_scr, mask_scr, *, n_iters):
    bf = jnp.bfloat16
    u_n = MLA_UNROLL
    tq = SEQ_TILE

    @pl.when((pl.program_id(0) == 0) & (pl.program_id(1) == 0))
    def _():
        key = lax.broadcasted_iota(jnp.int32, (tq, tq), 0)
        qry = lax.broadcasted_iota(jnp.int32, (tq, tq), 1)
        mask_scr[0] = jnp.zeros((tq, tq), jnp.float32)
        mask_scr[1] = jnp.where(key <= qry, 0.0, NEG_BIG)

    def iteration(i, half, carry, stages="CBA"):
        ms, accs, m_tiles, alphas = [list(c) for c in carry]
        new_tiles = [list(t) for t in m_tiles]
        new_alphas = [list(a) for a in alphas]

        def value_stage(g, u):
            f = i * u_n + u
            v_tile = vt_ref[0, tab_ref[1, f], g * V_EXT:(g + 1) * V_EXT, :]
            pv = _dot(v_tile, p_scr[half, g, u, :, :tq])
            acc = jnp.where(tab_ref[3, f] == 1, pv, alphas[g][u] * accs[g] + pv)
            ot_ref[0, tab_ref[0, f], g * MLA_V:(g + 1) * MLA_V, :] = (
                acc[:MLA_V] * (1.0 / acc[MLA_V:MLA_V + 1])).astype(bf)
            accs[g] = acc

        def exp_stage(g, u):
            f = (i + 1) * u_n + u
            m_in = jnp.where(tab_ref[3, f] == 1, NEG_BIG, ms[g])
            m = jnp.maximum(m_in, m_tiles[g][u])
            new_alphas[g][u] = jnp.exp2(m_in - m)
            p_scr[1 - half, g, u, :, :tq] = jnp.exp2(s_scr[g, u, :, :tq] - m).astype(bf)
            ms[g] = m

        def score_stage(g, u):
            f = (i + 2) * u_n + u
            kj = k_ref[0, g, pl.ds(pl.multiple_of(tab_ref[1, f] * tq, tq), tq), :]
            st = _dot(kj, qt_ref[0, tab_ref[0, f], g * HEAD_PAD:(g + 1) * HEAD_PAD, :])
            if u == 0:
                st = st + mask_scr[tab_ref[2, f]]
            s_scr[g, u, :, :tq] = st
            new_tiles[g][u] = jnp.max(st, axis=0, keepdims=True)

        for u in range(u_n):
            for g in range(MLA_GROUP):
                if "C" in stages:
                    value_stage(g, u)
                if "B" in stages:
                    exp_stage(g, u)
                if "A" in stages:
                    score_stage(g, u)
        return (tuple(ms), tuple(accs), tuple(tuple(t) for t in new_tiles),
                tuple(tuple(a) for a in new_alphas))

    def body(i2, carry):
        return iteration(2 * i2 + 1, 1, iteration(2 * i2, 0, carry))

    assert n_iters % 2 == 0 and n_iters >= 4
    row = mask_scr[0, 0:1, :]
    acc0 = jnp.zeros((V_EXT, tq), jnp.float32)
    carry = ((row,) * MLA_GROUP, (acc0,) * MLA_GROUP,
             ((row,) * u_n,) * MLA_GROUP, ((row,) * u_n,) * MLA_GROUP)
    carry = iteration(1, 1, iteration(0, 0, carry, "A"), "BA")
    carry = lax.fori_loop(1, n_iters // 2 - 1, body, carry)
    iteration(n_iters - 1, 1, iteration(n_iters - 2, 0, carry, "CB"), "C")


def _swa_logit_offsets(bias_ref, qpos, kpos, first_key):
    shape = (2 * BLOCK, BLOCK)
    n = jnp.maximum(qpos - kpos, 0)
    max_exact = REL_BUCKETS // 2
    nf = jnp.maximum(n, 1).astype(jnp.float32)
    large = max_exact + (jnp.log(nf / max_exact) / math.log(REL_MAX_DIST / max_exact)
                         * (REL_BUCKETS - max_exact)).astype(jnp.int32)
    bucket = jnp.where(n < max_exact, n, jnp.minimum(large, REL_BUCKETS - 1))
    ki = lax.broadcasted_iota(jnp.int32, shape, 0)
    qi = lax.broadcasted_iota(jnp.int32, shape, 1)
    delta = qi + BLOCK - ki
    valid = (delta >= 0) & (delta < WINDOW) & (ki >= first_key)
    tiles = []
    for hd in range(SWA_HEADS):
        table = jnp.broadcast_to(bias_ref[hd:hd + 1, :], shape)
        bias = jnp.take_along_axis(table, bucket, axis=1, mode="promise_in_bounds")
        tiles.append(jnp.where(valid, bias * LOG2E, NEG_BIG))
    return tiles


def _swa_store_offsets(off_scr, slot, tiles):
    groups = SWA_HEADS // SWA_KV_HEADS
    for hd, tile in enumerate(tiles):
        off_scr[slot, hd // groups, :, (hd % groups) * BLOCK:(hd % groups + 1) * BLOCK] = tile


_SWA_SLOT_RUN = 0
_SWA_SLOT_RUN_FIRST = 1
_SWA_SLOT_BLOCK = 2


def _swa_kernel(bias_ref, sink_ref, qst_ref, kp_ref, kc_ref, vp_ref, vc_ref,
                prow_ref, pprev_ref, ot_ref, off_scr):
    bf = jnp.bfloat16
    step = pl.program_id(1)
    per = SEQ_TILE // BLOCK
    groups = SWA_HEADS // SWA_KV_HEADS

    @pl.when((pl.program_id(0) == 0) & (step == 0))
    def _():
        run_q = lax.broadcasted_iota(jnp.int32, (1, BLOCK), 1) + BLOCK
        run_k = lax.broadcasted_iota(jnp.int32, (2 * BLOCK, 1), 0)
        for slot, first_key in ((_SWA_SLOT_RUN, 0), (_SWA_SLOT_RUN_FIRST, BLOCK)):
            _swa_store_offsets(off_scr, slot, _swa_logit_offsets(bias_ref, run_q, run_k, first_key))

    pos = prow_ref[0]
    prev = pprev_ref[0]
    base = pos[:, 0:1]
    run_pos = pos - base == lax.broadcasted_iota(jnp.int32, pos.shape, 1)
    run_prev = prev - base + BLOCK == lax.broadcasted_iota(jnp.int32, prev.shape, 1)
    consecutive = (jnp.min(jnp.where(run_pos, 1.0, 0.0)) > 0.5) & (
        (jnp.min(jnp.where(run_prev, 1.0, 0.0)) > 0.5) | (step == 0))

    @pl.when(jnp.logical_not(consecutive))
    def _():
        band = jnp.concatenate([prev, pos], axis=1)
        for j in range(per):
            keys = band[:, j * BLOCK:(j + 2) * BLOCK]
            kpos = jnp.broadcast_to(keys, (2 * BLOCK, 2 * BLOCK)).T[:, 0:1]
            first_key = jnp.where(step == 0, BLOCK, 0) if j == 0 else 0
            tiles = _swa_logit_offsets(bias_ref, pos[:, j * BLOCK:(j + 1) * BLOCK], kpos, first_key)
            _swa_store_offsets(off_scr, _SWA_SLOT_BLOCK + j, tiles)

    def scores(j, kvh):
        q0 = j * BLOCK
        if j == 0:
            slot = jnp.where(consecutive, jnp.where(step == 0, _SWA_SLOT_RUN_FIRST, _SWA_SLOT_RUN),
                             _SWA_SLOT_BLOCK)
            kband = jnp.concatenate([kp_ref[0], kc_ref[0, :BLOCK]], axis=0)
        else:
            slot = jnp.where(consecutive, _SWA_SLOT_RUN, _SWA_SLOT_BLOCK + j)
            kband = kc_ref[0, q0 - BLOCK:q0 + BLOCK]
        k_h = kband[:, kvh * SWA_HEAD_DIM:(kvh + 1) * SWA_HEAD_DIM]
        qcat = jnp.concatenate(
            [qst_ref[0, 0, (kvh * groups + g) * SWA_HEAD_DIM:(kvh * groups + g + 1) * SWA_HEAD_DIM,
                     q0:q0 + BLOCK] for g in range(groups)], axis=1)
        return _dot(k_h, qcat) + off_scr[slot, kvh]

    def attend(j, kvh, s):
        q0 = j * BLOCK
        vband = (jnp.concatenate([vp_ref[0, 0], vc_ref[0, 0, :, :BLOCK]], axis=1) if j == 0
                 else vc_ref[0, 0, :, q0 - BLOCK:q0 + BLOCK])
        v_h = vband[kvh * SWA_V_EXT:(kvh + 1) * SWA_V_EXT]
        sink = jnp.concatenate([jnp.full((1, BLOCK), sink_ref[kvh * groups + g] * LOG2E, jnp.float32)
                                for g in range(groups)], axis=1)
        m = jnp.maximum(jnp.max(s, axis=0, keepdims=True), sink)
        ov = _dot(v_h, jnp.exp2(s - m).astype(bf))
        denom = ov[SWA_HEAD_DIM:SWA_HEAD_DIM + 1] + jnp.exp2(sink - m)
        out = (ov[:SWA_HEAD_DIM] * (1.0 / denom)).astype(bf)
        for g in range(groups):
            hd = kvh * groups + g
            ot_ref[0, 0, hd * SWA_HEAD_DIM:(hd + 1) * SWA_HEAD_DIM, q0:q0 + BLOCK] = (
                out[:, g * BLOCK:(g + 1) * BLOCK])

    chains = [(j, kvh) for j in range(per) for kvh in range(SWA_KV_HEADS)]
    ahead = [scores(*chain) for chain in chains[:SWA_LOOKAHEAD]]
    for c, chain in enumerate(chains):
        if c + SWA_LOOKAHEAD < len(chains):
            ahead.append(scores(*chains[c + SWA_LOOKAHEAD]))
        attend(*chain, ahead.pop(0))


def _out_kernel(oat_ref, obt_ref, gt_ref, x_ref, wo_ref, fn_ref, o_ref):
    tm = SEQ_TILE

    def gated(t):
        h = gt_ref[0, t].astype(jnp.float32) * 0.5
        silu = (h * (1.0 + jnp.tanh(h))).astype(jnp.bfloat16)
        return jnp.concatenate([oat_ref[0, t], obt_ref[0, t]], axis=0) * silu

    def finish(t, proj):
        z = x_ref[0, t * tm:(t + 1) * tm] + proj
        o_ref[0, t * tm:(t + 1) * tm] = _rms_lanes(z, fn_ref[...])

    proj = _dot_tn(gated(0), wo_ref[...])
    for t in range(1, OUT_TILES):
        y_next = gated(t)
        finish(t - 1, proj)
        proj = _dot_tn(y_next, wo_ref[...])
    finish(OUT_TILES - 1, proj)


def _const_spec(shape):
    return pl.BlockSpec(shape, lambda *_: (0,) * len(shape))


def _params(n_axes):
    return pltpu.CompilerParams(dimension_semantics=("arbitrary",) * n_axes,
                                vmem_limit_bytes=VMEM_LIMIT)


def kernel(x, positions, norm_gain, w_in, q_a_norm, w_q_b, kv_a_norm, w_kv_b, sinks, rel_bias, w_out,
           final_norm):
    bf, f32 = jnp.bfloat16, jnp.float32
    batch, seq, _ = x.shape
    assert norm_gain.shape[0] == 1, "single-layer kernel"
    tm = SEQ_TILE
    nt = seq // tm
    per = tm // BLOCK

    w = w_in[0]
    c = 0
    w_qlat, c = w[:, c:c + MLA_Q_LORA], c + MLA_Q_LORA
    w_kvlat, c = w[:, c:c + MLA_KV_LORA], c + MLA_KV_LORA
    w_krope, c = w[:, c:c + MLA_ROPE], c + MLA_ROPE
    w_qs, c = w[:, c:c + SWA_WIDTH], c + SWA_WIDTH
    w_ks, c = w[:, c:c + SWA_KV_WIDTH], c + SWA_KV_WIDTH
    w_vs, c = w[:, c:c + SWA_KV_WIDTH], c + SWA_KV_WIDTH
    w_gate = w[:, c:c + MIX_WIDTH]
    wt = jnp.concatenate([w_qlat, w_kvlat, w_qs, w_vs, w_gate, w_krope], axis=1).T.astype(bf)
    ws = jnp.concatenate([w_kvlat, w_ks], axis=1).astype(bf)

    wq = w_q_b[0].T.reshape(MLA_HEADS, MLA_QK, MLA_Q_LORA)
    wqt = jnp.pad(wq, ((0, 0), (0, HEAD_PAD - MLA_QK), (0, 0))).reshape(MLA_HEADS * HEAD_PAD, MLA_Q_LORA)
    wkv = w_kv_b[0].reshape(MLA_KV_LORA, MLA_HEADS, MLA_NOPE + MLA_V)
    wk = jnp.pad(wkv[:, :, :MLA_NOPE], ((0, 0), (0, 0), (0, HEAD_PAD - MLA_NOPE)))
    wk = wk.reshape(MLA_KV_LORA, MLA_HEADS * HEAD_PAD).astype(bf)
    wvt = wkv[:, :, MLA_NOPE:].reshape(MLA_KV_LORA, MLA_WIDTH).T.astype(bf)
    wqt = wqt.astype(bf)

    inv = ROPE_THETA ** (-jnp.arange(ROPE_HALF, dtype=f32) / ROPE_HALF)
    inv_col = inv.reshape(ROPE_HALF, 1)
    pos_row = positions.reshape(batch, 1, seq)

    tile4 = lambda rows: pl.BlockSpec((1, 1, rows, tm), lambda b, i: (b, i, 0, 0))
    ptiles = lambda rows: pl.BlockSpec((1, PROJ_TILES, rows, tm), lambda b, i: (b, i, 0, 0))
    ptm = PROJ_TILES * tm
    qt, k, vt, qst, ks, vst, gt = pl.pallas_call(
        _proj_kernel,
        grid=(batch, nt // PROJ_TILES),
        in_specs=[
            pl.BlockSpec((1, ptm, D_MODEL), lambda b, i: (b, i, 0)),
            pl.BlockSpec((1, 1, ptm), lambda b, i: (b, 0, i)),
            _const_spec((1, D_MODEL)),
            _const_spec((_T_ROWS, D_MODEL)),
            _const_spec((D_MODEL, _S_COLS)),
            _const_spec((MLA_Q_LORA, 1)),
            _const_spec((MLA_KV_LORA, 1)),
            _const_spec((1, MLA_KV_LORA)),
            _const_spec((MLA_HEADS * HEAD_PAD, MLA_Q_LORA)),
            _const_spec((MLA_WIDTH, MLA_KV_LORA)),
            _const_spec((MLA_KV_LORA, MLA_HEADS * HEAD_PAD)),
            _const_spec((ROPE_HALF, 1)),
        ],
        out_specs=[
            ptiles(MLA_HEADS * HEAD_PAD),
            pl.BlockSpec((1, MLA_HEADS, ptm, HEAD_PAD), lambda b, i: (b, 0, i, 0)),
            ptiles(MLA_HEADS * V_EXT),
            ptiles(SWA_WIDTH),
            pl.BlockSpec((1, ptm, SWA_KV_WIDTH), lambda b, i: (b, i, 0)),
            ptiles(SWA_KV_HEADS * SWA_V_EXT),
            ptiles(MIX_WIDTH),
        ],
        out_shape=[
            jax.ShapeDtypeStruct((batch, nt, MLA_HEADS * HEAD_PAD, tm), bf),
            jax.ShapeDtypeStruct((batch, MLA_HEADS, seq, HEAD_PAD), bf),
            jax.ShapeDtypeStruct((batch, nt, MLA_HEADS * V_EXT, tm), bf),
            jax.ShapeDtypeStruct((batch, nt, SWA_WIDTH, tm), bf),
            jax.ShapeDtypeStruct((batch, seq, SWA_KV_WIDTH), bf),
            jax.ShapeDtypeStruct((batch, nt, SWA_KV_HEADS * SWA_V_EXT, tm), bf),
            jax.ShapeDtypeStruct((batch, nt, MIX_WIDTH, tm), bf),
        ],
        compiler_params=_params(2),
        name="proj",
    )(x, pos_row, norm_gain[0].reshape(1, D_MODEL), wt, ws,
      q_a_norm[0].reshape(MLA_Q_LORA, 1), kv_a_norm[0].reshape(MLA_KV_LORA, 1),
      kv_a_norm[0].reshape(1, MLA_KV_LORA), wqt, wvt, wk, inv_col)

    schedule, n_iters = _mla_schedule(nt)
    oat = pl.pallas_call(
        functools.partial(_mla_kernel, n_iters=n_iters),
        grid=(batch, MLA_HEADS // MLA_GROUP),
        in_specs=[
            pl.BlockSpec(memory_space=pltpu.SMEM),
            pl.BlockSpec((1, nt, MLA_GROUP * HEAD_PAD, tm), lambda b, h: (b, 0, h, 0)),
            pl.BlockSpec((1, MLA_GROUP, seq, HEAD_PAD), lambda b, h: (b, h, 0, 0)),
            pl.BlockSpec((1, nt, MLA_GROUP * V_EXT, tm), lambda b, h: (b, 0, h, 0)),
        ],
        out_specs=pl.BlockSpec((1, nt, MLA_GROUP * MLA_V, tm), lambda b, h: (b, 0, h, 0)),
        out_shape=jax.ShapeDtypeStruct((batch, nt, MLA_WIDTH, tm), bf),
        scratch_shapes=[pltpu.VMEM((MLA_GROUP, MLA_UNROLL, tm, tm + MLA_LANE_PAD), f32),
                        pltpu.VMEM((2, MLA_GROUP, MLA_UNROLL, tm, tm + MLA_LANE_PAD), bf),
                        pltpu.VMEM((2, tm, tm), f32)],
        compiler_params=_params(2),
        name="mla",
    )(jnp.asarray(schedule), qt, k, vt)

    prev = lambda i: jnp.maximum(i * per - 1, 0)
    bias_t = jnp.pad(rel_bias.T.astype(f32), ((0, 0), (0, LANES - REL_BUCKETS)))
    obt = pl.pallas_call(
        _swa_kernel,
        grid=(batch, nt),
        in_specs=[
            _const_spec((SWA_HEADS, LANES)),
            pl.BlockSpec(memory_space=pltpu.SMEM),
            tile4(SWA_WIDTH),
            pl.BlockSpec((1, BLOCK, SWA_KV_WIDTH), lambda b, i: (b, prev(i), 0)),
            pl.BlockSpec((1, tm, SWA_KV_WIDTH), lambda b, i: (b, i, 0)),
            pl.BlockSpec((1, 1, SWA_KV_HEADS * SWA_V_EXT, BLOCK),
                         lambda b, i: (b, prev(i) // per, 0, prev(i) % per)),
            tile4(SWA_KV_HEADS * SWA_V_EXT),
            pl.BlockSpec((1, 1, tm), lambda b, i: (b, 0, i)),
            pl.BlockSpec((1, 1, BLOCK), lambda b, i: (b, 0, prev(i))),
        ],
        out_specs=tile4(SWA_WIDTH),
        out_shape=jax.ShapeDtypeStruct((batch, nt, SWA_WIDTH, tm), bf),
        scratch_shapes=[pltpu.VMEM((_SWA_SLOT_BLOCK + per, SWA_KV_HEADS, 2 * BLOCK,
                                    (SWA_HEADS // SWA_KV_HEADS) * BLOCK), f32)],
        compiler_params=_params(2),
        name="swa",
    )(bias_t, sinks[0].astype(f32), qst, ks, ks, vst, vst, pos_row, pos_row)

    tiles = lambda rows: pl.BlockSpec((1, OUT_TILES, rows, tm), lambda b, i: (b, i, 0, 0))
    return pl.pallas_call(
        _out_kernel,
        grid=(batch, nt // OUT_TILES),
        in_specs=[
            tiles(MLA_WIDTH),
            tiles(SWA_WIDTH),
            tiles(MIX_WIDTH),
            pl.BlockSpec((1, OUT_TILES * tm, D_MODEL), lambda b, i: (b, i, 0)),
            _const_spec((MIX_WIDTH, D_MODEL)),
            _const_spec((1, D_MODEL)),
        ],
        out_specs=pl.BlockSpec((1, OUT_TILES * tm, D_MODEL), lambda b, i: (b, i, 0)),
        out_shape=jax.ShapeDtypeStruct((batch, seq, D_MODEL), x.dtype),
        compiler_params=_params(2),
        name="out",
    )(oat, obt, gt, x, w_out[0].astype(bf), final_norm.reshape(1, D_MODEL))
```

```python
import functools
import math

import jax
import numpy as np
import jax.numpy as jnp
from jax import lax
from jax.experimental import pallas as pl
from jax.experimental.pallas import tpu as pltpu

D_MODEL = 1024
MLA_HEADS = 8
MLA_Q_LORA = 256
MLA_KV_LORA = 128
MLA_NOPE = 64
MLA_ROPE = 32
MLA_V = 64
SWA_HEADS = 8
SWA_KV_HEADS = 2
SWA_HEAD_DIM = 64
WINDOW = 128
BLOCK = 128
REL_BUCKETS = 32
REL_MAX_DIST = 128
ROPE_THETA = 10000.0
EPS = 1e-6

MLA_QK = MLA_NOPE + MLA_ROPE
MLA_WIDTH = MLA_HEADS * MLA_V
SWA_WIDTH = SWA_HEADS * SWA_HEAD_DIM
SWA_KV_WIDTH = SWA_KV_HEADS * SWA_HEAD_DIM
MIX_WIDTH = MLA_WIDTH + SWA_WIDTH
ROPE_HALF = MLA_ROPE // 2

LANES = 128
HEAD_PAD = LANES
BF16_ROWS = 16
V_EXT = MLA_V + BF16_ROWS
SWA_V_EXT = SWA_HEAD_DIM + BF16_ROWS
LOG2E = math.log2(math.e)
SEQ_TILE = 512
MLA_UNROLL = 4
MLA_STAGES = 3
MLA_GROUP = 2
MLA_LANE_PAD = LANES
PROJ_TILES = 2
OUT_TILES = 2
SWA_LOOKAHEAD = 3
NEG_BIG = -1e30
VMEM_LIMIT = 56 * 1024 * 1024

_T_QLAT = 0
_T_KVLAT = _T_QLAT + MLA_Q_LORA
_T_QS = _T_KVLAT + MLA_KV_LORA
_T_VS = _T_QS + SWA_WIDTH
_T_GATE = _T_VS + SWA_KV_WIDTH
_T_KROPE = _T_GATE + MIX_WIDTH
_T_ROWS = _T_KROPE + MLA_ROPE
_S_KVLAT = 0
_S_KS = LANES
_S_COLS = 2 * LANES


def _rms_rows(v, gain_col):
    ms = jnp.mean(v * v, axis=0, keepdims=True)
    return v * lax.rsqrt(ms + EPS) * gain_col


def _rms_lanes(v, gain_row):
    ms = jnp.mean(v * v, axis=-1, keepdims=True)
    return v * lax.rsqrt(ms + EPS) * gain_row


def _dot(a, b):
    return jnp.dot(a, b, preferred_element_type=jnp.float32)


def _dot_nt(a, b):
    return lax.dot_general(a, b, (((1,), (1,)), ((), ())), preferred_element_type=jnp.float32)


def _dot_tn(a, b):
    return lax.dot_general(a, b, (((0,), (0,)), ((), ())), preferred_element_type=jnp.float32)


def _proj_kernel(x_ref, prow_ref, g1_ref, wt_ref, ws_ref, gq_ref, gkvc_ref, gkvr_ref,
                 wqt_ref, wvt_ref, wk_ref, invc_ref,
                 qt_ref, k_ref, vt_ref, qst_ref, ks_ref, vst_ref, gt_ref):
    bf = jnp.bfloat16
    part = x_ref.shape[1] // PROJ_TILES

    def normed(p):
        return _rms_lanes(x_ref[0, p * part:(p + 1) * part], g1_ref[...]).astype(bf)

    def project(h):
        return _dot_nt(wt_ref[...], h), _dot(h, ws_ref[...])

    def finish(p, pt, ps):
        tok = slice(p * part, (p + 1) * part)
        qn = _rms_rows(pt[_T_QLAT:_T_KVLAT], gq_ref[...]).astype(bf)
        qt = _dot(wqt_ref[...], qn) * (MLA_QK ** -0.5 * LOG2E)
        ang_t = invc_ref[...] * prow_ref[0, :, tok].astype(jnp.float32)
        cos_t, sin_t = jnp.cos(ang_t), jnp.sin(ang_t)
        for hd in range(MLA_HEADS):
            r0 = hd * HEAD_PAD
            x1 = qt[r0 + MLA_NOPE:r0 + MLA_NOPE + ROPE_HALF]
            x2 = qt[r0 + MLA_NOPE + ROPE_HALF:r0 + MLA_QK]
            qt_ref[0, p, r0:r0 + MLA_NOPE, :] = qt[r0:r0 + MLA_NOPE].astype(bf)
            qt_ref[0, p, r0 + MLA_NOPE:r0 + MLA_NOPE + ROPE_HALF, :] = (x1 * cos_t - x2 * sin_t).astype(bf)
            qt_ref[0, p, r0 + MLA_NOPE + ROPE_HALF:r0 + MLA_QK, :] = (x2 * cos_t + x1 * sin_t).astype(bf)
            qt_ref[0, p, r0 + MLA_QK:r0 + HEAD_PAD, :] = qt[r0 + MLA_QK:r0 + HEAD_PAD].astype(bf)

        kvn_t = _rms_rows(pt[_T_KVLAT:_T_QS], gkvc_ref[...]).astype(bf)
        vt = _dot(wvt_ref[...], kvn_t).astype(bf)
        ones_row = (lax.broadcasted_iota(jnp.int32, (BF16_ROWS, part), 0) == 0).astype(bf)
        for hd in range(MLA_HEADS):
            vt_ref[0, p, hd * V_EXT:hd * V_EXT + MLA_V, :] = vt[hd * MLA_V:(hd + 1) * MLA_V]
            vt_ref[0, p, hd * V_EXT + MLA_V:(hd + 1) * V_EXT, :] = ones_row
        kvn_s = _rms_lanes(ps[:, _S_KVLAT:_S_KS], gkvr_ref[...]).astype(bf)
        kn = _dot(kvn_s, wk_ref[...])
        y1 = pt[_T_KROPE:_T_KROPE + ROPE_HALF]
        y2 = pt[_T_KROPE + ROPE_HALF:_T_ROWS]
        kpe_t = jnp.concatenate([jnp.zeros((MLA_NOPE, part), jnp.float32),
                                 y1 * cos_t - y2 * sin_t, y2 * cos_t + y1 * sin_t,
                                 jnp.zeros((HEAD_PAD - MLA_QK, part), jnp.float32)], axis=0)
        kpe = kpe_t.T
        for hd in range(MLA_HEADS):
            k_ref[0, hd, tok] = (kn[:, hd * HEAD_PAD:(hd + 1) * HEAD_PAD] + kpe).astype(bf)

        qst_ref[0, p] = (pt[_T_QS:_T_VS] * (SWA_HEAD_DIM ** -0.5 * LOG2E)).astype(bf)
        ks_ref[0, tok] = ps[:, _S_KS:_S_COLS].astype(bf)
        for hd in range(SWA_KV_HEADS):
            vst_ref[0, p, hd * SWA_V_EXT:hd * SWA_V_EXT + SWA_HEAD_DIM, :] = (
                pt[_T_VS + hd * SWA_HEAD_DIM:_T_VS + (hd + 1) * SWA_HEAD_DIM].astype(bf))
            vst_ref[0, p, hd * SWA_V_EXT + SWA_HEAD_DIM:(hd + 1) * SWA_V_EXT, :] = ones_row
        gt_ref[0, p] = pt[_T_GATE:_T_KROPE].astype(bf)

    pt, ps = project(normed(0))
    for p in range(1, PROJ_TILES):
        h_next = normed(p)
        finish(p - 1, pt, ps)
        pt, ps = project(h_next)
    finish(PROJ_TILES - 1, pt, ps)


def _mla_schedule(n_tiles):
    rows, todo = [], list(range(n_tiles))
    while todo:
        gap = -len(rows) % MLA_UNROLL
        q = min(t for t in todo if t + 1 > gap)
        todo.remove(q)
        keys = list(range(q))
        keys.insert(gap, q)
        rows += [(q, kt, int(kt == q), int(j == 0)) for j, kt in enumerate(keys)]
    assert len(rows) % MLA_UNROLL == 0 and all(r[2] == 0 for j, r in enumerate(rows) if j % MLA_UNROLL)
    pad = (MLA_STAGES - 1) * MLA_UNROLL
    n_iters = len(rows) // MLA_UNROLL + (MLA_STAGES - 1)
    extra = (n_iters % 2) * MLA_UNROLL
    rows = [rows[0]] * (pad + extra) + rows + [rows[-1]] * pad
    return np.asarray(rows, np.int32).T.copy(), n_iters + n_iters % 2


def _mla_kernel(tab_ref, qt_ref, k_ref, vt_ref, ot_ref, s_scr, p_scr, mask_scr, *, n_iters):
    bf = jnp.bfloat16
    u_n = MLA_UNROLL
    tq = SEQ_TILE

    @pl.when((pl.program_id(0) == 0) & (pl.program_id(1) == 0))
    def _():
        key = lax.broadcasted_iota(jnp.int32, (tq, tq), 0)
        qry = lax.broadcasted_iota(jnp.int32, (tq, tq), 1)
        mask_scr[0] = jnp.zeros((tq, tq), jnp.float32)
        mask_scr[1] = jnp.where(key <= qry, 0.0, NEG_BIG)

    def iteration(i, half, carry, stages="CBA"):
        ms, accs, m_tiles, alphas = [list(c) for c in carry]
        new_tiles = [list(t) for t in m_tiles]
        new_alphas = [list(a) for a in alphas]

        def value_stage(g, u):
            f = i * u_n + u
            v_tile = vt_ref[0, tab_ref[1, f], g * V_EXT:(g + 1) * V_EXT, :]
            pv = _dot(v_tile, p_scr[half, g, u, :, :tq])
            acc = jnp.where(tab_ref[3, f] == 1, pv, alphas[g][u] * accs[g] + pv)
            ot_ref[0, tab_ref[0, f], g * MLA_V:(g + 1) * MLA_V, :] = (
                acc[:MLA_V] * (1.0 / acc[MLA_V:MLA_V + 1])).astype(bf)
            accs[g] = acc

        def exp_stage(g, u):
            f = (i + 1) * u_n + u
            m_in = jnp.where(tab_ref[3, f] == 1, NEG_BIG, ms[g])
            m = jnp.maximum(m_in, m_tiles[g][u])
            new_alphas[g][u] = jnp.exp2(m_in - m)
            p_scr[1 - half, g, u, :, :tq] = jnp.exp2(s_scr[g, u, :, :tq] - m).astype(bf)
            ms[g] = m

        def score_stage(g, u):
            f = (i + 2) * u_n + u
            kj = k_ref[0, g, pl.ds(pl.multiple_of(tab_ref[1, f] * tq, tq), tq), :]
            st = _dot(kj, qt_ref[0, tab_ref[0, f], g * HEAD_PAD:(g + 1) * HEAD_PAD, :])
            if u == 0:
                st = st + mask_scr[tab_ref[2, f]]
            s_scr[g, u, :, :tq] = st
            new_tiles[g][u] = jnp.max(st, axis=0, keepdims=True)

        for u in range(u_n):
            for g in range(MLA_GROUP):
                if "C" in stages:
                    value_stage(g, u)
                if "B" in stages:
                    exp_stage(g, u)
                if "A" in stages:
                    score_stage(g, u)
        return (tuple(ms), tuple(accs), tuple(tuple(t) for t in new_tiles),
                tuple(tuple(a) for a in new_alphas))

    def body(i2, carry):
        return iteration(2 * i2 + 1, 1, iteration(2 * i2, 0, carry))

    assert n_iters % 2 == 0 and n_iters >= 4
    row = mask_scr[0, 0:1, :]
    acc0 = jnp.zeros((V_EXT, tq), jnp.float32)
    carry = ((row,) * MLA_GROUP, (acc0,) * MLA_GROUP,
             ((row,) * u_n,) * MLA_GROUP, ((row,) * u_n,) * MLA_GROUP)
    carry = iteration(1, 1, iteration(0, 0, carry, "A"), "BA")
    carry = lax.fori_loop(1, n_iters // 2 - 1, body, carry)
    iteration(n_iters - 1, 1, iteration(n_iters - 2, 0, carry, "CB"), "C")


def _swa_logit_offsets(bias_ref, qpos, kpos, first_key):
    shape = (2 * BLOCK, BLOCK)
    n = jnp.maximum(qpos - kpos, 0)
    max_exact = REL_BUCKETS // 2
    nf = jnp.maximum(n, 1).astype(jnp.float32)
    large = max_exact + (jnp.log(nf / max_exact) / math.log(REL_MAX_DIST / max_exact)
                         * (REL_BUCKETS - max_exact)).astype(jnp.int32)
    bucket = jnp.where(n < max_exact, n, jnp.minimum(large, REL_BUCKETS - 1))
    ki = lax.broadcasted_iota(jnp.int32, shape, 0)
    qi = lax.broadcasted_iota(jnp.int32, shape, 1)
    delta = qi + BLOCK - ki
    valid = (delta >= 0) & (delta < WINDOW) & (ki >= first_key)
    tiles = []
    for hd in range(SWA_HEADS):
        table = jnp.broadcast_to(bias_ref[hd:hd + 1, :], shape)
        bias = jnp.take_along_axis(table, bucket, axis=1, mode="promise_in_bounds")
        tiles.append(jnp.where(valid, bias * LOG2E, NEG_BIG))
    return tiles


def _swa_store_offsets(off_scr, slot, tiles):
    groups = SWA_HEADS // SWA_KV_HEADS
    for hd, tile in enumerate(tiles):
        off_scr[slot, hd // groups, :, (hd % groups) * BLOCK:(hd % groups + 1) * BLOCK] = tile


_SWA_SLOT_RUN = 0
_SWA_SLOT_RUN_FIRST = 1
_SWA_SLOT_BLOCK = 2


def _swa_kernel(bias_ref, sink_ref, qst_ref, kp_ref, kc_ref, vp_ref, vc_ref,
                prow_ref, pprev_ref, ot_ref, off_scr):
    bf = jnp.bfloat16
    step = pl.program_id(1)
    per = SEQ_TILE // BLOCK
    groups = SWA_HEADS // SWA_KV_HEADS

    @pl.when((pl.program_id(0) == 0) & (step == 0))
    def _():
        run_q = lax.broadcasted_iota(jnp.int32, (1, BLOCK), 1) + BLOCK
        run_k = lax.broadcasted_iota(jnp.int32, (2 * BLOCK, 1), 0)
        for slot, first_key in ((_SWA_SLOT_RUN, 0), (_SWA_SLOT_RUN_FIRST, BLOCK)):
            _swa_store_offsets(off_scr, slot, _swa_logit_offsets(bias_ref, run_q, run_k, first_key))

    pos = prow_ref[0]
    prev = pprev_ref[0]
    base = pos[:, 0:1]
    run_pos = pos - base == lax.broadcasted_iota(jnp.int32, pos.shape, 1)
    run_prev = prev - base + BLOCK == lax.broadcasted_iota(jnp.int32, prev.shape, 1)
    consecutive = (jnp.min(jnp.where(run_pos, 1.0, 0.0)) > 0.5) & (
        (jnp.min(jnp.where(run_prev, 1.0, 0.0)) > 0.5) | (step == 0))

    @pl.when(jnp.logical_not(consecutive))
    def _():
        band = jnp.concatenate([prev, pos], axis=1)
        for j in range(per):
            keys = band[:, j * BLOCK:(j + 2) * BLOCK]
            kpos = jnp.broadcast_to(keys, (2 * BLOCK, 2 * BLOCK)).T[:, 0:1]
            first_key = jnp.where(step == 0, BLOCK, 0) if j == 0 else 0
            tiles = _swa_logit_offsets(bias_ref, pos[:, j * BLOCK:(j + 1) * BLOCK], kpos, first_key)
            _swa_store_offsets(off_scr, _SWA_SLOT_BLOCK + j, tiles)

    def scores(j, kvh):
        q0 = j * BLOCK
        if j == 0:
            slot = jnp.where(consecutive, jnp.where(step == 0, _SWA_SLOT_RUN_FIRST, _SWA_SLOT_RUN),
                             _SWA_SLOT_BLOCK)
            kband = jnp.concatenate([kp_ref[0], kc_ref[0, :BLOCK]], axis=0)
        else:
            slot = jnp.where(consecutive, _SWA_SLOT_RUN, _SWA_SLOT_BLOCK + j)
            kband = kc_ref[0, q0 - BLOCK:q0 + BLOCK]
        k_h = kband[:, kvh * SWA_HEAD_DIM:(kvh + 1) * SWA_HEAD_DIM]
        qcat = jnp.concatenate(
            [qst_ref[0, 0, (kvh * groups + g) * SWA_HEAD_DIM:(kvh * groups + g + 1) * SWA_HEAD_DIM,
                     q0:q0 + BLOCK] for g in range(groups)], axis=1)
        return _dot(k_h, qcat) + off_scr[slot, kvh]

    def attend(j, kvh, s):
        q0 = j * BLOCK
        vband = (jnp.concatenate([vp_ref[0, 0], vc_ref[0, 0, :, :BLOCK]], axis=1) if j == 0
                 else vc_ref[0, 0, :, q0 - BLOCK:q0 + BLOCK])
        v_h = vband[kvh * SWA_V_EXT:(kvh + 1) * SWA_V_EXT]
        sink = jnp.concatenate([jnp.full((1, BLOCK), sink_ref[kvh * groups + g] * LOG2E, jnp.float32)
                                for g in range(groups)], axis=1)
        m = jnp.maximum(jnp.max(s, axis=0, keepdims=True), sink)
        ov = _dot(v_h, jnp.exp2(s - m).astype(bf))
        denom = ov[SWA_HEAD_DIM:SWA_HEAD_DIM + 1] + jnp.exp2(sink - m)
        out = (ov[:SWA_HEAD_DIM] * (1.0 / denom)).astype(bf)
        for g in range(groups):
            hd = kvh * groups + g
            ot_ref[0, 0, hd * SWA_HEAD_DIM:(hd + 1) * SWA_HEAD_DIM, q0:q0 + BLOCK] = (
                out[:, g * BLOCK:(g + 1) * BLOCK])

    chains = [(j, kvh) for j in range(per) for kvh in range(SWA_KV_HEADS)]
    ahead = [scores(*chain) for chain in chains[:SWA_LOOKAHEAD]]
    for c, chain in enumerate(chains):
        if c + SWA_LOOKAHEAD < len(chains):
            ahead.append(scores(*chains[c + SWA_LOOKAHEAD]))
        attend(*chain, ahead.pop(0))


def _out_kernel(oat_ref, obt_ref, gt_ref, x_ref, wo_ref, fn_ref, o_ref):
    tm = SEQ_TILE

    def gated(t):
        h = gt_ref[0, t].astype(jnp.float32) * 0.5
        silu = (h * (1.0 + jnp.tanh(h))).astype(jnp.bfloat16)
        return jnp.concatenate([oat_ref[0, t], obt_ref[0, t]], axis=0) * silu

    def finish(t, proj):
        z = x_ref[0, t * tm:(t + 1) * tm] + proj
        o_ref[0, t * tm:(t + 1) * tm] = _rms_lanes(z, fn_ref[...])

    proj = _dot_tn(gated(0), wo_ref[...])
    for t in range(1, OUT_TILES):
        y_next = gated(t)
        finish(t - 1, proj)
        proj = _dot_tn(y_next, wo_ref[...])
    finish(OUT_TILES - 1, proj)


def _const_spec(shape):
    return pl.BlockSpec(shape, lambda *_: (0,) * len(shape))


def _params(n_axes):
    return pltpu.CompilerParams(dimension_semantics=("arbitrary",) * n_axes,
                                vmem_limit_bytes=VMEM_LIMIT)


def kernel(x, positions, norm_gain, w_in, q_a_norm, w_q_b, kv_a_norm, w_kv_b, sinks, rel_bias, w_out,
           final_norm):
    bf, f32 = jnp.bfloat16, jnp.float32
    batch, seq, _ = x.shape
    assert norm_gain.shape[0] == 1, "single-layer kernel"
    tm = SEQ_TILE
    nt = seq // tm
    per = tm // BLOCK

    w = w_in[0]
    c = 0
    w_qlat, c = w[:, c:c + MLA_Q_LORA], c + MLA_Q_LORA
    w_kvlat, c = w[:, c:c + MLA_KV_LORA], c + MLA_KV_LORA
    w_krope, c = w[:, c:c + MLA_ROPE], c + MLA_ROPE
    w_qs, c = w[:, c:c + SWA_WIDTH], c + SWA_WIDTH
    w_ks, c = w[:, c:c + SWA_KV_WIDTH], c + SWA_KV_WIDTH
    w_vs, c = w[:, c:c + SWA_KV_WIDTH], c + SWA_KV_WIDTH
    w_gate = w[:, c:c + MIX_WIDTH]
    wt = jnp.concatenate([w_qlat, w_kvlat, w_qs, w_vs, w_gate, w_krope], axis=1).T.astype(bf)
    ws = jnp.concatenate([w_kvlat, w_ks], axis=1).astype(bf)

    wq = w_q_b[0].T.reshape(MLA_HEADS, MLA_QK, MLA_Q_LORA)
    wqt = jnp.pad(wq, ((0, 0), (0, HEAD_PAD - MLA_QK), (0, 0))).reshape(MLA_HEADS * HEAD_PAD, MLA_Q_LORA)
    wkv = w_kv_b[0].reshape(MLA_KV_LORA, MLA_HEADS, MLA_NOPE + MLA_V)
    wk = jnp.pad(wkv[:, :, :MLA_NOPE], ((0, 0), (0, 0), (0, HEAD_PAD - MLA_NOPE)))
    wk = wk.reshape(MLA_KV_LORA, MLA_HEADS * HEAD_PAD).astype(bf)
    wvt = wkv[:, :, MLA_NOPE:].reshape(MLA_KV_LORA, MLA_WIDTH).T.astype(bf)
    wqt = wqt.astype(bf)

    inv = ROPE_THETA ** (-jnp.arange(ROPE_HALF, dtype=f32) / ROPE_HALF)
    inv_col = inv.reshape(ROPE_HALF, 1)
    pos_row = positions.reshape(batch, 1, seq)

    tile4 = lambda rows: pl.BlockSpec((1, 1, rows, tm), lambda b, i: (b, i, 0, 0))
    ptiles = lambda rows: pl.BlockSpec((1, PROJ_TILES, rows, tm), lambda b, i: (b, i, 0, 0))
    ptm = PROJ_TILES * tm
    qt, k, vt, qst, ks, vst, gt = pl.pallas_call(
        _proj_kernel,
        grid=(batch, nt // PROJ_TILES),
        in_specs=[
            pl.BlockSpec((1, ptm, D_MODEL), lambda b, i: (b, i, 0)),
            pl.BlockSpec((1, 1, ptm), lambda b, i: (b, 0, i)),
            _const_spec((1, D_MODEL)),
            _const_spec((_T_ROWS, D_MODEL)),
            _const_spec((D_MODEL, _S_COLS)),
            _const_spec((MLA_Q_LORA, 1)),
            _const_spec((MLA_KV_LORA, 1)),
            _const_spec((1, MLA_KV_LORA)),
            _const_spec((MLA_HEADS * HEAD_PAD, MLA_Q_LORA)),
            _const_spec((MLA_WIDTH, MLA_KV_LORA)),
            _const_spec((MLA_KV_LORA, MLA_HEADS * HEAD_PAD)),
            _const_spec((ROPE_HALF, 1)),
        ],
        out_specs=[
            ptiles(MLA_HEADS * HEAD_PAD),
            pl.BlockSpec((1, MLA_HEADS, ptm, HEAD_PAD), lambda b, i: (b, 0, i, 0)),
            ptiles(MLA_HEADS * V_EXT),
            ptiles(SWA_WIDTH),
            pl.BlockSpec((1, ptm, SWA_KV_WIDTH), lambda b, i: (b, i, 0)),
            ptiles(SWA_KV_HEADS * SWA_V_EXT),
            ptiles(MIX_WIDTH),
        ],
        out_shape=[
            jax.ShapeDtypeStruct((batch, nt, MLA_HEADS * HEAD_PAD, tm), bf),
            jax.ShapeDtypeStruct((batch, MLA_HEADS, seq, HEAD_PAD), bf),
            jax.ShapeDtypeStruct((batch, nt, MLA_HEADS * V_EXT, tm), bf),
            jax.ShapeDtypeStruct((batch, nt, SWA_WIDTH, tm), bf),
            jax.ShapeDtypeStruct((batch, seq, SWA_KV_WIDTH), bf),
            jax.ShapeDtypeStruct((batch, nt, SWA_KV_HEADS * SWA_V_EXT, tm), bf),
            jax.ShapeDtypeStruct((batch, nt, MIX_WIDTH, tm), bf),
        ],
        compiler_params=_params(2),
        name="proj",
    )(x, pos_row, norm_gain[0].reshape(1, D_MODEL), wt, ws,
      q_a_norm[0].reshape(MLA_Q_LORA, 1), kv_a_norm[0].reshape(MLA_KV_LORA, 1),
      kv_a_norm[0].reshape(1, MLA_KV_LORA), wqt, wvt, wk, inv_col)

    schedule, n_iters = _mla_schedule(nt)
    oat = pl.pallas_call(
        functools.partial(_mla_kernel, n_iters=n_iters),
        grid=(batch, MLA_HEADS // MLA_GROUP),
        in_specs=[
            pl.BlockSpec(memory_space=pltpu.SMEM),
            pl.BlockSpec((1, nt, MLA_GROUP * HEAD_PAD, tm), lambda b, h: (b, 0, h, 0)),
            pl.BlockSpec((1, MLA_GROUP, seq, HEAD_PAD), lambda b, h: (b, h, 0, 0)),
            pl.BlockSpec((1, nt, MLA_GROUP * V_EXT, tm), lambda b, h: (b, 0, h, 0)),
        ],
        out_specs=pl.BlockSpec((1, nt, MLA_GROUP * MLA_V, tm), lambda b, h: (b, 0, h, 0)),
        out_shape=jax.ShapeDtypeStruct((batch, nt, MLA_WIDTH, tm), bf),
        scratch_shapes=[pltpu.VMEM((MLA_GROUP, MLA_UNROLL, tm, tm + MLA_LANE_PAD), f32),
                        pltpu.VMEM((2, MLA_GROUP, MLA_UNROLL, tm, tm + MLA_LANE_PAD), bf),
                        pltpu.VMEM((2, tm, tm), f32)],
        compiler_params=_params(2),
        name="mla",
    )(jnp.asarray(schedule), qt, k, vt)

    prev = lambda i: jnp.maximum(i * per - 1, 0)
    bias_t = jnp.pad(rel_bias.T.astype(f32), ((0, 0), (0, LANES - REL_BUCKETS)))
    obt = pl.pallas_call(
        _swa_kernel,
        grid=(batch, nt),
        in_specs=[
            _const_spec((SWA_HEADS, LANES)),
            pl.BlockSpec(memory_space=pltpu.SMEM),
            tile4(SWA_WIDTH),
            pl.BlockSpec((1, BLOCK, SWA_KV_WIDTH), lambda b, i: (b, prev(i), 0)),
            pl.BlockSpec((1, tm, SWA_KV_WIDTH), lambda b, i: (b, i, 0)),
            pl.BlockSpec((1, 1, SWA_KV_HEADS * SWA_V_EXT, BLOCK),
                         lambda b, i: (b, prev(i) // per, 0, prev(i) % per)),
            tile4(SWA_KV_HEADS * SWA_V_EXT),
            pl.BlockSpec((1, 1, tm), lambda b, i: (b, 0, i)),
            pl.BlockSpec((1, 1, BLOCK), lambda b, i: (b, 0, prev(i))),
        ],
        out_specs=tile4(SWA_WIDTH),
        out_shape=jax.ShapeDtypeStruct((batch, nt, SWA_WIDTH, tm), bf),
        scratch_shapes=[pltpu.VMEM((_SWA_SLOT_BLOCK + per, SWA_KV_HEADS, 2 * BLOCK,
                                    (SWA_HEADS // SWA_KV_HEADS) * BLOCK), f32)],
        compiler_params=_params(2),
        name="swa",
    )(bias_t, sinks[0].astype(f32), qst, ks, ks, vst, vst, pos_row, pos_row)

    tiles = lambda rows: pl.BlockSpec((1, OUT_TILES, rows, tm), lambda b, i: (b, i, 0, 0))
    return pl.pallas_call(
        _out_kernel,
        grid=(batch, nt // OUT_TILES),
        in_specs=[
            tiles(MLA_WIDTH),
            tiles(SWA_WIDTH),
            tiles(MIX_WIDTH),
            pl.BlockSpec((1, OUT_TILES * tm, D_MODEL), lambda b, i: (b, i, 0)),
            _const_spec((MIX_WIDTH, D_MODEL)),
            _const_spec((1, D_MODEL)),
        ],
        out_specs=pl.BlockSpec((1, OUT_TILES * tm, D_MODEL), lambda b, i: (b, i, 0)),
        out_shape=jax.ShapeDtypeStruct((batch, seq, D_MODEL), x.dtype),
        compiler_params=_params(2),
        name="out",
    )(oat, obt, gt, x, w_out[0].astype(bf), final_norm.reshape(1, D_MODEL))
```

```python
import functools
import math

import jax
import numpy as np
import jax.numpy as jnp
from jax import lax
from jax.experimental import pallas as pl
from jax.experimental.pallas import tpu as pltpu

D_MODEL = 1024
MLA_HEADS = 8
MLA_Q_LORA = 256
MLA_KV_LORA = 128
MLA_NOPE = 64
MLA_ROPE = 32
MLA_V = 64
SWA_HEADS = 8
SWA_KV_HEADS = 2
SWA_HEAD_DIM = 64
WINDOW = 128
BLOCK = 128
REL_BUCKETS = 32
REL_MAX_DIST = 128
ROPE_THETA = 10000.0
EPS = 1e-6

MLA_QK = MLA_NOPE + MLA_ROPE
MLA_WIDTH = MLA_HEADS * MLA_V
SWA_WIDTH = SWA_HEADS * SWA_HEAD_DIM
SWA_KV_WIDTH = SWA_KV_HEADS * SWA_HEAD_DIM
MIX_WIDTH = MLA_WIDTH + SWA_WIDTH
ROPE_HALF = MLA_ROPE // 2

LANES = 128
HEAD_PAD = LANES
BF16_ROWS = 16
V_EXT = MLA_V + BF16_ROWS
SWA_V_EXT = SWA_HEAD_DIM + BF16_ROWS
LOG2E = math.log2(math.e)
SEQ_TILE = 512
MLA_UNROLL = 4
MLA_STAGES = 3
MLA_GROUP = 2
MLA_LANE_PAD = LANES
PROJ_TILES = 2
OUT_TILES = 2
SWA_LOOKAHEAD = 3
NEG_BIG = -1e30
VMEM_LIMIT = 56 * 1024 * 1024

_T_QLAT = 0
_T_KVLAT = _T_QLAT + MLA_Q_LORA
_T_QS = _T_KVLAT + MLA_KV_LORA
_T_VS = _T_QS + SWA_WIDTH
_T_GATE = _T_VS + SWA_KV_WIDTH
_T_KROPE = _T_GATE + MIX_WIDTH
_T_ROWS = _T_KROPE + MLA_ROPE
_S_KVLAT = 0
_S_KS = LANES
_S_COLS = 2 * LANES


def _rms_rows(v, gain_col):
    ms = jnp.mean(v * v, axis=0, keepdims=True)
    return v * lax.rsqrt(ms + EPS) * gain_col


def _rms_lanes(v, gain_row):
    ms = jnp.mean(v * v, axis=-1, keepdims=True)
    return v * lax.rsqrt(ms + EPS) * gain_row


def _dot(a, b):
    return jnp.dot(a, b, preferred_element_type=jnp.float32)


def _dot_nt(a, b):
    return lax.dot_general(a, b, (((1,), (1,)), ((), ())), preferred_element_type=jnp.float32)


def _dot_tn(a, b):
    return lax.dot_general(a, b, (((0,), (0,)), ((), ())), preferred_element_type=jnp.float32)


def _proj_kernel(x_ref, prow_ref, g1_ref, wt_ref, ws_ref, gq_ref, gkvc_ref, gkvr_ref,
                 wqt_ref, wvt_ref, wk_ref, invc_ref,
                 qt_ref, k_ref, vt_ref, qst_ref, ks_ref, vst_ref, gt_ref):
    bf = jnp.bfloat16
    part = x_ref.shape[1] // PROJ_TILES

    def normed(p):
        return _rms_lanes(x_ref[0, p * part:(p + 1) * part], g1_ref[...]).astype(bf)

    def project(h):
        return _dot_nt(wt_ref[...], h), _dot(h, ws_ref[...])

    def finish(p, pt, ps):
        tok = slice(p * part, (p + 1) * part)
        qn = _rms_rows(pt[_T_QLAT:_T_KVLAT], gq_ref[...]).astype(bf)
        qt = _dot(wqt_ref[...], qn) * (MLA_QK ** -0.5 * LOG2E)
        ang_t = invc_ref[...] * prow_ref[0, :, tok].astype(jnp.float32)
        cos_t, sin_t = jnp.cos(ang_t), jnp.sin(ang_t)
        for hd in range(MLA_HEADS):
            r0 = hd * HEAD_PAD
            x1 = qt[r0 + MLA_NOPE:r0 + MLA_NOPE + ROPE_HALF]
            x2 = qt[r0 + MLA_NOPE + ROPE_HALF:r0 + MLA_QK]
            qt_ref[0, p, r0:r0 + MLA_NOPE, :] = qt[r0:r0 + MLA_NOPE].astype(bf)
            qt_ref[0, p, r0 + MLA_NOPE:r0 + MLA_NOPE + ROPE_HALF, :] = (x1 * cos_t - x2 * sin_t).astype(bf)
            qt_ref[0, p, r0 + MLA_NOPE + ROPE_HALF:r0 + MLA_QK, :] = (x2 * cos_t + x1 * sin_t).astype(bf)
            qt_ref[0, p, r0 + MLA_QK:r0 + HEAD_PAD, :] = qt[r0 + MLA_QK:r0 + HEAD_PAD].astype(bf)

        kvn_t = _rms_rows(pt[_T_KVLAT:_T_QS], gkvc_ref[...]).astype(bf)
        vt = _dot(wvt_ref[...], kvn_t).astype(bf)
        ones_row = (lax.broadcasted_iota(jnp.int32, (BF16_ROWS, part), 0) == 0).astype(bf)
        for hd in range(MLA_HEADS):
            vt_ref[0, p, hd * V_EXT:hd * V_EXT + MLA_V, :] = vt[hd * MLA_V:(hd + 1) * MLA_V]
            vt_ref[0, p, hd * V_EXT + MLA_V:(hd + 1) * V_EXT, :] = ones_row
        kvn_s = _rms_lanes(ps[:, _S_KVLAT:_S_KS], gkvr_ref[...]).astype(bf)
        kn = _dot(kvn_s, wk_ref[...])
        y1 = pt[_T_KROPE:_T_KROPE + ROPE_HALF]
        y2 = pt[_T_KROPE + ROPE_HALF:_T_ROWS]
        kpe_t = jnp.concatenate([jnp.zeros((MLA_NOPE, part), jnp.float32),
                                 y1 * cos_t - y2 * sin_t, y2 * cos_t + y1 * sin_t,
                                 jnp.zeros((HEAD_PAD - MLA_QK, part), jnp.float32)], axis=0)
        kpe = kpe_t.T
        for hd in range(MLA_HEADS):
            k_ref[0, hd, tok] = (kn[:, hd * HEAD_PAD:(hd + 1) * HEAD_PAD] + kpe).astype(bf)

        qst_ref[0, p] = (pt[_T_QS:_T_VS] * (SWA_HEAD_DIM ** -0.5 * LOG2E)).astype(bf)
        ks_ref[0, tok] = ps[:, _S_KS:_S_COLS].astype(bf)
        for hd in range(SWA_KV_HEADS):
            vst_ref[0, p, hd * SWA_V_EXT:hd * SWA_V_EXT + SWA_HEAD_DIM, :] = (
                pt[_T_VS + hd * SWA_HEAD_DIM:_T_VS + (hd + 1) * SWA_HEAD_DIM].astype(bf))
            vst_ref[0, p, hd * SWA_V_EXT + SWA_HEAD_DIM:(hd + 1) * SWA_V_EXT, :] = ones_row
        gt_ref[0, p] = pt[_T_GATE:_T_KROPE].astype(bf)

    pt, ps = project(normed(0))
    for p in range(1, PROJ_TILES):
        h_next = normed(p)
        finish(p - 1, pt, ps)
        pt, ps = project(h_next)
    finish(PROJ_TILES - 1, pt, ps)


def _mla_schedule(n_tiles):
    rows, todo = [], list(range(n_tiles))
    while todo:
        gap = -len(rows) % MLA_UNROLL
        q = min(t for t in todo if t + 1 > gap)
        todo.remove(q)
        keys = list(range(q))
        keys.insert(gap, q)
        rows += [(q, kt, int(kt == q), int(j == 0)) for j, kt in enumerate(keys)]
    assert len(rows) % MLA_UNROLL == 0 and all(r[2] == 0 for j, r in enumerate(rows) if j % MLA_UNROLL)
    pad = (MLA_STAGES - 1) * MLA_UNROLL
    n_iters = len(rows) // MLA_UNROLL + (MLA_STAGES - 1)
    extra = (n_iters % 2) * MLA_UNROLL
    rows = [rows[0]] * (pad + extra) + rows + [rows[-1]] * pad
    return np.asarray(rows, np.int32).T.copy(), n_iters + n_iters % 2


def _mla_kernel(tab_ref, qt_ref, k_ref, vt_ref, ot_ref, s_scr, p_scr, mask_scr, *, n_iters):
    bf = jnp.bfloat16
    u_n = MLA_UNROLL
    tq = SEQ_TILE

    @pl.when((pl.program_id(0) == 0) & (pl.program_id(1) == 0))
    def _():
        key = lax.broadcasted_iota(jnp.int32, (tq, tq), 0)
        qry = lax.broadcasted_iota(jnp.int32, (tq, tq), 1)
        mask_scr[0] = jnp.zeros((tq, tq), jnp.float32)
        mask_scr[1] = jnp.where(key <= qry, 0.0, NEG_BIG)

    def iteration(i, half, carry, stages="CBA"):
        ms, accs, m_tiles, alphas = [list(c) for c in carry]
        new_tiles = [list(t) for t in m_tiles]
        new_alphas = [list(a) for a in alphas]

        def value_stage(g, u):
            f = i * u_n + u
            v_tile = vt_ref[0, tab_ref[1, f], g * V_EXT:(g + 1) * V_EXT, :]
            pv = _dot(v_tile, p_scr[half, g, u, :, :tq])
            acc = jnp.where(tab_ref[3, f] == 1, pv, alphas[g][u] * accs[g] + pv)
            ot_ref[0, tab_ref[0, f], g * MLA_V:(g + 1) * MLA_V, :] = (
                acc[:MLA_V] * (1.0 / acc[MLA_V:MLA_V + 1])).astype(bf)
            accs[g] = acc

        def exp_stage(g, u):
            f = (i + 1) * u_n + u
            m_in = jnp.where(tab_ref[3, f] == 1, NEG_BIG, ms[g])
            m = jnp.maximum(m_in, m_tiles[g][u])
            new_alphas[g][u] = jnp.exp2(m_in - m)
            p_scr[1 - half, g, u, :, :tq] = jnp.exp2(s_scr[g, u, :, :tq] - m).astype(bf)
            ms[g] = m

        def score_stage(g, u):
            f = (i + 2) * u_n + u
            kj = k_ref[0, g, pl.ds(pl.multiple_of(tab_ref[1, f] * tq, tq), tq), :]
            st = _dot(kj, qt_ref[0, tab_ref[0, f], g * HEAD_PAD:(g + 1) * HEAD_PAD, :])
            if u == 0:
                st = st + mask_scr[tab_ref[2, f]]
            s_scr[g, u, :, :tq] = st
            new_tiles[g][u] = jnp.max(st, axis=0, keepdims=True)

        for u in range(u_n):
            for g in range(MLA_GROUP):
                if "C" in stages:
                    value_stage(g, u)
                if "B" in stages:
                    exp_stage(g, u)
                if "A" in stages:
                    score_stage(g, u)
        return (tuple(ms), tuple(accs), tuple(tuple(t) for t in new_tiles),
                tuple(tuple(a) for a in new_alphas))

    def body(i2, carry):
        return iteration(2 * i2 + 1, 1, iteration(2 * i2, 0, carry))

    assert n_iters % 2 == 0 and n_iters >= 4
    row = mask_scr[0, 0:1, :]
    acc0 = jnp.zeros((V_EXT, tq), jnp.float32)
    carry = ((row,) * MLA_GROUP, (acc0,) * MLA_GROUP,
             ((row,) * u_n,) * MLA_GROUP, ((row,) * u_n,) * MLA_GROUP)
    carry = iteration(1, 1, iteration(0, 0, carry, "A"), "BA")
    carry = lax.fori_loop(1, n_iters // 2 - 1, body, carry)
    iteration(n_iters - 1, 1, iteration(n_iters - 2, 0, carry, "CB"), "C")


def _swa_logit_offsets(bias_ref, qpos, kpos, first_key):
    shape = (2 * BLOCK, BLOCK)
    n = jnp.maximum(qpos - kpos, 0)
    max_exact = REL_BUCKETS // 2
    nf = jnp.maximum(n, 1).astype(jnp.float32)
    large = max_exact + (jnp.log(nf / max_exact) / math.log(REL_MAX_DIST / max_exact)
                         * (REL_BUCKETS - max_exact)).astype(jnp.int32)
    bucket = jnp.where(n < max_exact, n, jnp.minimum(large, REL_BUCKETS - 1))
    ki = lax.broadcasted_iota(jnp.int32, shape, 0)
    qi = lax.broadcasted_iota(jnp.int32, shape, 1)
    delta = qi + BLOCK - ki
    valid = (delta >= 0) & (delta < WINDOW) & (ki >= first_key)
    tiles = []
    for hd in range(SWA_HEADS):
        table = jnp.broadcast_to(bias_ref[hd:hd + 1, :], shape)
        bias = jnp.take_along_axis(table, bucket, axis=1, mode="promise_in_bounds")
        tiles.append(jnp.where(valid, bias * LOG2E, NEG_BIG))
    return tiles


def _swa_store_offsets(off_scr, slot, tiles):
    groups = SWA_HEADS // SWA_KV_HEADS
    for hd, tile in enumerate(tiles):
        off_scr[slot, hd // groups, :, (hd % groups) * BLOCK:(hd % groups + 1) * BLOCK] = tile


_SWA_SLOT_RUN = 0
_SWA_SLOT_RUN_FIRST = 1
_SWA_SLOT_BLOCK = 2


def _swa_kernel(bias_ref, sink_ref, qst_ref, kp_ref, kc_ref, vp_ref, vc_ref,
                prow_ref, pprev_ref, ot_ref, off_scr):
    bf = jnp.bfloat16
    step = pl.program_id(1)
    per = SEQ_TILE // BLOCK
    groups = SWA_HEADS // SWA_KV_HEADS

    @pl.when((pl.program_id(0) == 0) & (step == 0))
    def _():
        run_q = lax.broadcasted_iota(jnp.int32, (1, BLOCK), 1) + BLOCK
        run_k = lax.broadcasted_iota(jnp.int32, (2 * BLOCK, 1), 0)
        for slot, first_key in ((_SWA_SLOT_RUN, 0), (_SWA_SLOT_RUN_FIRST, BLOCK)):
            _swa_store_offsets(off_scr, slot, _swa_logit_offsets(bias_ref, run_q, run_k, first_key))

    def attention(slot_of_block):
        def scores(j, kvh):
            q0 = j * BLOCK
            kband = (jnp.concatenate([kp_ref[0], kc_ref[0, :BLOCK]], axis=0) if j == 0
                     else kc_ref[0, q0 - BLOCK:q0 + BLOCK])
            k_h = kband[:, kvh * SWA_HEAD_DIM:(kvh + 1) * SWA_HEAD_DIM]
            qcat = jnp.concatenate(
                [qst_ref[0, 0, (kvh * groups + g) * SWA_HEAD_DIM:(kvh * groups + g + 1) * SWA_HEAD_DIM,
                         q0:q0 + BLOCK] for g in range(groups)], axis=1)
            return _dot(k_h, qcat) + off_scr[slot_of_block(j), kvh]

        def attend(j, kvh, s):
            q0 = j * BLOCK
            vband = (jnp.concatenate([vp_ref[0, 0], vc_ref[0, 0, :, :BLOCK]], axis=1) if j == 0
                     else vc_ref[0, 0, :, q0 - BLOCK:q0 + BLOCK])
            v_h = vband[kvh * SWA_V_EXT:(kvh + 1) * SWA_V_EXT]
            sink = jnp.concatenate([jnp.full((1, BLOCK), sink_ref[kvh * groups + g] * LOG2E, jnp.float32)
                                    for g in range(groups)], axis=1)
            m = jnp.maximum(jnp.max(s, axis=0, keepdims=True), sink)
            ov = _dot(v_h, jnp.exp2(s - m).astype(bf))
            denom = ov[SWA_HEAD_DIM:SWA_HEAD_DIM + 1] + jnp.exp2(sink - m)
            out = (ov[:SWA_HEAD_DIM] * (1.0 / denom)).astype(bf)
            for g in range(groups):
                hd = kvh * groups + g
                ot_ref[0, 0, hd * SWA_HEAD_DIM:(hd + 1) * SWA_HEAD_DIM, q0:q0 + BLOCK] = (
                    out[:, g * BLOCK:(g + 1) * BLOCK])

        chains = [(j, kvh) for j in range(per) for kvh in range(SWA_KV_HEADS)]
        ahead = [scores(*chain) for chain in chains[:SWA_LOOKAHEAD]]
        for c, chain in enumerate(chains):
            if c + SWA_LOOKAHEAD < len(chains):
                ahead.append(scores(*chains[c + SWA_LOOKAHEAD]))
            attend(*chain, ahead.pop(0))

    attention(lambda j: jnp.where(step == 0, _SWA_SLOT_RUN_FIRST, _SWA_SLOT_RUN) if j == 0 else _SWA_SLOT_RUN)

    pos = prow_ref[0]
    prev = pprev_ref[0]
    base = pos[:, 0:1]
    run_pos = pos - base == lax.broadcasted_iota(jnp.int32, pos.shape, 1)
    run_prev = prev - base + BLOCK == lax.broadcasted_iota(jnp.int32, prev.shape, 1)
    consecutive = (jnp.min(jnp.where(run_pos, 1.0, 0.0)) > 0.5) & (
        (jnp.min(jnp.where(run_prev, 1.0, 0.0)) > 0.5) | (step == 0))

    @pl.when(jnp.logical_not(consecutive))
    def _():
        band = jnp.concatenate([prev, pos], axis=1)
        for j in range(per):
            keys = band[:, j * BLOCK:(j + 2) * BLOCK]
            kpos = jnp.broadcast_to(keys, (2 * BLOCK, 2 * BLOCK)).T[:, 0:1]
            first_key = jnp.where(step == 0, BLOCK, 0) if j == 0 else 0
            tiles = _swa_logit_offsets(bias_ref, pos[:, j * BLOCK:(j + 1) * BLOCK], kpos, first_key)
            _swa_store_offsets(off_scr, _SWA_SLOT_BLOCK + j, tiles)
        attention(lambda j: _SWA_SLOT_BLOCK + j)


def _out_kernel(oat_ref, obt_ref, gt_ref, x_ref, wo_ref, fn_ref, o_ref):
    tm = SEQ_TILE

    def gated(t):
        h = gt_ref[0, t].astype(jnp.float32) * 0.5
        silu = (h * (1.0 + jnp.tanh(h))).astype(jnp.bfloat16)
        return jnp.concatenate([oat_ref[0, t], obt_ref[0, t]], axis=0) * silu

    def finish(t, proj):
        z = x_ref[0, t * tm:(t + 1) * tm] + proj
        o_ref[0, t * tm:(t + 1) * tm] = _rms_lanes(z, fn_ref[...])

    proj = _dot_tn(gated(0), wo_ref[...])
    for t in range(1, OUT_TILES):
        y_next = gated(t)
        finish(t - 1, proj)
        proj = _dot_tn(y_next, wo_ref[...])
    finish(OUT_TILES - 1, proj)


def _const_spec(shape):
    return pl.BlockSpec(shape, lambda *_: (0,) * len(shape))


def _params(n_axes):
    return pltpu.CompilerParams(dimension_semantics=("arbitrary",) * n_axes,
                                vmem_limit_bytes=VMEM_LIMIT)


def kernel(x, positions, norm_gain, w_in, q_a_norm, w_q_b, kv_a_norm, w_kv_b, sinks, rel_bias, w_out,
           final_norm):
    bf, f32 = jnp.bfloat16, jnp.float32
    batch, seq, _ = x.shape
    assert norm_gain.shape[0] == 1, "single-layer kernel"
    tm = SEQ_TILE
    nt = seq // tm
    per = tm // BLOCK

    w = w_in[0]
    c = 0
    w_qlat, c = w[:, c:c + MLA_Q_LORA], c + MLA_Q_LORA
    w_kvlat, c = w[:, c:c + MLA_KV_LORA], c + MLA_KV_LORA
    w_krope, c = w[:, c:c + MLA_ROPE], c + MLA_ROPE
    w_qs, c = w[:, c:c + SWA_WIDTH], c + SWA_WIDTH
    w_ks, c = w[:, c:c + SWA_KV_WIDTH], c + SWA_KV_WIDTH
    w_vs, c = w[:, c:c + SWA_KV_WIDTH], c + SWA_KV_WIDTH
    w_gate = w[:, c:c + MIX_WIDTH]
    wt = jnp.concatenate([w_qlat, w_kvlat, w_qs, w_vs, w_gate, w_krope], axis=1).T.astype(bf)
    ws = jnp.concatenate([w_kvlat, w_ks], axis=1).astype(bf)

    wq = w_q_b[0].T.reshape(MLA_HEADS, MLA_QK, MLA_Q_LORA)
    wqt = jnp.pad(wq, ((0, 0), (0, HEAD_PAD - MLA_QK), (0, 0))).reshape(MLA_HEADS * HEAD_PAD, MLA_Q_LORA)
    wkv = w_kv_b[0].reshape(MLA_KV_LORA, MLA_HEADS, MLA_NOPE + MLA_V)
    wk = jnp.pad(wkv[:, :, :MLA_NOPE], ((0, 0), (0, 0), (0, HEAD_PAD - MLA_NOPE)))
    wk = wk.reshape(MLA_KV_LORA, MLA_HEADS * HEAD_PAD).astype(bf)
    wvt = wkv[:, :, MLA_NOPE:].reshape(MLA_KV_LORA, MLA_WIDTH).T.astype(bf)
    wqt = wqt.astype(bf)

    inv = ROPE_THETA ** (-jnp.arange(ROPE_HALF, dtype=f32) / ROPE_HALF)
    inv_col = inv.reshape(ROPE_HALF, 1)
    pos_row = positions.reshape(batch, 1, seq)

    tile4 = lambda rows: pl.BlockSpec((1, 1, rows, tm), lambda b, i: (b, i, 0, 0))
    ptiles = lambda rows: pl.BlockSpec((1, PROJ_TILES, rows, tm), lambda b, i: (b, i, 0, 0))
    ptm = PROJ_TILES * tm
    qt, k, vt, qst, ks, vst, gt = pl.pallas_call(
        _proj_kernel,
        grid=(batch, nt // PROJ_TILES),
        in_specs=[
            pl.BlockSpec((1, ptm, D_MODEL), lambda b, i: (b, i, 0)),
            pl.BlockSpec((1, 1, ptm), lambda b, i: (b, 0, i)),
            _const_spec((1, D_MODEL)),
            _const_spec((_T_ROWS, D_MODEL)),
            _const_spec((D_MODEL, _S_COLS)),
            _const_spec((MLA_Q_LORA, 1)),
            _const_spec((MLA_KV_LORA, 1)),
            _const_spec((1, MLA_KV_LORA)),
            _const_spec((MLA_HEADS * HEAD_PAD, MLA_Q_LORA)),
            _const_spec((MLA_WIDTH, MLA_KV_LORA)),
            _const_spec((MLA_KV_LORA, MLA_HEADS * HEAD_PAD)),
            _const_spec((ROPE_HALF, 1)),
        ],
        out_specs=[
            ptiles(MLA_HEADS * HEAD_PAD),
            pl.BlockSpec((1, MLA_HEADS, ptm, HEAD_PAD), lambda b, i: (b, 0, i, 0)),
            ptiles(MLA_HEADS * V_EXT),
            ptiles(SWA_WIDTH),
            pl.BlockSpec((1, ptm, SWA_KV_WIDTH), lambda b, i: (b, i, 0)),
            ptiles(SWA_KV_HEADS * SWA_V_EXT),
            ptiles(MIX_WIDTH),
        ],
        out_shape=[
            jax.ShapeDtypeStruct((batch, nt, MLA_HEADS * HEAD_PAD, tm), bf),
            jax.ShapeDtypeStruct((batch, MLA_HEADS, seq, HEAD_PAD), bf),
            jax.ShapeDtypeStruct((batch, nt, MLA_HEADS * V_EXT, tm), bf),
            jax.ShapeDtypeStruct((batch, nt, SWA_WIDTH, tm), bf),
            jax.ShapeDtypeStruct((batch, seq, SWA_KV_WIDTH), bf),
            jax.ShapeDtypeStruct((batch, nt, SWA_KV_HEADS * SWA_V_EXT, tm), bf),
            jax.ShapeDtypeStruct((batch, nt, MIX_WIDTH, tm), bf),
        ],
        compiler_params=_params(2),
        name="proj",
    )(x, pos_row, norm_gain[0].reshape(1, D_MODEL), wt, ws,
      q_a_norm[0].reshape(MLA_Q_LORA, 1), kv_a_norm[0].reshape(MLA_KV_LORA, 1),
      kv_a_norm[0].reshape(1, MLA_KV_LORA), wqt, wvt, wk, inv_col)

    schedule, n_iters = _mla_schedule(nt)
    oat = pl.pallas_call(
        functools.partial(_mla_kernel, n_iters=n_iters),
        grid=(batch, MLA_HEADS // MLA_GROUP),
        in_specs=[
            pl.BlockSpec(memory_space=pltpu.SMEM),
            pl.BlockSpec((1, nt, MLA_GROUP * HEAD_PAD, tm), lambda b, h: (b, 0, h, 0)),
            pl.BlockSpec((1, MLA_GROUP, seq, HEAD_PAD), lambda b, h: (b, h, 0, 0)),
            pl.BlockSpec((1, nt, MLA_GROUP * V_EXT, tm), lambda b, h: (b, 0, h, 0)),
        ],
        out_specs=pl.BlockSpec((1, nt, MLA_GROUP * MLA_V, tm), lambda b, h: (b, 0, h, 0)),
        out_shape=jax.ShapeDtypeStruct((batch, nt, MLA_WIDTH, tm), bf),
        scratch_shapes=[pltpu.VMEM((MLA_GROUP, MLA_UNROLL, tm, tm + MLA_LANE_PAD), f32),
                        pltpu.VMEM((2, MLA_GROUP, MLA_UNROLL, tm, tm + MLA_LANE_PAD), bf),
                        pltpu.VMEM((2, tm, tm), f32)],
        compiler_params=_params(2),
        name="mla",
    )(jnp.asarray(schedule), qt, k, vt)

    prev = lambda i: jnp.maximum(i * per - 1, 0)
    bias_t = jnp.pad(rel_bias.T.astype(f32), ((0, 0), (0, LANES - REL_BUCKETS)))
    obt = pl.pallas_call(
        _swa_kernel,
        grid=(batch, nt),
        in_specs=[
            _const_spec((SWA_HEADS, LANES)),
            pl.BlockSpec(memory_space=pltpu.SMEM),
            tile4(SWA_WIDTH),
            pl.BlockSpec((1, BLOCK, SWA_KV_WIDTH), lambda b, i: (b, prev(i), 0)),
            pl.BlockSpec((1, tm, SWA_KV_WIDTH), lambda b, i: (b, i, 0)),
            pl.BlockSpec((1, 1, SWA_KV_HEADS * SWA_V_EXT, BLOCK),
                         lambda b, i: (b, prev(i) // per, 0, prev(i) % per)),
            tile4(SWA_KV_HEADS * SWA_V_EXT),
            pl.BlockSpec((1, 1, tm), lambda b, i: (b, 0, i)),
            pl.BlockSpec((1, 1, BLOCK), lambda b, i: (b, 0, prev(i))),
        ],
        out_specs=tile4(SWA_WIDTH),
        out_shape=jax.ShapeDtypeStruct((batch, nt, SWA_WIDTH, tm), bf),
        scratch_shapes=[pltpu.VMEM((_SWA_SLOT_BLOCK + per, SWA_KV_HEADS, 2 * BLOCK,
                                    (SWA_HEADS // SWA_KV_HEADS) * BLOCK), f32)],
        compiler_params=_params(2),
        name="swa",
    )(bias_t, sinks[0].astype(f32), qst, ks, ks, vst, vst, pos_row, pos_row)

    tiles = lambda rows: pl.BlockSpec((1, OUT_TILES, rows, tm), lambda b, i: (b, i, 0, 0))
    return pl.pallas_call(
        _out_kernel,
        grid=(batch, nt // OUT_TILES),
        in_specs=[
            tiles(MLA_WIDTH),
            tiles(SWA_WIDTH),
            tiles(MIX_WIDTH),
            pl.BlockSpec((1, OUT_TILES * tm, D_MODEL), lambda b, i: (b, i, 0)),
            _const_spec((MIX_WIDTH, D_MODEL)),
            _const_spec((1, D_MODEL)),
        ],
        out_specs=pl.BlockSpec((1, OUT_TILES * tm, D_MODEL), lambda b, i: (b, i, 0)),
        out_shape=jax.ShapeDtypeStruct((batch, seq, D_MODEL), x.dtype),
        compiler_params=_params(2),
        name="out",
    )(oat, obt, gt, x, w_out[0].astype(bf), final_norm.reshape(1, D_MODEL))
```

```python
import functools
import math

import jax
import numpy as np
import jax.numpy as jnp
from jax import lax
from jax.experimental import pallas as pl
from jax.experimental.pallas import tpu as pltpu

D_MODEL = 1024
MLA_HEADS = 8
MLA_Q_LORA = 256
MLA_KV_LORA = 128
MLA_NOPE = 64
MLA_ROPE = 32
MLA_V = 64
SWA_HEADS = 8
SWA_KV_HEADS = 2
SWA_HEAD_DIM = 64
WINDOW = 128
BLOCK = 128
REL_BUCKETS = 32
REL_MAX_DIST = 128
ROPE_THETA = 10000.0
EPS = 1e-6

MLA_QK = MLA_NOPE + MLA_ROPE
MLA_WIDTH = MLA_HEADS * MLA_V
SWA_WIDTH = SWA_HEADS * SWA_HEAD_DIM
SWA_KV_WIDTH = SWA_KV_HEADS * SWA_HEAD_DIM
MIX_WIDTH = MLA_WIDTH + SWA_WIDTH
ROPE_HALF = MLA_ROPE // 2

LANES = 128
HEAD_PAD = LANES
BF16_ROWS = 16
V_EXT = MLA_V + BF16_ROWS
SWA_V_EXT = SWA_HEAD_DIM + BF16_ROWS
LOG2E = math.log2(math.e)
SEQ_TILE = 512
MLA_UNROLL = 4
MLA_STAGES = 3
MLA_GROUP = 2
MLA_LANE_PAD = LANES
PROJ_TILES = 2
OUT_TILES = 2
SWA_LOOKAHEAD = 3
NEG_BIG = -1e30
VMEM_LIMIT = 56 * 1024 * 1024

_T_QLAT = 0
_T_KVLAT = _T_QLAT + MLA_Q_LORA
_T_QS = _T_KVLAT + MLA_KV_LORA
_T_VS = _T_QS + SWA_WIDTH
_T_GATE = _T_VS + SWA_KV_WIDTH
_T_KROPE = _T_GATE + MIX_WIDTH
_T_ROWS = _T_KROPE + MLA_ROPE
_S_KVLAT = 0
_S_KS = LANES
_S_COLS = 2 * LANES


def _rms_rows(v, gain_col):
    ms = jnp.mean(v * v, axis=0, keepdims=True)
    return v * lax.rsqrt(ms + EPS) * gain_col


def _rms_lanes(v, gain_row):
    ms = jnp.mean(v * v, axis=-1, keepdims=True)
    return v * lax.rsqrt(ms + EPS) * gain_row


def _dot(a, b):
    return jnp.dot(a, b, preferred_element_type=jnp.float32)


def _dot_nt(a, b):
    return lax.dot_general(a, b, (((1,), (1,)), ((), ())), preferred_element_type=jnp.float32)


def _dot_tn(a, b):
    return lax.dot_general(a, b, (((0,), (0,)), ((), ())), preferred_element_type=jnp.float32)


def _proj_kernel(x_ref, prow_ref, g1_ref, wt_ref, ws_ref, gq_ref, gkvc_ref, gkvr_ref,
                 wqt_ref, wvt_ref, wk_ref, invc_ref,
                 qt_ref, k_ref, vt_ref, qst_ref, ks_ref, vst_ref, gt_ref):
    bf = jnp.bfloat16
    part = x_ref.shape[1] // PROJ_TILES

    def normed(p):
        return _rms_lanes(x_ref[0, p * part:(p + 1) * part], g1_ref[...]).astype(bf)

    def project(h):
        return _dot_nt(wt_ref[...], h), _dot(h, ws_ref[...])

    def finish(p, pt, ps):
        tok = slice(p * part, (p + 1) * part)
        qn = _rms_rows(pt[_T_QLAT:_T_KVLAT], gq_ref[...]).astype(bf)
        qt = _dot(wqt_ref[...], qn) * (MLA_QK ** -0.5 * LOG2E)
        ang_t = invc_ref[...] * prow_ref[0, :, tok].astype(jnp.float32)
        cos_t, sin_t = jnp.cos(ang_t), jnp.sin(ang_t)
        head_pad = jnp.zeros((HEAD_PAD - MLA_QK, part), bf)
        for hd in range(MLA_HEADS):
            s0 = hd * MLA_QK
            r0 = hd * HEAD_PAD
            x1 = qt[s0 + MLA_NOPE:s0 + MLA_NOPE + ROPE_HALF]
            x2 = qt[s0 + MLA_NOPE + ROPE_HALF:s0 + MLA_QK]
            qt_ref[0, p, r0:r0 + MLA_NOPE, :] = qt[s0:s0 + MLA_NOPE].astype(bf)
            qt_ref[0, p, r0 + MLA_NOPE:r0 + MLA_NOPE + ROPE_HALF, :] = (x1 * cos_t - x2 * sin_t).astype(bf)
            qt_ref[0, p, r0 + MLA_NOPE + ROPE_HALF:r0 + MLA_QK, :] = (x2 * cos_t + x1 * sin_t).astype(bf)
            qt_ref[0, p, r0 + MLA_QK:r0 + HEAD_PAD, :] = head_pad

        kvn_t = _rms_rows(pt[_T_KVLAT:_T_QS], gkvc_ref[...]).astype(bf)
        vt = _dot(wvt_ref[...], kvn_t).astype(bf)
        ones_row = (lax.broadcasted_iota(jnp.int32, (BF16_ROWS, part), 0) == 0).astype(bf)
        for hd in range(MLA_HEADS):
            vt_ref[0, p, hd * V_EXT:hd * V_EXT + MLA_V, :] = vt[hd * MLA_V:(hd + 1) * MLA_V]
            vt_ref[0, p, hd * V_EXT + MLA_V:(hd + 1) * V_EXT, :] = ones_row
        kvn_s = _rms_lanes(ps[:, _S_KVLAT:_S_KS], gkvr_ref[...]).astype(bf)
        kn = _dot(kvn_s, wk_ref[...])
        y1 = pt[_T_KROPE:_T_KROPE + ROPE_HALF]
        y2 = pt[_T_KROPE + ROPE_HALF:_T_ROWS]
        kpe_t = jnp.concatenate([jnp.zeros((MLA_NOPE, part), jnp.float32),
                                 y1 * cos_t - y2 * sin_t, y2 * cos_t + y1 * sin_t,
                                 jnp.zeros((HEAD_PAD - MLA_QK, part), jnp.float32)], axis=0)
        kpe = kpe_t.T
        for hd in range(MLA_HEADS):
            k_ref[0, hd, tok] = (kn[:, hd * HEAD_PAD:(hd + 1) * HEAD_PAD] + kpe).astype(bf)

        qst_ref[0, p] = (pt[_T_QS:_T_VS] * (SWA_HEAD_DIM ** -0.5 * LOG2E)).astype(bf)
        ks_ref[0, tok] = ps[:, _S_KS:_S_COLS].astype(bf)
        for hd in range(SWA_KV_HEADS):
            vst_ref[0, p, hd * SWA_V_EXT:hd * SWA_V_EXT + SWA_HEAD_DIM, :] = (
                pt[_T_VS + hd * SWA_HEAD_DIM:_T_VS + (hd + 1) * SWA_HEAD_DIM].astype(bf))
            vst_ref[0, p, hd * SWA_V_EXT + SWA_HEAD_DIM:(hd + 1) * SWA_V_EXT, :] = ones_row
        gt_ref[0, p] = pt[_T_GATE:_T_KROPE].astype(bf)

    pt, ps = project(normed(0))
    for p in range(1, PROJ_TILES):
        h_next = normed(p)
        finish(p - 1, pt, ps)
        pt, ps = project(h_next)
    finish(PROJ_TILES - 1, pt, ps)


def _mla_schedule(n_tiles):
    rows, todo = [], list(range(n_tiles))
    while todo:
        gap = -len(rows) % MLA_UNROLL
        q = min(t for t in todo if t + 1 > gap)
        todo.remove(q)
        keys = list(range(q))
        keys.insert(gap, q)
        rows += [(q, kt, int(kt == q), int(j == 0)) for j, kt in enumerate(keys)]
    assert len(rows) % MLA_UNROLL == 0 and all(r[2] == 0 for j, r in enumerate(rows) if j % MLA_UNROLL)
    pad = (MLA_STAGES - 1) * MLA_UNROLL
    n_iters = len(rows) // MLA_UNROLL + (MLA_STAGES - 1)
    extra = (n_iters % 2) * MLA_UNROLL
    rows = [rows[0]] * (pad + extra) + rows + [rows[-1]] * pad
    return np.asarray(rows, np.int32).T.copy(), n_iters + n_iters % 2


def _mla_kernel(tab_ref, qt_ref, k_ref, vt_ref, ot_ref, s_scr, p_scr, mask_scr, *, n_iters):
    bf = jnp.bfloat16
    u_n = MLA_UNROLL
    tq = SEQ_TILE

    @pl.when((pl.program_id(0) == 0) & (pl.program_id(1) == 0))
    def _():
        key = lax.broadcasted_iota(jnp.int32, (tq, tq), 0)
        qry = lax.broadcasted_iota(jnp.int32, (tq, tq), 1)
        mask_scr[0] = jnp.zeros((tq, tq), jnp.float32)
        mask_scr[1] = jnp.where(key <= qry, 0.0, NEG_BIG)

    def iteration(i, half, carry, stages="CBA"):
        ms, accs, m_tiles, alphas = [list(c) for c in carry]
        new_tiles = [list(t) for t in m_tiles]
        new_alphas = [list(a) for a in alphas]

        def value_stage(g, u):
            f = i * u_n + u
            v_tile = vt_ref[0, tab_ref[1, f], g * V_EXT:(g + 1) * V_EXT, :]
            pv = _dot(v_tile, p_scr[half, g, u, :, :tq])
            acc = jnp.where(tab_ref[3, f] == 1, pv, alphas[g][u] * accs[g] + pv)
            ot_ref[0, tab_ref[0, f], g * MLA_V:(g + 1) * MLA_V, :] = (
                acc[:MLA_V] * (1.0 / acc[MLA_V:MLA_V + 1])).astype(bf)
            accs[g] = acc

        def exp_stage(g, u):
            f = (i + 1) * u_n + u
            m_in = jnp.where(tab_ref[3, f] == 1, NEG_BIG, ms[g])
            m = jnp.maximum(m_in, m_tiles[g][u])
            new_alphas[g][u] = jnp.exp2(m_in - m)
            p_scr[1 - half, g, u, :, :tq] = jnp.exp2(s_scr[g, u, :, :tq] - m).astype(bf)
            ms[g] = m

        def score_stage(g, u):
            f = (i + 2) * u_n + u
            kj = k_ref[0, g, pl.ds(pl.multiple_of(tab_ref[1, f] * tq, tq), tq), :]
            st = _dot(kj, qt_ref[0, tab_ref[0, f], g * HEAD_PAD:(g + 1) * HEAD_PAD, :])
            if u == 0:
                st = st + mask_scr[tab_ref[2, f]]
            s_scr[g, u, :, :tq] = st
            new_tiles[g][u] = jnp.max(st, axis=0, keepdims=True)

        for u in range(u_n):
            for g in range(MLA_GROUP):
                if "C" in stages:
                    value_stage(g, u)
                if "B" in stages:
                    exp_stage(g, u)
                if "A" in stages:
                    score_stage(g, u)
        return (tuple(ms), tuple(accs), tuple(tuple(t) for t in new_tiles),
                tuple(tuple(a) for a in new_alphas))

    def body(i2, carry):
        return iteration(2 * i2 + 1, 1, iteration(2 * i2, 0, carry))

    assert n_iters % 2 == 0 and n_iters >= 4
    row = mask_scr[0, 0:1, :]
    acc0 = jnp.zeros((V_EXT, tq), jnp.float32)
    carry = ((row,) * MLA_GROUP, (acc0,) * MLA_GROUP,
             ((row,) * u_n,) * MLA_GROUP, ((row,) * u_n,) * MLA_GROUP)
    carry = iteration(1, 1, iteration(0, 0, carry, "A"), "BA")
    carry = lax.fori_loop(1, n_iters // 2 - 1, body, carry)
    iteration(n_iters - 1, 1, iteration(n_iters - 2, 0, carry, "CB"), "C")


def _swa_logit_offsets(bias_ref, qpos, kpos, first_key):
    shape = (2 * BLOCK, BLOCK)
    n = jnp.maximum(qpos - kpos, 0)
    max_exact = REL_BUCKETS // 2
    nf = jnp.maximum(n, 1).astype(jnp.float32)
    large = max_exact + (jnp.log(nf / max_exact) / math.log(REL_MAX_DIST / max_exact)
                         * (REL_BUCKETS - max_exact)).astype(jnp.int32)
    bucket = jnp.where(n < max_exact, n, jnp.minimum(large, REL_BUCKETS - 1))
    ki = lax.broadcasted_iota(jnp.int32, shape, 0)
    qi = lax.broadcasted_iota(jnp.int32, shape, 1)
    delta = qi + BLOCK - ki
    valid = (delta >= 0) & (delta < WINDOW) & (ki >= first_key)
    tiles = []
    for hd in range(SWA_HEADS):
        table = jnp.broadcast_to(bias_ref[hd:hd + 1, :], shape)
        bias = jnp.take_along_axis(table, bucket, axis=1, mode="promise_in_bounds")
        tiles.append(jnp.where(valid, bias * LOG2E, NEG_BIG))
    return tiles


def _swa_store_offsets(off_scr, slot, tiles):
    groups = SWA_HEADS // SWA_KV_HEADS
    for hd, tile in enumerate(tiles):
        off_scr[slot, hd // groups, :, (hd % groups) * BLOCK:(hd % groups + 1) * BLOCK] = tile


_SWA_SLOT_RUN = 0
_SWA_SLOT_RUN_FIRST = 1
_SWA_SLOT_BLOCK = 2


def _swa_kernel(bias_ref, sink_ref, qst_ref, kp_ref, kc_ref, vp_ref, vc_ref,
                prow_ref, pprev_ref, ot_ref, off_scr):
    bf = jnp.bfloat16
    step = pl.program_id(1)
    per = SEQ_TILE // BLOCK
    groups = SWA_HEADS // SWA_KV_HEADS

    @pl.when((pl.program_id(0) == 0) & (step == 0))
    def _():
        run_q = lax.broadcasted_iota(jnp.int32, (1, BLOCK), 1) + BLOCK
        run_k = lax.broadcasted_iota(jnp.int32, (2 * BLOCK, 1), 0)
        for slot, first_key in ((_SWA_SLOT_RUN, 0), (_SWA_SLOT_RUN_FIRST, BLOCK)):
            _swa_store_offsets(off_scr, slot, _swa_logit_offsets(bias_ref, run_q, run_k, first_key))

    def attention(slot_of_block):
        def scores(j, kvh):
            q0 = j * BLOCK
            kband = (jnp.concatenate([kp_ref[0], kc_ref[0, :BLOCK]], axis=0) if j == 0
                     else kc_ref[0, q0 - BLOCK:q0 + BLOCK])
            k_h = kband[:, kvh * SWA_HEAD_DIM:(kvh + 1) * SWA_HEAD_DIM]
            qcat = jnp.concatenate(
                [qst_ref[0, 0, (kvh * groups + g) * SWA_HEAD_DIM:(kvh * groups + g + 1) * SWA_HEAD_DIM,
                         q0:q0 + BLOCK] for g in range(groups)], axis=1)
            return _dot(k_h, qcat) + off_scr[slot_of_block(j), kvh]

        def attend(j, kvh, s):
            q0 = j * BLOCK
            vband = (jnp.concatenate([vp_ref[0, 0], vc_ref[0, 0, :, :BLOCK]], axis=1) if j == 0
                     else vc_ref[0, 0, :, q0 - BLOCK:q0 + BLOCK])
            v_h = vband[kvh * SWA_V_EXT:(kvh + 1) * SWA_V_EXT]
            sink = jnp.concatenate([jnp.full((1, BLOCK), sink_ref[kvh * groups + g] * LOG2E, jnp.float32)
                                    for g in range(groups)], axis=1)
            m = jnp.maximum(jnp.max(s, axis=0, keepdims=True), sink)
            ov = _dot(v_h, jnp.exp2(s - m).astype(bf))
            denom = ov[SWA_HEAD_DIM:SWA_HEAD_DIM + 1] + jnp.exp2(sink - m)
            out = (ov[:SWA_HEAD_DIM] * (1.0 / denom)).astype(bf)
            for g in range(groups):
                hd = kvh * groups + g
                ot_ref[0, 0, hd * SWA_HEAD_DIM:(hd + 1) * SWA_HEAD_DIM, q0:q0 + BLOCK] = (
                    out[:, g * BLOCK:(g + 1) * BLOCK])

        chains = [(j, kvh) for j in range(per) for kvh in range(SWA_KV_HEADS)]
        ahead = [scores(*chain) for chain in chains[:SWA_LOOKAHEAD]]
        for c, chain in enumerate(chains):
            if c + SWA_LOOKAHEAD < len(chains):
                ahead.append(scores(*chains[c + SWA_LOOKAHEAD]))
            attend(*chain, ahead.pop(0))

    attention(lambda j: jnp.where(step == 0, _SWA_SLOT_RUN_FIRST, _SWA_SLOT_RUN) if j == 0 else _SWA_SLOT_RUN)

    pos = prow_ref[0]
    prev = pprev_ref[0]
    base = pos[:, 0:1]
    run_pos = pos - base == lax.broadcasted_iota(jnp.int32, pos.shape, 1)
    run_prev = prev - base + BLOCK == lax.broadcasted_iota(jnp.int32, prev.shape, 1)
    consecutive = (jnp.min(jnp.where(run_pos, 1.0, 0.0)) > 0.5) & (
        (jnp.min(jnp.where(run_prev, 1.0, 0.0)) > 0.5) | (step == 0))

    @pl.when(jnp.logical_not(consecutive))
    def _():
        band = jnp.concatenate([prev, pos], axis=1)
        for j in range(per):
            keys = band[:, j * BLOCK:(j + 2) * BLOCK]
            kpos = jnp.broadcast_to(keys, (2 * BLOCK, 2 * BLOCK)).T[:, 0:1]
            first_key = jnp.where(step == 0, BLOCK, 0) if j == 0 else 0
            tiles = _swa_logit_offsets(bias_ref, pos[:, j * BLOCK:(j + 1) * BLOCK], kpos, first_key)
            _swa_store_offsets(off_scr, _SWA_SLOT_BLOCK + j, tiles)
        attention(lambda j: _SWA_SLOT_BLOCK + j)


def _out_kernel(oat_ref, obt_ref, gt_ref, x_ref, wo_ref, fn_ref, o_ref):
    tm = SEQ_TILE

    def gated(t):
        h = gt_ref[0, t].astype(jnp.float32) * 0.5
        silu = (h * (1.0 + jnp.tanh(h))).astype(jnp.bfloat16)
        return jnp.concatenate([oat_ref[0, t], obt_ref[0, t]], axis=0) * silu

    def finish(t, proj):
        z = x_ref[0, t * tm:(t + 1) * tm] + proj
        o_ref[0, t * tm:(t + 1) * tm] = _rms_lanes(z, fn_ref[...])

    proj = _dot_tn(gated(0), wo_ref[...])
    for t in range(1, OUT_TILES):
        y_next = gated(t)
        finish(t - 1, proj)
        proj = _dot_tn(y_next, wo_ref[...])
    finish(OUT_TILES - 1, proj)


def _const_spec(shape):
    return pl.BlockSpec(shape, lambda *_: (0,) * len(shape))


def _params(n_axes):
    return pltpu.CompilerParams(dimension_semantics=("arbitrary",) * n_axes,
                                vmem_limit_bytes=VMEM_LIMIT)


def kernel(x, positions, norm_gain, w_in, q_a_norm, w_q_b, kv_a_norm, w_kv_b, sinks, rel_bias, w_out,
           final_norm):
    bf, f32 = jnp.bfloat16, jnp.float32
    batch, seq, _ = x.shape
    assert norm_gain.shape[0] == 1, "single-layer kernel"
    tm = SEQ_TILE
    nt = seq // tm
    per = tm // BLOCK

    w = w_in[0]
    c = 0
    w_qlat, c = w[:, c:c + MLA_Q_LORA], c + MLA_Q_LORA
    w_kvlat, c = w[:, c:c + MLA_KV_LORA], c + MLA_KV_LORA
    w_krope, c = w[:, c:c + MLA_ROPE], c + MLA_ROPE
    w_qs, c = w[:, c:c + SWA_WIDTH], c + SWA_WIDTH
    w_ks, c = w[:, c:c + SWA_KV_WIDTH], c + SWA_KV_WIDTH
    w_vs, c = w[:, c:c + SWA_KV_WIDTH], c + SWA_KV_WIDTH
    w_gate = w[:, c:c + MIX_WIDTH]
    wt = jnp.concatenate([w_qlat, w_kvlat, w_qs, w_vs, w_gate, w_krope], axis=1).T.astype(bf)
    ws = jnp.concatenate([w_kvlat, w_ks], axis=1).astype(bf)

    wqt = w_q_b[0].T.astype(bf)
    wkv = w_kv_b[0].reshape(MLA_KV_LORA, MLA_HEADS, MLA_NOPE + MLA_V)
    wk = jnp.pad(wkv[:, :, :MLA_NOPE], ((0, 0), (0, 0), (0, HEAD_PAD - MLA_NOPE)))
    wk = wk.reshape(MLA_KV_LORA, MLA_HEADS * HEAD_PAD).astype(bf)
    wvt = wkv[:, :, MLA_NOPE:].reshape(MLA_KV_LORA, MLA_WIDTH).T.astype(bf)

    inv = ROPE_THETA ** (-jnp.arange(ROPE_HALF, dtype=f32) / ROPE_HALF)
    inv_col = inv.reshape(ROPE_HALF, 1)
    pos_row = positions.reshape(batch, 1, seq)

    tile4 = lambda rows: pl.BlockSpec((1, 1, rows, tm), lambda b, i: (b, i, 0, 0))
    ptiles = lambda rows: pl.BlockSpec((1, PROJ_TILES, rows, tm), lambda b, i: (b, i, 0, 0))
    ptm = PROJ_TILES * tm
    qt, k, vt, qst, ks, vst, gt = pl.pallas_call(
        _proj_kernel,
        grid=(batch, nt // PROJ_TILES),
        in_specs=[
            pl.BlockSpec((1, ptm, D_MODEL), lambda b, i: (b, i, 0)),
            pl.BlockSpec((1, 1, ptm), lambda b, i: (b, 0, i)),
            _const_spec((1, D_MODEL)),
            _const_spec((_T_ROWS, D_MODEL)),
            _const_spec((D_MODEL, _S_COLS)),
            _const_spec((MLA_Q_LORA, 1)),
            _const_spec((MLA_KV_LORA, 1)),
            _const_spec((1, MLA_KV_LORA)),
            _const_spec((MLA_HEADS * MLA_QK, MLA_Q_LORA)),
            _const_spec((MLA_WIDTH, MLA_KV_LORA)),
            _const_spec((MLA_KV_LORA, MLA_HEADS * HEAD_PAD)),
            _const_spec((ROPE_HALF, 1)),
        ],
        out_specs=[
            ptiles(MLA_HEADS * HEAD_PAD),
            pl.BlockSpec((1, MLA_HEADS, ptm, HEAD_PAD), lambda b, i: (b, 0, i, 0)),
            ptiles(MLA_HEADS * V_EXT),
            ptiles(SWA_WIDTH),
            pl.BlockSpec((1, ptm, SWA_KV_WIDTH), lambda b, i: (b, i, 0)),
            ptiles(SWA_KV_HEADS * SWA_V_EXT),
            ptiles(MIX_WIDTH),
        ],
        out_shape=[
            jax.ShapeDtypeStruct((batch, nt, MLA_HEADS * HEAD_PAD, tm), bf),
            jax.ShapeDtypeStruct((batch, MLA_HEADS, seq, HEAD_PAD), bf),
            jax.ShapeDtypeStruct((batch, nt, MLA_HEADS * V_EXT, tm), bf),
            jax.ShapeDtypeStruct((batch, nt, SWA_WIDTH, tm), bf),
            jax.ShapeDtypeStruct((batch, seq, SWA_KV_WIDTH), bf),
            jax.ShapeDtypeStruct((batch, nt, SWA_KV_HEADS * SWA_V_EXT, tm), bf),
            jax.ShapeDtypeStruct((batch, nt, MIX_WIDTH, tm), bf),
        ],
        compiler_params=_params(2),
        name="proj",
    )(x, pos_row, norm_gain[0].reshape(1, D_MODEL), wt, ws,
      q_a_norm[0].reshape(MLA_Q_LORA, 1), kv_a_norm[0].reshape(MLA_KV_LORA, 1),
      kv_a_norm[0].reshape(1, MLA_KV_LORA), wqt, wvt, wk, inv_col)

    schedule, n_iters = _mla_schedule(nt)
    oat = pl.pallas_call(
        functools.partial(_mla_kernel, n_iters=n_iters),
        grid=(batch, MLA_HEADS // MLA_GROUP),
        in_specs=[
            pl.BlockSpec(memory_space=pltpu.SMEM),
            pl.BlockSpec((1, nt, MLA_GROUP * HEAD_PAD, tm), lambda b, h: (b, 0, h, 0)),
            pl.BlockSpec((1, MLA_GROUP, seq, HEAD_PAD), lambda b, h: (b, h, 0, 0)),
            pl.BlockSpec((1, nt, MLA_GROUP * V_EXT, tm), lambda b, h: (b, 0, h, 0)),
        ],
        out_specs=pl.BlockSpec((1, nt, MLA_GROUP * MLA_V, tm), lambda b, h: (b, 0, h, 0)),
        out_shape=jax.ShapeDtypeStruct((batch, nt, MLA_WIDTH, tm), bf),
        scratch_shapes=[pltpu.VMEM((MLA_GROUP, MLA_UNROLL, tm, tm + MLA_LANE_PAD), f32),
                        pltpu.VMEM((2, MLA_GROUP, MLA_UNROLL, tm, tm + MLA_LANE_PAD), bf),
                        pltpu.VMEM((2, tm, tm), f32)],
        compiler_params=_params(2),
        name="mla",
    )(jnp.asarray(schedule), qt, k, vt)

    prev = lambda i: jnp.maximum(i * per - 1, 0)
    bias_t = jnp.pad(rel_bias.T.astype(f32), ((0, 0), (0, LANES - REL_BUCKETS)))
    obt = pl.pallas_call(
        _swa_kernel,
        grid=(batch, nt),
        in_specs=[
            _const_spec((SWA_HEADS, LANES)),
            pl.BlockSpec(memory_space=pltpu.SMEM),
            tile4(SWA_WIDTH),
            pl.BlockSpec((1, BLOCK, SWA_KV_WIDTH), lambda b, i: (b, prev(i), 0)),
            pl.BlockSpec((1, tm, SWA_KV_WIDTH), lambda b, i: (b, i, 0)),
            pl.BlockSpec((1, 1, SWA_KV_HEADS * SWA_V_EXT, BLOCK),
                         lambda b, i: (b, prev(i) // per, 0, prev(i) % per)),
            tile4(SWA_KV_HEADS * SWA_V_EXT),
            pl.BlockSpec((1, 1, tm), lambda b, i: (b, 0, i)),
            pl.BlockSpec((1, 1, BLOCK), lambda b, i: (b, 0, prev(i))),
        ],
        out_specs=tile4(SWA_WIDTH),
        out_shape=jax.ShapeDtypeStruct((batch, nt, SWA_WIDTH, tm), bf),
        scratch_shapes=[pltpu.VMEM((_SWA_SLOT_BLOCK + per, SWA_KV_HEADS, 2 * BLOCK,
                                    (SWA_HEADS // SWA_KV_HEADS) * BLOCK), f32)],
        compiler_params=_params(2),
        name="swa",
    )(bias_t, sinks[0].astype(f32), qst, ks, ks, vst, vst, pos_row, pos_row)

    tiles = lambda rows: pl.BlockSpec((1, OUT_TILES, rows, tm), lambda b, i: (b, i, 0, 0))
    return pl.pallas_call(
        _out_kernel,
        grid=(batch, nt // OUT_TILES),
        in_specs=[
            tiles(MLA_WIDTH),
            tiles(SWA_WIDTH),
            tiles(MIX_WIDTH),
            pl.BlockSpec((1, OUT_TILES * tm, D_MODEL), lambda b, i: (b, i, 0)),
            _const_spec((MIX_WIDTH, D_MODEL)),
            _const_spec((1, D_MODEL)),
        ],
        out_specs=pl.BlockSpec((1, OUT_TILES * tm, D_MODEL), lambda b, i: (b, i, 0)),
        out_shape=jax.ShapeDtypeStruct((batch, seq, D_MODEL), x.dtype),
        compiler_params=_params(2),
        name="out",
    )(oat, obt, gt, x, w_out[0].astype(bf), final_norm.reshape(1, D_MODEL))
```

```python
import functools
import math

import jax
import numpy as np
import jax.numpy as jnp
from jax import lax
from jax.experimental import pallas as pl
from jax.experimental.pallas import tpu as pltpu

D_MODEL = 1024
MLA_HEADS = 8
MLA_Q_LORA = 256
MLA_KV_LORA = 128
MLA_NOPE = 64
MLA_ROPE = 32
MLA_V = 64
SWA_HEADS = 8
SWA_KV_HEADS = 2
SWA_HEAD_DIM = 64
WINDOW = 128
BLOCK = 128
REL_BUCKETS = 32
REL_MAX_DIST = 128
ROPE_THETA = 10000.0
EPS = 1e-6

MLA_QK = MLA_NOPE + MLA_ROPE
MLA_WIDTH = MLA_HEADS * MLA_V
SWA_WIDTH = SWA_HEADS * SWA_HEAD_DIM
SWA_KV_WIDTH = SWA_KV_HEADS * SWA_HEAD_DIM
MIX_WIDTH = MLA_WIDTH + SWA_WIDTH
ROPE_HALF = MLA_ROPE // 2

LANES = 128
HEAD_PAD = LANES
BF16_ROWS = 16
V_EXT = MLA_V + BF16_ROWS
SWA_V_EXT = SWA_HEAD_DIM + BF16_ROWS
LOG2E = math.log2(math.e)
SEQ_TILE = 512
MLA_UNROLL = 4
MLA_STAGES = 3
MLA_GROUP = 2
MLA_LANE_PAD = LANES
PROJ_TILES = 2
SWA_LOOKAHEAD = 3
NEG_BIG = -1e30
VMEM_LIMIT = 56 * 1024 * 1024

_T_QLAT = 0
_T_KVLAT = _T_QLAT + MLA_Q_LORA
_T_QS = _T_KVLAT + MLA_KV_LORA
_T_VS = _T_QS + SWA_WIDTH
_T_GATE = _T_VS + SWA_KV_WIDTH
_T_KROPE = _T_GATE + MIX_WIDTH
_T_ROWS = _T_KROPE + MLA_ROPE
_S_KVLAT = 0
_S_KS = LANES
_S_COLS = 2 * LANES


def _rms_rows(v, gain_col):
    ms = jnp.mean(v * v, axis=0, keepdims=True)
    return v * lax.rsqrt(ms + EPS) * gain_col


def _rms_lanes(v, gain_row):
    ms = jnp.mean(v * v, axis=-1, keepdims=True)
    return v * lax.rsqrt(ms + EPS) * gain_row


def _dot(a, b):
    return jnp.dot(a, b, preferred_element_type=jnp.float32)


def _dot_nt(a, b):
    return lax.dot_general(a, b, (((1,), (1,)), ((), ())), preferred_element_type=jnp.float32)


def _dot_tn(a, b):
    return lax.dot_general(a, b, (((0,), (0,)), ((), ())), preferred_element_type=jnp.float32)


def _proj_kernel(x_ref, prow_ref, g1_ref, wt_ref, ws_ref, gq_ref, gkvc_ref, gkvr_ref,
                 wqt_ref, wvt_ref, wk_ref, invc_ref,
                 qt_ref, k_ref, vt_ref, qst_ref, ks_ref, vst_ref, gt_ref):
    bf = jnp.bfloat16
    part = x_ref.shape[1] // PROJ_TILES

    def normed(p):
        return _rms_lanes(x_ref[0, p * part:(p + 1) * part], g1_ref[...]).astype(bf)

    def project(h):
        return _dot_nt(wt_ref[...], h), _dot(h, ws_ref[...])

    def finish(p, pt, ps):
        tok = slice(p * part, (p + 1) * part)
        qn = _rms_rows(pt[_T_QLAT:_T_KVLAT], gq_ref[...]).astype(bf)
        qt = _dot(wqt_ref[...], qn) * (MLA_QK ** -0.5 * LOG2E)
        ang_t = invc_ref[...] * prow_ref[0, :, tok].astype(jnp.float32)
        cos_t, sin_t = jnp.cos(ang_t), jnp.sin(ang_t)
        head_pad = jnp.zeros((HEAD_PAD - MLA_QK, part), bf)
        for hd in range(MLA_HEADS):
            s0 = hd * MLA_QK
            r0 = hd * HEAD_PAD
            x1 = qt[s0 + MLA_NOPE:s0 + MLA_NOPE + ROPE_HALF]
            x2 = qt[s0 + MLA_NOPE + ROPE_HALF:s0 + MLA_QK]
            qt_ref[0, p, r0:r0 + MLA_NOPE, :] = qt[s0:s0 + MLA_NOPE].astype(bf)
            qt_ref[0, p, r0 + MLA_NOPE:r0 + MLA_NOPE + ROPE_HALF, :] = (x1 * cos_t - x2 * sin_t).astype(bf)
            qt_ref[0, p, r0 + MLA_NOPE + ROPE_HALF:r0 + MLA_QK, :] = (x2 * cos_t + x1 * sin_t).astype(bf)
            qt_ref[0, p, r0 + MLA_QK:r0 + HEAD_PAD, :] = head_pad

        kvn_t = _rms_rows(pt[_T_KVLAT:_T_QS], gkvc_ref[...]).astype(bf)
        vt = _dot(wvt_ref[...], kvn_t).astype(bf)
        ones_row = (lax.broadcasted_iota(jnp.int32, (BF16_ROWS, part), 0) == 0).astype(bf)
        for hd in range(MLA_HEADS):
            vt_ref[0, p, hd * V_EXT:hd * V_EXT + MLA_V, :] = vt[hd * MLA_V:(hd + 1) * MLA_V]
            vt_ref[0, p, hd * V_EXT + MLA_V:(hd + 1) * V_EXT, :] = ones_row
        kvn_s = _rms_lanes(ps[:, _S_KVLAT:_S_KS], gkvr_ref[...]).astype(bf)
        kn = _dot(kvn_s, wk_ref[...])
        y1 = pt[_T_KROPE:_T_KROPE + ROPE_HALF]
        y2 = pt[_T_KROPE + ROPE_HALF:_T_ROWS]
        kpe_t = jnp.concatenate([jnp.zeros((MLA_NOPE, part), jnp.float32),
                                 y1 * cos_t - y2 * sin_t, y2 * cos_t + y1 * sin_t,
                                 jnp.zeros((HEAD_PAD - MLA_QK, part), jnp.float32)], axis=0)
        kpe = kpe_t.T
        for hd in range(MLA_HEADS):
            k_ref[0, hd, tok] = (kn[:, hd * HEAD_PAD:(hd + 1) * HEAD_PAD] + kpe).astype(bf)

        qst_ref[0, p] = (pt[_T_QS:_T_VS] * (SWA_HEAD_DIM ** -0.5 * LOG2E)).astype(bf)
        ks_ref[0, tok] = ps[:, _S_KS:_S_COLS].astype(bf)
        for hd in range(SWA_KV_HEADS):
            vst_ref[0, p, hd * SWA_V_EXT:hd * SWA_V_EXT + SWA_HEAD_DIM, :] = (
                pt[_T_VS + hd * SWA_HEAD_DIM:_T_VS + (hd + 1) * SWA_HEAD_DIM].astype(bf))
            vst_ref[0, p, hd * SWA_V_EXT + SWA_HEAD_DIM:(hd + 1) * SWA_V_EXT, :] = ones_row
        gt_ref[0, p] = pt[_T_GATE:_T_KROPE].astype(bf)

    pt, ps = project(normed(0))
    for p in range(1, PROJ_TILES):
        h_next = normed(p)
        finish(p - 1, pt, ps)
        pt, ps = project(h_next)
    finish(PROJ_TILES - 1, pt, ps)


def _mla_schedule(n_tiles):
    rows, todo = [], list(range(n_tiles))
    while todo:
        gap = -len(rows) % MLA_UNROLL
        q = min(t for t in todo if t + 1 > gap)
        todo.remove(q)
        keys = list(range(q))
        keys.insert(gap, q)
        rows += [(q, kt, int(kt == q), int(j == 0)) for j, kt in enumerate(keys)]
    assert len(rows) % MLA_UNROLL == 0 and all(r[2] == 0 for j, r in enumerate(rows) if j % MLA_UNROLL)
    pad = (MLA_STAGES - 1) * MLA_UNROLL
    n_iters = len(rows) // MLA_UNROLL + (MLA_STAGES - 1)
    extra = (n_iters % 2) * MLA_UNROLL
    rows = [rows[0]] * (pad + extra) + rows + [rows[-1]] * pad
    return np.asarray(rows, np.int32).T.copy(), n_iters + n_iters % 2


def _mla_kernel(tab_ref, qt_ref, k_ref, vt_ref, ot_ref, s_scr, p_scr, mask_scr, *, n_iters):
    bf = jnp.bfloat16
    u_n = MLA_UNROLL
    tq = SEQ_TILE

    @pl.when((pl.program_id(0) == 0) & (pl.program_id(1) == 0))
    def _():
        key = lax.broadcasted_iota(jnp.int32, (tq, tq), 0)
        qry = lax.broadcasted_iota(jnp.int32, (tq, tq), 1)
        mask_scr[0] = jnp.zeros((tq, tq), jnp.float32)
        mask_scr[1] = jnp.where(key <= qry, 0.0, NEG_BIG)

    def iteration(i, half, carry, stages="CBA"):
        ms, accs, m_tiles, alphas = [list(c) for c in carry]
        new_tiles = [list(t) for t in m_tiles]
        new_alphas = [list(a) for a in alphas]

        def value_stage(g, u):
            f = i * u_n + u
            v_tile = vt_ref[0, tab_ref[1, f], g * V_EXT:(g + 1) * V_EXT, :]
            pv = _dot(v_tile, p_scr[half, g, u, :, :tq])
            acc = jnp.where(tab_ref[3, f] == 1, pv, alphas[g][u] * accs[g] + pv)
            ot_ref[0, tab_ref[0, f], g * MLA_V:(g + 1) * MLA_V, :] = (
                acc[:MLA_V] * (1.0 / acc[MLA_V:MLA_V + 1])).astype(bf)
            accs[g] = acc

        def exp_stage(g, u):
            f = (i + 1) * u_n + u
            m_in = jnp.where(tab_ref[3, f] == 1, NEG_BIG, ms[g])
            m = jnp.maximum(m_in, m_tiles[g][u])
            new_alphas[g][u] = jnp.exp2(m_in - m)
            p_scr[1 - half, g, u, :, :tq] = jnp.exp2(s_scr[g, u, :, :tq] - m).astype(bf)
            ms[g] = m

        def score_stage(g, u):
            f = (i + 2) * u_n + u
            kj = k_ref[0, g, pl.ds(pl.multiple_of(tab_ref[1, f] * tq, tq), tq), :]
            st = _dot(kj, qt_ref[0, tab_ref[0, f], g * HEAD_PAD:(g + 1) * HEAD_PAD, :])
            if u == 0:
                st = st + mask_scr[tab_ref[2, f]]
            s_scr[g, u, :, :tq] = st
            new_tiles[g][u] = jnp.max(st, axis=0, keepdims=True)

        for u in range(u_n):
            for g in range(MLA_GROUP):
                if "C" in stages:
                    value_stage(g, u)
                if "B" in stages:
                    exp_stage(g, u)
                if "A" in stages:
                    score_stage(g, u)
        return (tuple(ms), tuple(accs), tuple(tuple(t) for t in new_tiles),
                tuple(tuple(a) for a in new_alphas))

    def body(i2, carry):
        return iteration(2 * i2 + 1, 1, iteration(2 * i2, 0, carry))

    assert n_iters % 2 == 0 and n_iters >= 4
    row = mask_scr[0, 0:1, :]
    acc0 = jnp.zeros((V_EXT, tq), jnp.float32)
    carry = ((row,) * MLA_GROUP, (acc0,) * MLA_GROUP,
             ((row,) * u_n,) * MLA_GROUP, ((row,) * u_n,) * MLA_GROUP)
    carry = iteration(1, 1, iteration(0, 0, carry, "A"), "BA")
    carry = lax.fori_loop(1, n_iters // 2 - 1, body, carry)
    iteration(n_iters - 1, 1, iteration(n_iters - 2, 0, carry, "CB"), "C")


def _swa_logit_offsets(bias_ref, qpos, kpos, first_key):
    shape = (2 * BLOCK, BLOCK)
    n = jnp.maximum(qpos - kpos, 0)
    max_exact = REL_BUCKETS // 2
    nf = jnp.maximum(n, 1).astype(jnp.float32)
    large = max_exact + (jnp.log(nf / max_exact) / math.log(REL_MAX_DIST / max_exact)
                         * (REL_BUCKETS - max_exact)).astype(jnp.int32)
    bucket = jnp.where(n < max_exact, n, jnp.minimum(large, REL_BUCKETS - 1))
    ki = lax.broadcasted_iota(jnp.int32, shape, 0)
    qi = lax.broadcasted_iota(jnp.int32, shape, 1)
    delta = qi + BLOCK - ki
    valid = (delta >= 0) & (delta < WINDOW) & (ki >= first_key)
    tiles = []
    for hd in range(SWA_HEADS):
        table = jnp.broadcast_to(bias_ref[hd:hd + 1, :], shape)
        bias = jnp.take_along_axis(table, bucket, axis=1, mode="promise_in_bounds")
        tiles.append(jnp.where(valid, bias * LOG2E, NEG_BIG))
    return tiles


def _swa_store_offsets(off_scr, slot, tiles):
    groups = SWA_HEADS // SWA_KV_HEADS
    for hd, tile in enumerate(tiles):
        off_scr[slot, hd // groups, :, (hd % groups) * BLOCK:(hd % groups + 1) * BLOCK] = tile


_SWA_SLOT_RUN = 0
_SWA_SLOT_RUN_FIRST = 1
_SWA_SLOT_BLOCK = 2


def _swa_kernel(bias_ref, sink_ref, qst_ref, kp_ref, kc_ref, vp_ref, vc_ref,
                prow_ref, pprev_ref, oat_ref, gt_ref, x_ref, wo_ref, fn_ref, o_ref, off_scr, ot_ref):
    bf = jnp.bfloat16
    step = pl.program_id(1)
    per = SEQ_TILE // BLOCK
    groups = SWA_HEADS // SWA_KV_HEADS

    @pl.when((pl.program_id(0) == 0) & (step == 0))
    def _():
        run_q = lax.broadcasted_iota(jnp.int32, (1, BLOCK), 1) + BLOCK
        run_k = lax.broadcasted_iota(jnp.int32, (2 * BLOCK, 1), 0)
        for slot, first_key in ((_SWA_SLOT_RUN, 0), (_SWA_SLOT_RUN_FIRST, BLOCK)):
            _swa_store_offsets(off_scr, slot, _swa_logit_offsets(bias_ref, run_q, run_k, first_key))

    def attention(slot_of_block):
        def scores(j, kvh):
            q0 = j * BLOCK
            kband = (jnp.concatenate([kp_ref[0], kc_ref[0, :BLOCK]], axis=0) if j == 0
                     else kc_ref[0, q0 - BLOCK:q0 + BLOCK])
            k_h = kband[:, kvh * SWA_HEAD_DIM:(kvh + 1) * SWA_HEAD_DIM]
            qcat = jnp.concatenate(
                [qst_ref[0, 0, (kvh * groups + g) * SWA_HEAD_DIM:(kvh * groups + g + 1) * SWA_HEAD_DIM,
                         q0:q0 + BLOCK] for g in range(groups)], axis=1)
            return _dot(k_h, qcat) + off_scr[slot_of_block(j), kvh]

        def attend(j, kvh, s):
            q0 = j * BLOCK
            vband = (jnp.concatenate([vp_ref[0, 0], vc_ref[0, 0, :, :BLOCK]], axis=1) if j == 0
                     else vc_ref[0, 0, :, q0 - BLOCK:q0 + BLOCK])
            v_h = vband[kvh * SWA_V_EXT:(kvh + 1) * SWA_V_EXT]
            sink = jnp.concatenate([jnp.full((1, BLOCK), sink_ref[kvh * groups + g] * LOG2E, jnp.float32)
                                    for g in range(groups)], axis=1)
            m = jnp.maximum(jnp.max(s, axis=0, keepdims=True), sink)
            ov = _dot(v_h, jnp.exp2(s - m).astype(bf))
            denom = ov[SWA_HEAD_DIM:SWA_HEAD_DIM + 1] + jnp.exp2(sink - m)
            out = (ov[:SWA_HEAD_DIM] * (1.0 / denom)).astype(bf)
            for g in range(groups):
                hd = kvh * groups + g
                ot_ref[0, 0, hd * SWA_HEAD_DIM:(hd + 1) * SWA_HEAD_DIM, q0:q0 + BLOCK] = (
                    out[:, g * BLOCK:(g + 1) * BLOCK])

        chains = [(j, kvh) for j in range(per) for kvh in range(SWA_KV_HEADS)]
        ahead = [scores(*chain) for chain in chains[:SWA_LOOKAHEAD]]
        for c, chain in enumerate(chains):
            if c + SWA_LOOKAHEAD < len(chains):
                ahead.append(scores(*chains[c + SWA_LOOKAHEAD]))
            attend(*chain, ahead.pop(0))

    attention(lambda j: jnp.where(step == 0, _SWA_SLOT_RUN_FIRST, _SWA_SLOT_RUN) if j == 0 else _SWA_SLOT_RUN)

    pos = prow_ref[0]
    prev = pprev_ref[0]
    base = pos[:, 0:1]
    run_pos = pos - base == lax.broadcasted_iota(jnp.int32, pos.shape, 1)
    run_prev = prev - base + BLOCK == lax.broadcasted_iota(jnp.int32, prev.shape, 1)
    consecutive = (jnp.min(jnp.where(run_pos, 1.0, 0.0)) > 0.5) & (
        (jnp.min(jnp.where(run_prev, 1.0, 0.0)) > 0.5) | (step == 0))

    @pl.when(jnp.logical_not(consecutive))
    def _():
        band = jnp.concatenate([prev, pos], axis=1)
        for j in range(per):
            keys = band[:, j * BLOCK:(j + 2) * BLOCK]
            kpos = jnp.broadcast_to(keys, (2 * BLOCK, 2 * BLOCK)).T[:, 0:1]
            first_key = jnp.where(step == 0, BLOCK, 0) if j == 0 else 0
            tiles = _swa_logit_offsets(bias_ref, pos[:, j * BLOCK:(j + 1) * BLOCK], kpos, first_key)
            _swa_store_offsets(off_scr, _SWA_SLOT_BLOCK + j, tiles)
        attention(lambda j: _SWA_SLOT_BLOCK + j)

    h = gt_ref[0, 0].astype(jnp.float32) * 0.5
    silu = (h * (1.0 + jnp.tanh(h))).astype(bf)
    yt = jnp.concatenate([oat_ref[0, 0], ot_ref[0, 0]], axis=0) * silu
    z = x_ref[0] + _dot_tn(yt, wo_ref[...])
    o_ref[0] = _rms_lanes(z, fn_ref[...])


def _const_spec(shape):
    return pl.BlockSpec(shape, lambda *_: (0,) * len(shape))


def _params(n_axes):
    return pltpu.CompilerParams(dimension_semantics=("arbitrary",) * n_axes,
                                vmem_limit_bytes=VMEM_LIMIT)


def kernel(x, positions, norm_gain, w_in, q_a_norm, w_q_b, kv_a_norm, w_kv_b, sinks, rel_bias, w_out,
           final_norm):
    bf, f32 = jnp.bfloat16, jnp.float32
    batch, seq, _ = x.shape
    assert norm_gain.shape[0] == 1, "single-layer kernel"
    tm = SEQ_TILE
    nt = seq // tm
    per = tm // BLOCK

    w = w_in[0]
    c = 0
    w_qlat, c = w[:, c:c + MLA_Q_LORA], c + MLA_Q_LORA
    w_kvlat, c = w[:, c:c + MLA_KV_LORA], c + MLA_KV_LORA
    w_krope, c = w[:, c:c + MLA_ROPE], c + MLA_ROPE
    w_qs, c = w[:, c:c + SWA_WIDTH], c + SWA_WIDTH
    w_ks, c = w[:, c:c + SWA_KV_WIDTH], c + SWA_KV_WIDTH
    w_vs, c = w[:, c:c + SWA_KV_WIDTH], c + SWA_KV_WIDTH
    w_gate = w[:, c:c + MIX_WIDTH]
    wt = jnp.concatenate([w_qlat, w_kvlat, w_qs, w_vs, w_gate, w_krope], axis=1).T.astype(bf)
    ws = jnp.concatenate([w_kvlat, w_ks], axis=1).astype(bf)

    wqt = w_q_b[0].T.astype(bf)
    wkv = w_kv_b[0].reshape(MLA_KV_LORA, MLA_HEADS, MLA_NOPE + MLA_V)
    wk = jnp.pad(wkv[:, :, :MLA_NOPE], ((0, 0), (0, 0), (0, HEAD_PAD - MLA_NOPE)))
    wk = wk.reshape(MLA_KV_LORA, MLA_HEADS * HEAD_PAD).astype(bf)
    wvt = wkv[:, :, MLA_NOPE:].reshape(MLA_KV_LORA, MLA_WIDTH).T.astype(bf)

    inv = ROPE_THETA ** (-jnp.arange(ROPE_HALF, dtype=f32) / ROPE_HALF)
    inv_col = inv.reshape(ROPE_HALF, 1)
    pos_row = positions.reshape(batch, 1, seq)

    tile4 = lambda rows: pl.BlockSpec((1, 1, rows, tm), lambda b, i: (b, i, 0, 0))
    ptiles = lambda rows: pl.BlockSpec((1, PROJ_TILES, rows, tm), lambda b, i: (b, i, 0, 0))
    ptm = PROJ_TILES * tm
    qt, k, vt, qst, ks, vst, gt = pl.pallas_call(
        _proj_kernel,
        grid=(batch, nt // PROJ_TILES),
        in_specs=[
            pl.BlockSpec((1, ptm, D_MODEL), lambda b, i: (b, i, 0)),
            pl.BlockSpec((1, 1, ptm), lambda b, i: (b, 0, i)),
            _const_spec((1, D_MODEL)),
            _const_spec((_T_ROWS, D_MODEL)),
            _const_spec((D_MODEL, _S_COLS)),
            _const_spec((MLA_Q_LORA, 1)),
            _const_spec((MLA_KV_LORA, 1)),
            _const_spec((1, MLA_KV_LORA)),
            _const_spec((MLA_HEADS * MLA_QK, MLA_Q_LORA)),
            _const_spec((MLA_WIDTH, MLA_KV_LORA)),
            _const_spec((MLA_KV_LORA, MLA_HEADS * HEAD_PAD)),
            _const_spec((ROPE_HALF, 1)),
        ],
        out_specs=[
            ptiles(MLA_HEADS * HEAD_PAD),
            pl.BlockSpec((1, MLA_HEADS, ptm, HEAD_PAD), lambda b, i: (b, 0, i, 0)),
            ptiles(MLA_HEADS * V_EXT),
            ptiles(SWA_WIDTH),
            pl.BlockSpec((1, ptm, SWA_KV_WIDTH), lambda b, i: (b, i, 0)),
            ptiles(SWA_KV_HEADS * SWA_V_EXT),
            ptiles(MIX_WIDTH),
        ],
        out_shape=[
            jax.ShapeDtypeStruct((batch, nt, MLA_HEADS * HEAD_PAD, tm), bf),
            jax.ShapeDtypeStruct((batch, MLA_HEADS, seq, HEAD_PAD), bf),
            jax.ShapeDtypeStruct((batch, nt, MLA_HEADS * V_EXT, tm), bf),
            jax.ShapeDtypeStruct((batch, nt, SWA_WIDTH, tm), bf),
            jax.ShapeDtypeStruct((batch, seq, SWA_KV_WIDTH), bf),
            jax.ShapeDtypeStruct((batch, nt, SWA_KV_HEADS * SWA_V_EXT, tm), bf),
            jax.ShapeDtypeStruct((batch, nt, MIX_WIDTH, tm), bf),
        ],
        compiler_params=_params(2),
        name="proj",
    )(x, pos_row, norm_gain[0].reshape(1, D_MODEL), wt, ws,
      q_a_norm[0].reshape(MLA_Q_LORA, 1), kv_a_norm[0].reshape(MLA_KV_LORA, 1),
      kv_a_norm[0].reshape(1, MLA_KV_LORA), wqt, wvt, wk, inv_col)

    schedule, n_iters = _mla_schedule(nt)
    oat = pl.pallas_call(
        functools.partial(_mla_kernel, n_iters=n_iters),
        grid=(batch, MLA_HEADS // MLA_GROUP),
        in_specs=[
            pl.BlockSpec(memory_space=pltpu.SMEM),
            pl.BlockSpec((1, nt, MLA_GROUP * HEAD_PAD, tm), lambda b, h: (b, 0, h, 0)),
            pl.BlockSpec((1, MLA_GROUP, seq, HEAD_PAD), lambda b, h: (b, h, 0, 0)),
            pl.BlockSpec((1, nt, MLA_GROUP * V_EXT, tm), lambda b, h: (b, 0, h, 0)),
        ],
        out_specs=pl.BlockSpec((1, nt, MLA_GROUP * MLA_V, tm), lambda b, h: (b, 0, h, 0)),
        out_shape=jax.ShapeDtypeStruct((batch, nt, MLA_WIDTH, tm), bf),
        scratch_shapes=[pltpu.VMEM((MLA_GROUP, MLA_UNROLL, tm, tm + MLA_LANE_PAD), f32),
                        pltpu.VMEM((2, MLA_GROUP, MLA_UNROLL, tm, tm + MLA_LANE_PAD), bf),
                        pltpu.VMEM((2, tm, tm), f32)],
        compiler_params=_params(2),
        name="mla",
    )(jnp.asarray(schedule), qt, k, vt)

    prev = lambda i: jnp.maximum(i * per - 1, 0)
    bias_t = jnp.pad(rel_bias.T.astype(f32), ((0, 0), (0, LANES - REL_BUCKETS)))
    return pl.pallas_call(
        _swa_kernel,
        grid=(batch, nt),
        in_specs=[
            _const_spec((SWA_HEADS, LANES)),
            pl.BlockSpec(memory_space=pltpu.SMEM),
            tile4(SWA_WIDTH),
            pl.BlockSpec((1, BLOCK, SWA_KV_WIDTH), lambda b, i: (b, prev(i), 0)),
            pl.BlockSpec((1, tm, SWA_KV_WIDTH), lambda b, i: (b, i, 0)),
            pl.BlockSpec((1, 1, SWA_KV_HEADS * SWA_V_EXT, BLOCK),
                         lambda b, i: (b, prev(i) // per, 0, prev(i) % per)),
            tile4(SWA_KV_HEADS * SWA_V_EXT),
            pl.BlockSpec((1, 1, tm), lambda b, i: (b, 0, i)),
            pl.BlockSpec((1, 1, BLOCK), lambda b, i: (b, 0, prev(i))),
            tile4(MLA_WIDTH),
            tile4(MIX_WIDTH),
            pl.BlockSpec((1, tm, D_MODEL), lambda b, i: (b, i, 0)),
            _const_spec((MIX_WIDTH, D_MODEL)),
            _const_spec((1, D_MODEL)),
        ],
        out_specs=pl.BlockSpec((1, tm, D_MODEL), lambda b, i: (b, i, 0)),
        out_shape=jax.ShapeDtypeStruct((batch, seq, D_MODEL), x.dtype),
        scratch_shapes=[pltpu.VMEM((_SWA_SLOT_BLOCK + per, SWA_KV_HEADS, 2 * BLOCK,
                                    (SWA_HEADS // SWA_KV_HEADS) * BLOCK), f32),
                        pltpu.VMEM((1, 1, SWA_WIDTH, tm), bf)],
        compiler_params=_params(2),
        name="swa_out",
    )(bias_t, sinks[0].astype(f32), qst, ks, ks, vst, vst, pos_row, pos_row,
      oat, gt, x, w_out[0].astype(bf), final_norm.reshape(1, D_MODEL))
```
